```python
import jax, jax.numpy as jnp
from jax import lax
import numpy as np

D_MODEL = 1024
BATCH = 4
SEQ = 4096
DEPTH = 1
DEC_BATCH = 16
DEC_SEQ = 32
PAST_LEN = 4096

CHUNK = 64
GLA_HEADS = 4
GLA_DK = 128
GLA_DV = 256
GLA_KEY = GLA_HEADS * GLA_DK
GLA_VAL = GLA_HEADS * GLA_DV
GLA_RANK = 16
GLA_TAU = 16.0
CONV_DIM = D_MODEL
CONV_W = 3
D_FF = -(-8 * D_MODEL // (3 * 256)) * 256
N_MOD = 6
EPS = 1e-6
IN_SIZES = (GLA_KEY, GLA_KEY, GLA_VAL, GLA_VAL, GLA_RANK, CONV_DIM, CONV_DIM, CONV_DIM, D_MODEL, D_MODEL)
IN_DIM = sum(IN_SIZES)

kernel_name = 'streaming_gla_shortconv_hybrid'


def rmsnorm(x, g):
    xf = x.astype(jnp.float32)
    xf = xf * lax.rsqrt(jnp.mean(xf * xf, axis=-1, keepdims=True) + EPS)
    return (xf * g.astype(jnp.float32)).astype(x.dtype)


def gla_recurrence(q, k, v, la, S0):
    Bn, L = q.shape[0], q.shape[1]
    blk = min(CHUNK, L)
    n = L // blk

    def to_blocks(t):
        return jnp.moveaxis(t.reshape((Bn, n, blk) + t.shape[2:]), 1, 0)

    mask = jnp.tril(jnp.ones((blk, blk), bool))[None, :, :, None, None]

    def step(S, inp):
        qb, kb, vb, lb = inp
        bcum = jnp.cumsum(lb, axis=1)
        diff = bcum[:, :, None] - bcum[:, None]
        decay = jnp.exp(jnp.where(mask, diff, -jnp.inf))
        scores = jnp.einsum('bihd,bjhd,bijhd->bhij', qb, kb, decay)
        o = (jnp.einsum('bhij,bjhv->bihv', scores, vb)
             + jnp.einsum('bihd,bhdv->bihv', qb * jnp.exp(bcum), S))
        blast = bcum[:, -1]
        S = (jnp.exp(blast)[..., None] * S
             + jnp.einsum('bjhd,bjhv->bhdv', kb * jnp.exp(blast[:, None] - bcum), vb))
        return S, o

    S, o = lax.scan(step, S0, (to_blocks(q), to_blocks(k), to_blocks(v), to_blocks(la)))
    o = jnp.moveaxis(o, 0, 1).reshape(Bn, L, GLA_HEADS, GLA_DV)
    return S, o


def causal_conv(u, prev, conv_w, conv_b):
    L = u.shape[1]
    up = jnp.concatenate([prev.astype(u.dtype), u], axis=1)
    y = up[:, 0:L] * conv_w[0] + up[:, 1:L + 1] * conv_w[1] + up[:, 2:L + 2] * conv_w[2] + conv_b
    return y, up[:, L:]


def token_mixer(h, S0, conv_prev, w_in, w_alpha, b_alpha, gla_norm_g, w_gla_out, conv_w, conv_b, w_conv_out, w_o):
    Bn, L, _ = h.shape
    z = h @ w_in
    q, k, v, g, a, cb, cc, ch, ga, gb = jnp.split(z, list(np.cumsum(IN_SIZES)[:-1]), axis=-1)
    q = q.reshape(Bn, L, GLA_HEADS, GLA_DK).astype(jnp.float32) * (GLA_DK ** -0.5)
    k = k.reshape(Bn, L, GLA_HEADS, GLA_DK).astype(jnp.float32)
    v = v.reshape(Bn, L, GLA_HEADS, GLA_DV).astype(jnp.float32)
    la = jax.nn.log_sigmoid((a @ w_alpha + b_alpha).astype(jnp.float32)) / GLA_TAU
    la = la.reshape(Bn, L, GLA_HEADS, GLA_DK)
    S_new, o = gla_recurrence(q, k, v, la, S0.astype(jnp.float32))
    o = o * lax.rsqrt(jnp.mean(o * o, axis=-1, keepdims=True) + EPS) * gla_norm_g.astype(jnp.float32)
    o = o.reshape(Bn, L, GLA_VAL).astype(h.dtype) * jax.nn.silu(g)
    y_a = o @ w_gla_out
    conv, new_buf = causal_conv(cc * ch, conv_prev, conv_w, conv_b)
    y_b = (cb * conv) @ w_conv_out
    merged = jax.nn.sigmoid(ga) * y_a + jax.nn.sigmoid(gb) * y_b
    return merged @ w_o, S_new, new_buf


def layer(x, c, S0, conv_prev, p):
    (w_mod, b_mod, norm1_g, w_in, w_alpha, b_alpha, gla_norm_g, w_gla_out,
     conv_w, conv_b, w_conv_out, w_o, norm2_g, w_ffn_in, w_ffn_out) = p
    mod = (c @ w_mod + b_mod)[:, None, :]
    sh1, sc1, g1, sh2, sc2, g2 = jnp.split(mod, N_MOD, axis=-1)
    h = rmsnorm(x, norm1_g) * (1 + sc1) + sh1
    m, S_new, buf = token_mixer(h, S0, conv_prev, w_in, w_alpha, b_alpha, gla_norm_g, w_gla_out,
                                conv_w, conv_b, w_conv_out, w_o)
    x = x + g1 * m
    h = rmsnorm(x, norm2_g) * (1 + sc2) + sh2
    gt, upv = jnp.split(h @ w_ffn_in, 2, axis=-1)
    x = x + g2 * ((jax.nn.silu(gt) * upv) @ w_ffn_out)
    return x, S_new, buf


def setup_inputs(seed: int = 0) -> dict:
    key = jax.random.key(seed)
    ks = jax.random.split(key, 24)
    f32 = jnp.float32
    D = D_MODEL

    def nrm(k, shape, scale):
        return jax.random.normal(k, shape, f32) * scale

    return {
        'x_prompt': nrm(ks[0], (BATCH, SEQ, D), 1.0),
        'x_sample': nrm(ks[1], (DEC_BATCH, DEC_SEQ, D), 1.0),
        'c_prompt': nrm(ks[2], (BATCH, D), 1.0),
        'c_sample': nrm(ks[3], (DEC_BATCH, D), 1.0),
        'state_gla': nrm(ks[4], (DEPTH, DEC_BATCH, GLA_HEADS, GLA_DK, GLA_DV), 0.5),
        'cache_conv': nrm(ks[5], (DEPTH, DEC_BATCH, CONV_W - 1, CONV_DIM), 1.0),
        'w_mod': nrm(ks[6], (DEPTH, D, N_MOD * D), 0.5 * D ** -0.5),
        'b_mod': nrm(ks[7], (DEPTH, N_MOD * D), 0.02),
        'norm1_g': 1.0 + nrm(ks[8], (DEPTH, D), 0.02),
        'w_in': nrm(ks[9], (DEPTH, D, IN_DIM), D ** -0.5),
        'w_alpha': nrm(ks[10], (DEPTH, GLA_RANK, GLA_KEY), GLA_RANK ** -0.5),
        'b_alpha': nrm(ks[11], (DEPTH, GLA_KEY), 0.02),
        'gla_norm_g': 1.0 + nrm(ks[12], (DEPTH, GLA_DV), 0.02),
        'w_gla_out': nrm(ks[13], (DEPTH, GLA_VAL, D), GLA_VAL ** -0.5),
        'conv_w': nrm(ks[14], (DEPTH, CONV_W, CONV_DIM), CONV_W ** -0.5),
        'conv_b': nrm(ks[15], (DEPTH, CONV_DIM), 0.02),
        'w_conv_out': nrm(ks[16], (DEPTH, CONV_DIM, D), CONV_DIM ** -0.5),
        'w_o': nrm(ks[17], (DEPTH, D, D), D ** -0.5),
        'norm2_g': 1.0 + nrm(ks[18], (DEPTH, D), 0.02),
        'w_ffn_in': nrm(ks[19], (DEPTH, D, 2 * D_FF), D ** -0.5),
        'w_ffn_out': nrm(ks[20], (DEPTH, D_FF, D), D_FF ** -0.5),
        'norm_f_g': 1.0 + nrm(ks[21], (D,), 0.02),
    }


def reference(x_prompt, x_sample, c_prompt, c_sample, state_gla, cache_conv, w_mod, b_mod, norm1_g,
              w_in, w_alpha, b_alpha, gla_norm_g, w_gla_out, conv_w, conv_b, w_conv_out, w_o,
              norm2_g, w_ffn_in, w_ffn_out, norm_f_g):
    Bp = x_prompt.shape[0]
    yp, ys = x_prompt, x_sample
    sp_list, cp_list, ss_list, cs_list = [], [], [], []
    for l in range(DEPTH):
        p = (w_mod[l], b_mod[l], norm1_g[l], w_in[l], w_alpha[l], b_alpha[l], gla_norm_g[l],
             w_gla_out[l], conv_w[l], conv_b[l], w_conv_out[l], w_o[l], norm2_g[l],
             w_ffn_in[l], w_ffn_out[l])
        S0_p = jnp.zeros((Bp, GLA_HEADS, GLA_DK, GLA_DV), jnp.float32)
        buf0_p = jnp.zeros((Bp, CONV_W - 1, CONV_DIM), x_prompt.dtype)
        yp, sp, cp = layer(yp, c_prompt, S0_p, buf0_p, p)
        ys, ss, cs = layer(ys, c_sample, state_gla[l], cache_conv[l], p)
        sp_list.append(sp)
        cp_list.append(cp)
        ss_list.append(ss)
        cs_list.append(cs)
    y_prompt = rmsnorm(yp, norm_f_g)
    y_sample = rmsnorm(ys, norm_f_g)
    state_gla_p = jnp.stack(sp_list)
    cache_conv_p = jnp.stack(cp_list)
    state_gla_s = jnp.stack(ss_list)
    cache_conv_s = jnp.stack(cs_list)
    return (y_prompt, y_sample, state_gla_p, cache_conv_p, state_gla_s, cache_conv_s)
```

```python
import functools

import jax
import jax.numpy as jnp
from jax import lax
from jax.experimental import pallas as pl
from jax.experimental.pallas import tpu as pltpu

F32 = jnp.float32
BF16 = jnp.bfloat16

D_MODEL = 1024
GLA_HEADS = 4
GLA_DK = 128
GLA_DV = 256
GLA_KEY = GLA_HEADS * GLA_DK
GLA_VAL = GLA_HEADS * GLA_DV
GLA_RANK = 16
GLA_TAU = 16.0
CONV_DIM = D_MODEL
CONV_W = 3
D_FF = 2816
N_MOD = 6
EPS = 1e-6

LANES = 128
SUBLANES = 8
RANK_PAD = LANES

OFF_Q = 0
OFF_K = OFF_Q + GLA_KEY
OFF_V = OFF_K + GLA_KEY
OFF_G = OFF_V + GLA_VAL
OFF_A = OFF_G + GLA_VAL
OFF_CB = OFF_A + RANK_PAD
OFF_CC = OFF_CB + CONV_DIM
OFF_CH = OFF_CC + CONV_DIM
OFF_GA = OFF_CH + CONV_DIM
OFF_GB = OFF_GA + D_MODEL
IN_PAD = OFF_GB + D_MODEL

SAFE_LOG_DECAY = 60.0

VMEM_LIMIT = 60 * 1024 * 1024


def _dot(a, b):
    return jnp.dot(a, b, preferred_element_type=F32)


def _dot_nt(a, b):
    return lax.dot_general(a, b, (((1,), (1,)), ((), ())), preferred_element_type=F32)


def _dot_tn(a, b):
    return lax.dot_general(a, b, (((0,), (0,)), ((), ())), preferred_element_type=F32)


def _sigmoid(x):
    return 1.0 / (1.0 + jnp.exp(-x))


def _mod_kernel(c_ref, w_ref, b_ref, o_ref):
    o_ref[...] = _dot(c_ref[...].astype(BF16), w_ref[...].astype(BF16)) + b_ref[...]


def _modulation(c_all, w_mod, b_mod):
    n = c_all.shape[0]
    nblk = N_MOD
    return pl.pallas_call(
        _mod_kernel,
        grid=(nblk,),
        in_specs=[
            pl.BlockSpec((n, D_MODEL), lambda j: (0, 0)),
            pl.BlockSpec((D_MODEL, D_MODEL), lambda j: (0, j)),
            pl.BlockSpec((1, D_MODEL), lambda j: (0, j)),
        ],
        out_specs=pl.BlockSpec((n, D_MODEL), lambda j: (0, j)),
        out_shape=jax.ShapeDtypeStruct((n, N_MOD * D_MODEL), F32),
        name="adaln_mod",
    )(c_all, w_mod, b_mod.reshape(1, -1))


def _mixer_kernel(x_ref, mod_ref, s0_ref, cprev_ref, n1g_ref, win_ref, walpha_ref, balpha_ref,
                  gng_ref, wgla_ref, convw_ref, convb_ref, wconv_ref, wo_ref,
                  x1_ref, sout_ref, cout_ref,
                  hb_s, q_s, k_s, v_s, sg_s, b_s, ubuf_s, yb_s, o_s, og_s, st_s,
                  *, n_seq, tbl, chunk):
    rows = n_seq * tbl
    nchunk = tbl // chunk
    j = pl.program_id(1)
    nj = pl.num_programs(1)

    @pl.when(j == 0)
    def _init():
        for s in range(n_seq):
            for h in range(GLA_HEADS):
                st_s[s, h] = s0_ref[s, h].T
        ubuf_s[:, SUBLANES - 2:SUBLANES, :] = cprev_ref[...]

    x3 = x_ref[...]
    sh1 = mod_ref[:, 0:1, :]
    sc1 = mod_ref[:, 1:2, :]
    g1 = mod_ref[:, 2:3, :]
    ms = jnp.mean(x3 * x3, axis=-1, keepdims=True)
    h3 = x3 * lax.rsqrt(ms + EPS) * n1g_ref[...] * (1.0 + sc1) + sh1
    hb_s[...] = h3.reshape(rows, D_MODEL).astype(BF16)

    def proj(off, n):
        return _dot(hb_s[...], win_ref[:, off:off + n])

    q_s[...] = proj(OFF_Q, GLA_KEY) * (GLA_DK ** -0.5)
    k_s[...] = proj(OFF_K, GLA_KEY)
    v_s[...] = proj(OFF_V, GLA_VAL).astype(BF16)
    g = proj(OFF_G, GLA_VAL)
    sg_s[...] = (g * _sigmoid(g)).astype(BF16)
    za = proj(OFF_A, RANK_PAD)
    xa = _dot(za.astype(BF16), walpha_ref[...]) + balpha_ref[...]
    la = (jnp.minimum(xa, 0.0) - jnp.log1p(jnp.exp(-jnp.abs(xa)))) * (1.0 / GLA_TAU)

    u = proj(OFF_CC, CONV_DIM) * proj(OFF_CH, CONV_DIM)
    ubuf_s[:, SUBLANES:, :] = u.reshape(n_seq, tbl, CONV_DIM)
    conv = (ubuf_s[:, SUBLANES - 2:SUBLANES - 2 + tbl, :] * convw_ref[0:1, :]
            + ubuf_s[:, SUBLANES - 1:SUBLANES - 1 + tbl, :] * convw_ref[1:2, :]
            + ubuf_s[:, SUBLANES:SUBLANES + tbl, :] * convw_ref[2:3, :]
            + convb_ref[...])
    tail = ubuf_s[:, tbl + SUBLANES - 2:tbl + SUBLANES, :]
    ubuf_s[:, SUBLANES - 2:SUBLANES, :] = tail
    cout_ref[...] = tail
    cb = proj(OFF_CB, CONV_DIM)
    ub = (cb.reshape(n_seq, tbl, CONV_DIM) * conv).reshape(rows, CONV_DIM).astype(BF16)
    yb = _dot(ub, wconv_ref[...])
    yb_s[...] = (_sigmoid(proj(OFF_GB, D_MODEL)) * yb).astype(BF16)

    ri = lax.broadcasted_iota(jnp.int32, (chunk, chunk), 0)
    ci = lax.broadcasted_iota(jnp.int32, (chunk, chunk), 1)
    tril = ri >= ci
    tmat = tril.astype(BF16)
    bmin = jnp.zeros((1, GLA_KEY), F32)
    for idx in range(n_seq * nchunk):
        la_c = la[idx * chunk:(idx + 1) * chunk, :]
        hi = la_c.astype(BF16)
        lo = (la_c - hi.astype(F32)).astype(BF16)
        b_c = _dot(tmat, hi) + _dot(tmat, lo)
        b_s[idx * chunk:(idx + 1) * chunk, :] = b_c
        bmin = jnp.minimum(bmin, b_c[chunk - 1:chunk, :])
    safe = jnp.min(bmin) > -SAFE_LOG_DECAY

    gng = gng_ref[...]

    def chunk_body(idx, carry):
        r0 = pl.multiple_of(idx * chunk, chunk)
        s = idx // nchunk
        q = q_s[pl.ds(r0, chunk), :]
        k = k_s[pl.ds(r0, chunk), :]
        b = b_s[pl.ds(r0, chunk), :]
        v = v_s[pl.ds(r0, chunk), :]
        blast = b[chunk - 1:chunk, :]
        qb = (q * jnp.exp(b)).astype(BF16)
        kb = (k * jnp.exp(blast - b)).astype(BF16)
        dec = jnp.exp(blast)

        @pl.when(safe)
        def _fast():
            bmid = b[chunk // 2 - 1:chunk // 2, :]
            qs = (q * jnp.exp(b - bmid)).astype(BF16)
            ks = (k * jnp.exp(bmid - b)).astype(BF16)
            for h in range(GLA_HEADS):
                ksl = slice(h * GLA_DK, (h + 1) * GLA_DK)
                vsl = slice(h * GLA_DV, (h + 1) * GLA_DV)
                a = jnp.where(tril, _dot_nt(qs[:, ksl], ks[:, ksl]), 0.0)
                o_s[pl.ds(r0, chunk), vsl] = _dot(a.astype(BF16), v[:, vsl])

        @pl.when(jnp.logical_not(safe))
        def _pairwise():
            hr = lax.broadcasted_iota(jnp.int32, (GLA_KEY, GLA_VAL), 0) // GLA_DK
            hc = lax.broadcasted_iota(jnp.int32, (GLA_KEY, GLA_VAL), 1) // GLA_DV
            headsel = (hr == hc).astype(BF16)
            rowid = lax.broadcasted_iota(jnp.int32, (chunk, 1), 0)
            vf = v.astype(F32)

            def jbody(jj, acc):
                pick = rowid == jj
                kj = jnp.sum(jnp.where(pick, k, 0.0), axis=0, keepdims=True)
                bj = jnp.sum(jnp.where(pick, b, 0.0), axis=0, keepdims=True)
                vj = jnp.sum(jnp.where(pick, vf, 0.0), axis=0, keepdims=True)
                dm = q * kj * jnp.exp(jnp.minimum(b - bj, 0.0))
                dm = jnp.where(rowid >= jj, dm, 0.0)
                return acc + _dot(dm.astype(BF16), headsel) * vj

            o_s[pl.ds(r0, chunk), :] = lax.fori_loop(
                0, chunk, jbody, jnp.zeros((chunk, GLA_VAL), F32))

        for h in range(GLA_HEADS):
            ksl = slice(h * GLA_DK, (h + 1) * GLA_DK)
            vsl = slice(h * GLA_DV, (h + 1) * GLA_DV)
            st = st_s[s, h]
            o_h = o_s[pl.ds(r0, chunk), vsl] + _dot_nt(qb[:, ksl], st.astype(BF16))
            st_s[s, h] = dec[:, ksl] * st + _dot_tn(v[:, vsl], kb[:, ksl])
            o_h = o_h * lax.rsqrt(jnp.mean(o_h * o_h, axis=-1, keepdims=True) + EPS) * gng
            og_s[pl.ds(r0, chunk), vsl] = (
                o_h * sg_s[pl.ds(r0, chunk), vsl].astype(F32)).astype(BF16)
        return carry

    lax.fori_loop(0, n_seq * nchunk, chunk_body, 0)

    @pl.when(j == nj - 1)
    def _final_state():
        for s in range(n_seq):
            for h in range(GLA_HEADS):
                sout_ref[s, h] = st_s[s, h].T

    ya = _dot(og_s[...], wgla_ref[...])
    merged = _sigmoid(proj(OFF_GA, D_MODEL)) * ya + yb_s[...].astype(F32)
    m = _dot(merged.astype(BF16), wo_ref[...])
    x1_ref[...] = x_ref[...] + g1 * m.reshape(n_seq, tbl, D_MODEL)


def _const_spec(shape):
    nd = len(shape)
    return pl.BlockSpec(shape, lambda i, j: (0,) * nd, pipeline_mode=pl.Buffered(1))


def _mixer(x, mod, s0, cprev, weights, *, n_seq, tbl, chunk, name):
    n_streams, length, _ = x.shape
    grid = (n_streams // n_seq, length // tbl)
    rows = n_seq * tbl
    (n1g, win, walpha, balpha, gng, wgla, convw, convb, wconv, wo) = weights
    kern = functools.partial(_mixer_kernel, n_seq=n_seq, tbl=tbl, chunk=chunk)
    in_specs = [
        pl.BlockSpec((n_seq, tbl, D_MODEL), lambda i, j: (i, j, 0)),
        pl.BlockSpec((n_seq, N_MOD, D_MODEL), lambda i, j: (i, 0, 0)),
        pl.BlockSpec((n_seq, GLA_HEADS, GLA_DK, GLA_DV), lambda i, j: (i, 0, 0, 0)),
        pl.BlockSpec((n_seq, CONV_W - 1, CONV_DIM), lambda i, j: (i, 0, 0)),
        _const_spec(n1g.shape), _const_spec(win.shape), _const_spec(walpha.shape),
        _const_spec(balpha.shape), _const_spec(gng.shape), _const_spec(wgla.shape),
        _const_spec(convw.shape), _const_spec(convb.shape), _const_spec(wconv.shape),
        _const_spec(wo.shape),
    ]
    out_specs = [
        pl.BlockSpec((n_seq, tbl, D_MODEL), lambda i, j: (i, j, 0)),
        pl.BlockSpec((n_seq, GLA_HEADS, GLA_DK, GLA_DV), lambda i, j: (i, 0, 0, 0)),
        pl.BlockSpec((n_seq, CONV_W - 1, CONV_DIM), lambda i, j: (i, 0, 0)),
    ]
    out_shape = [
        jax.ShapeDtypeStruct(x.shape, F32),
        jax.ShapeDtypeStruct((n_streams, GLA_HEADS, GLA_DK, GLA_DV), F32),
        jax.ShapeDtypeStruct((n_streams, CONV_W - 1, CONV_DIM), F32),
    ]
    scratch = [
        pltpu.VMEM((rows, D_MODEL), BF16),
        pltpu.VMEM((rows, GLA_KEY), F32),
        pltpu.VMEM((rows, GLA_KEY), F32),
        pltpu.VMEM((rows, GLA_VAL), BF16),
        pltpu.VMEM((rows, GLA_VAL), BF16),
        pltpu.VMEM((rows, GLA_KEY), F32),
        pltpu.VMEM((n_seq, tbl + SUBLANES, CONV_DIM), F32),
        pltpu.VMEM((rows, D_MODEL), BF16),
        pltpu.VMEM((rows, GLA_VAL), F32),
        pltpu.VMEM((rows, GLA_VAL), BF16),
        pltpu.VMEM((n_seq, GLA_HEADS, GLA_DV, GLA_DK), F32),
    ]
    return pl.pallas_call(
        kern, grid=grid, in_specs=in_specs, out_specs=out_specs, out_shape=out_shape,
        scratch_shapes=scratch,
        compiler_params=pltpu.CompilerParams(
            dimension_semantics=("arbitrary", "arbitrary"), vmem_limit_bytes=VMEM_LIMIT),
        name=name,
    )(x, mod, s0, cprev, n1g, win, walpha, balpha, gng, wgla, convw, convb, wconv, wo)


FF_GROUP = 256


def _ffn_kernel(x_ref, mod_ref, n2g_ref, win_ref, wout_ref, nfg_ref, y_ref, hb_s, act_s,
                *, n_seq, tbl):
    rows = n_seq * tbl
    x3 = x_ref[...]
    sh2 = mod_ref[:, 3:4, :]
    sc2 = mod_ref[:, 4:5, :]
    g2 = mod_ref[:, 5:6, :]
    ms = jnp.mean(x3 * x3, axis=-1, keepdims=True)
    h3 = x3 * lax.rsqrt(ms + EPS) * n2g_ref[...] * (1.0 + sc2) + sh2
    hb_s[...] = h3.reshape(rows, D_MODEL).astype(BF16)
    for c in range(D_FF // FF_GROUP):
        gt = _dot(hb_s[...], win_ref[:, c * FF_GROUP:(c + 1) * FF_GROUP])
        up = _dot(hb_s[...], win_ref[:, D_FF + c * FF_GROUP:D_FF + (c + 1) * FF_GROUP])
        act_s[:, c * FF_GROUP:(c + 1) * FF_GROUP] = (gt * _sigmoid(gt) * up).astype(BF16)
    f = _dot(act_s[...], wout_ref[...])
    x2 = x3 + g2 * f.reshape(n_seq, tbl, D_MODEL)
    ms2 = jnp.mean(x2 * x2, axis=-1, keepdims=True)
    y_ref[...] = x2 * lax.rsqrt(ms2 + EPS) * nfg_ref[...]


def _ffn(x, mod, n2g, win, wout, nfg, *, n_seq, tbl, name):
    n_streams, length, _ = x.shape
    grid = (n_streams // n_seq, length // tbl)
    rows = n_seq * tbl
    kern = functools.partial(_ffn_kernel, n_seq=n_seq, tbl=tbl)
    return pl.pallas_call(
        kern, grid=grid,
        in_specs=[
            pl.BlockSpec((n_seq, tbl, D_MODEL), lambda i, j: (i, j, 0)),
            pl.BlockSpec((n_seq, N_MOD, D_MODEL), lambda i, j: (i, 0, 0)),
            _const_spec(n2g.shape), _const_spec(win.shape), _const_spec(wout.shape),
            _const_spec(nfg.shape),
        ],
        out_specs=pl.BlockSpec((n_seq, tbl, D_MODEL), lambda i, j: (i, j, 0)),
        out_shape=jax.ShapeDtypeStruct(x.shape, F32),
        scratch_shapes=[pltpu.VMEM((rows, D_MODEL), BF16), pltpu.VMEM((rows, D_FF), BF16)],
        compiler_params=pltpu.CompilerParams(
            dimension_semantics=("arbitrary", "arbitrary"), vmem_limit_bytes=VMEM_LIMIT),
        name=name,
    )(x, mod, n2g, win, wout, nfg)


def _pack_w_in(w_in):
    a0 = 2 * GLA_KEY + 2 * GLA_VAL
    a1 = a0 + GLA_RANK
    pad = jnp.zeros((D_MODEL, RANK_PAD - GLA_RANK), w_in.dtype)
    return jnp.concatenate([w_in[:, :a1], pad, w_in[:, a1:]], axis=1).astype(BF16)


def kernel(x_prompt, x_sample, c_prompt, c_sample, state_gla, cache_conv, w_mod, b_mod, norm1_g,
           w_in, w_alpha, b_alpha, gla_norm_g, w_gla_out, conv_w, conv_b, w_conv_out, w_o,
           norm2_g, w_ffn_in, w_ffn_out, norm_f_g):
    bp = x_prompt.shape[0]
    bs = x_sample.shape[0]
    depth = w_mod.shape[0]
    assert depth == 1
    l = 0
    c_all = jnp.concatenate([c_prompt, c_sample], axis=0)
    n_c = c_all.shape[0]
    n_pad = -n_c % SUBLANES
    c_all = jnp.concatenate([c_all, jnp.zeros((n_pad, D_MODEL), F32)], axis=0)
    mod = _modulation(c_all, w_mod[l], b_mod[l]).reshape(n_c + n_pad, N_MOD, D_MODEL)
    mod_p, mod_s = mod[:bp], mod[bp:bp + bs]

    walpha = jnp.concatenate(
        [w_alpha[l], jnp.zeros((RANK_PAD - GLA_RANK, GLA_KEY), F32)], axis=0).astype(BF16)
    mix_w = (norm1_g[l].reshape(1, -1), _pack_w_in(w_in[l]), walpha, b_alpha[l].reshape(1, -1),
             gla_norm_g[l].reshape(1, -1), w_gla_out[l].astype(BF16), conv_w[l],
             conv_b[l].reshape(1, -1), w_conv_out[l].astype(BF16), w_o[l].astype(BF16))
    ffn_w = (norm2_g[l].reshape(1, -1), w_ffn_in[l].astype(BF16), w_ffn_out[l].astype(BF16),
             norm_f_g.reshape(1, -1))

    s0_p = jnp.zeros((bp, GLA_HEADS, GLA_DK, GLA_DV), F32)
    c0_p = jnp.zeros((bp, CONV_W - 1, CONV_DIM), F32)
    x1_p, st_p, cv_p = _mixer(x_prompt, mod_p, s0_p, c0_p, mix_w,
                              n_seq=1, tbl=256, chunk=128, name="mixer_prompt")
    x1_s, st_s, cv_s = _mixer(x_sample, mod_s, state_gla[l], cache_conv[l], mix_w,
                              n_seq=8, tbl=x_sample.shape[1], chunk=x_sample.shape[1],
                              name="mixer_sample")
    y_p = _ffn(x1_p, mod_p, *ffn_w, n_seq=1, tbl=512, name="ffn_prompt")
    y_s = _ffn(x1_s, mod_s, *ffn_w, n_seq=bs, tbl=x_sample.shape[1], name="ffn_sample")
    return (y_p, y_s, st_p[None], cv_p[None], st_s[None], cv_s[None])
```

```python
import functools

import jax
import jax.numpy as jnp
from jax import lax
from jax.experimental import pallas as pl
from jax.experimental.pallas import tpu as pltpu

F32 = jnp.float32
BF16 = jnp.bfloat16

D_MODEL = 1024
GLA_HEADS = 4
GLA_DK = 128
GLA_DV = 256
GLA_KEY = GLA_HEADS * GLA_DK
GLA_VAL = GLA_HEADS * GLA_DV
GLA_RANK = 16
GLA_TAU = 16.0
CONV_DIM = D_MODEL
CONV_W = 3
D_FF = 2816
N_MOD = 6
EPS = 1e-6

LANES = 128
SUBLANES = 8
RANK_PAD = LANES

OFF_Q = 0
OFF_K = OFF_Q + GLA_KEY
OFF_V = OFF_K + GLA_KEY
OFF_G = OFF_V + GLA_VAL
OFF_A = OFF_G + GLA_VAL
OFF_CB = OFF_A + RANK_PAD
OFF_CC = OFF_CB + CONV_DIM
OFF_CH = OFF_CC + CONV_DIM
OFF_GA = OFF_CH + CONV_DIM
OFF_GB = OFF_GA + D_MODEL
IN_PAD = OFF_GB + D_MODEL

SAFE_LOG_DECAY = 60.0

VMEM_LIMIT = 60 * 1024 * 1024


def _dot(a, b):
    return jnp.dot(a, b, preferred_element_type=F32)


def _dot_nt(a, b):
    return lax.dot_general(a, b, (((1,), (1,)), ((), ())), preferred_element_type=F32)


def _dot_tn(a, b):
    return lax.dot_general(a, b, (((0,), (0,)), ((), ())), preferred_element_type=F32)


def _sigmoid(x):
    return 1.0 / (1.0 + jnp.exp(-x))


def _mod_kernel(c_ref, w_ref, b_ref, o_ref):
    o_ref[...] = _dot(c_ref[...].astype(BF16), w_ref[...].astype(BF16)) + b_ref[...]


def _modulation(c_all, w_mod, b_mod):
    n = c_all.shape[0]
    nblk = N_MOD
    return pl.pallas_call(
        _mod_kernel,
        grid=(nblk,),
        in_specs=[
            pl.BlockSpec((n, D_MODEL), lambda j: (0, 0)),
            pl.BlockSpec((D_MODEL, D_MODEL), lambda j: (0, j)),
            pl.BlockSpec((1, D_MODEL), lambda j: (0, j)),
        ],
        out_specs=pl.BlockSpec((n, D_MODEL), lambda j: (0, j)),
        out_shape=jax.ShapeDtypeStruct((n, N_MOD * D_MODEL), F32),
        name="adaln_mod",
    )(c_all, w_mod, b_mod.reshape(1, -1))


def _mixer_kernel(x_ref, mod_ref, s0_ref, cprev_ref, n1g_ref, win_ref, walpha_ref, balpha_ref,
                  gng_ref, wgla_ref, convw_ref, convb_ref, wconv_ref, wo_ref,
                  x1_ref, sout_ref, cout_ref,
                  hb_s, q_s, k_s, v_s, sg_s, b_s, ubuf_s, yb_s, o_s, og_s, st_s,
                  *, n_seq, tbl, chunk, unroll):
    rows = n_seq * tbl
    nchunk = tbl // chunk
    j = pl.program_id(1)
    nj = pl.num_programs(1)

    @pl.when(j == 0)
    def _init():
        for s in range(n_seq):
            for h in range(GLA_HEADS):
                st_s[s, h] = s0_ref[s, h].T
        ubuf_s[:, SUBLANES - 2:SUBLANES, :] = cprev_ref[...]

    x3 = x_ref[...]
    sh1 = mod_ref[:, 0:1, :]
    sc1 = mod_ref[:, 1:2, :]
    g1 = mod_ref[:, 2:3, :]
    ms = jnp.mean(x3 * x3, axis=-1, keepdims=True)
    h3 = x3 * lax.rsqrt(ms + EPS) * n1g_ref[...] * (1.0 + sc1) + sh1
    hb_s[...] = h3.reshape(rows, D_MODEL).astype(BF16)

    def proj(off, n):
        return _dot(hb_s[...], win_ref[:, off:off + n])

    q_s[...] = proj(OFF_Q, GLA_KEY) * (GLA_DK ** -0.5)
    k_s[...] = proj(OFF_K, GLA_KEY)
    v_s[...] = proj(OFF_V, GLA_VAL).astype(BF16)
    g = proj(OFF_G, GLA_VAL)
    sg_s[...] = (g * _sigmoid(g)).astype(BF16)
    za = proj(OFF_A, RANK_PAD)
    xa = _dot(za.astype(BF16), walpha_ref[...]) + balpha_ref[...]
    la = (jnp.minimum(xa, 0.0) - jnp.log1p(jnp.exp(-jnp.abs(xa)))) * (1.0 / GLA_TAU)

    u = proj(OFF_CC, CONV_DIM) * proj(OFF_CH, CONV_DIM)
    ubuf_s[:, SUBLANES:, :] = u.reshape(n_seq, tbl, CONV_DIM)
    conv = (ubuf_s[:, SUBLANES - 2:SUBLANES - 2 + tbl, :] * convw_ref[0:1, :]
            + ubuf_s[:, SUBLANES - 1:SUBLANES - 1 + tbl, :] * convw_ref[1:2, :]
            + ubuf_s[:, SUBLANES:SUBLANES + tbl, :] * convw_ref[2:3, :]
            + convb_ref[...])
    tail = ubuf_s[:, tbl + SUBLANES - 2:tbl + SUBLANES, :]
    ubuf_s[:, SUBLANES - 2:SUBLANES, :] = tail
    cout_ref[...] = tail
    cb = proj(OFF_CB, CONV_DIM)
    ub = (cb.reshape(n_seq, tbl, CONV_DIM) * conv).reshape(rows, CONV_DIM).astype(BF16)
    yb = _dot(ub, wconv_ref[...])
    yb_s[...] = (_sigmoid(proj(OFF_GB, D_MODEL)) * yb).astype(BF16)

    ri = lax.broadcasted_iota(jnp.int32, (chunk, chunk), 0)
    ci = lax.broadcasted_iota(jnp.int32, (chunk, chunk), 1)
    tril = ri >= ci
    tmat = tril.astype(BF16)
    bmin = jnp.zeros((1, GLA_KEY), F32)
    for idx in range(n_seq * nchunk):
        la_c = la[idx * chunk:(idx + 1) * chunk, :]
        hi = la_c.astype(BF16)
        lo = (la_c - hi.astype(F32)).astype(BF16)
        b_c = _dot(tmat, hi) + _dot(tmat, lo)
        b_s[idx * chunk:(idx + 1) * chunk, :] = b_c
        bmin = jnp.minimum(bmin, b_c[chunk - 1:chunk, :])
    safe = jnp.min(bmin) > -SAFE_LOG_DECAY

    gng = gng_ref[...]
    n_idx = n_seq * nchunk

    def loop(body):
        if unroll:
            for idx in range(n_idx):
                body(idx * chunk, idx // nchunk)
        else:
            def fbody(idx, carry):
                body(pl.multiple_of(idx * chunk, chunk), idx // nchunk)
                return carry
            lax.fori_loop(0, n_idx, fbody, 0)

    def intra_fast(r0, s):
        q = q_s[pl.ds(r0, chunk), :]
        k = k_s[pl.ds(r0, chunk), :]
        b = b_s[pl.ds(r0, chunk), :]
        v = v_s[pl.ds(r0, chunk), :]
        bmid = b[chunk // 2 - 1:chunk // 2, :]
        qs = (q * jnp.exp(b - bmid)).astype(BF16)
        ks = (k * jnp.exp(bmid - b)).astype(BF16)
        for h in range(GLA_HEADS):
            ksl = slice(h * GLA_DK, (h + 1) * GLA_DK)
            vsl = slice(h * GLA_DV, (h + 1) * GLA_DV)
            a = jnp.where(tril, _dot_nt(qs[:, ksl], ks[:, ksl]), 0.0)
            o_s[pl.ds(r0, chunk), vsl] = _dot(a.astype(BF16), v[:, vsl])

    loop(intra_fast)

    @pl.when(jnp.logical_not(safe))
    def _pairwise():
        hr = lax.broadcasted_iota(jnp.int32, (GLA_KEY, GLA_VAL), 0) // GLA_DK
        hc = lax.broadcasted_iota(jnp.int32, (GLA_KEY, GLA_VAL), 1) // GLA_DV
        headsel = (hr == hc).astype(BF16)
        rowid = lax.broadcasted_iota(jnp.int32, (chunk, 1), 0)

        def cbody(idx, carry):
            r0 = pl.multiple_of(idx * chunk, chunk)
            q = q_s[pl.ds(r0, chunk), :]
            k = k_s[pl.ds(r0, chunk), :]
            b = b_s[pl.ds(r0, chunk), :]
            vf = v_s[pl.ds(r0, chunk), :].astype(F32)

            def jbody(jj, acc):
                pick = rowid == jj
                kj = jnp.sum(jnp.where(pick, k, 0.0), axis=0, keepdims=True)
                bj = jnp.sum(jnp.where(pick, b, 0.0), axis=0, keepdims=True)
                vj = jnp.sum(jnp.where(pick, vf, 0.0), axis=0, keepdims=True)
                dm = q * kj * jnp.exp(jnp.minimum(b - bj, 0.0))
                dm = jnp.where(rowid >= jj, dm, 0.0)
                return acc + _dot(dm.astype(BF16), headsel) * vj

            o_s[pl.ds(r0, chunk), :] = lax.fori_loop(
                0, chunk, jbody, jnp.zeros((chunk, GLA_VAL), F32))
            return carry

        lax.fori_loop(0, n_idx, cbody, 0)

    def state_step(r0, s):
        q = q_s[pl.ds(r0, chunk), :]
        k = k_s[pl.ds(r0, chunk), :]
        b = b_s[pl.ds(r0, chunk), :]
        v = v_s[pl.ds(r0, chunk), :]
        blast = b[chunk - 1:chunk, :]
        qb = (q * jnp.exp(b)).astype(BF16)
        kb = (k * jnp.exp(blast - b)).astype(BF16)
        dec = jnp.exp(blast)
        for h in range(GLA_HEADS):
            ksl = slice(h * GLA_DK, (h + 1) * GLA_DK)
            vsl = slice(h * GLA_DV, (h + 1) * GLA_DV)
            st = st_s[s, h]
            o_h = o_s[pl.ds(r0, chunk), vsl] + _dot_nt(qb[:, ksl], st.astype(BF16))
            st_s[s, h] = dec[:, ksl] * st + _dot_tn(v[:, vsl], kb[:, ksl])
            o_h = o_h * lax.rsqrt(jnp.mean(o_h * o_h, axis=-1, keepdims=True) + EPS) * gng
            og_s[pl.ds(r0, chunk), vsl] = (
                o_h * sg_s[pl.ds(r0, chunk), vsl].astype(F32)).astype(BF16)

    loop(state_step)

    @pl.when(j == nj - 1)
    def _final_state():
        for s in range(n_seq):
            for h in range(GLA_HEADS):
                sout_ref[s, h] = st_s[s, h].T

    ya = _dot(og_s[...], wgla_ref[...])
    merged = _sigmoid(proj(OFF_GA, D_MODEL)) * ya + yb_s[...].astype(F32)
    m = _dot(merged.astype(BF16), wo_ref[...])
    x1_ref[...] = x_ref[...] + g1 * m.reshape(n_seq, tbl, D_MODEL)


def _const_spec(shape):
    nd = len(shape)
    return pl.BlockSpec(shape, lambda i, j: (0,) * nd, pipeline_mode=pl.Buffered(1))


def _mixer(x, mod, s0, cprev, weights, *, n_seq, tbl, chunk, name):
    n_streams, length, _ = x.shape
    grid = (n_streams // n_seq, length // tbl)
    rows = n_seq * tbl
    (n1g, win, walpha, balpha, gng, wgla, convw, convb, wconv, wo) = weights
    kern = functools.partial(_mixer_kernel, n_seq=n_seq, tbl=tbl, chunk=chunk,
                             unroll=n_seq == 1)
    in_specs = [
        pl.BlockSpec((n_seq, tbl, D_MODEL), lambda i, j: (i, j, 0)),
        pl.BlockSpec((n_seq, N_MOD, D_MODEL), lambda i, j: (i, 0, 0)),
        pl.BlockSpec((n_seq, GLA_HEADS, GLA_DK, GLA_DV), lambda i, j: (i, 0, 0, 0)),
        pl.BlockSpec((n_seq, CONV_W - 1, CONV_DIM), lambda i, j: (i, 0, 0)),
        _const_spec(n1g.shape), _const_spec(win.shape), _const_spec(walpha.shape),
        _const_spec(balpha.shape), _const_spec(gng.shape), _const_spec(wgla.shape),
        _const_spec(convw.shape), _const_spec(convb.shape), _const_spec(wconv.shape),
        _const_spec(wo.shape),
    ]
    out_specs = [
        pl.BlockSpec((n_seq, tbl, D_MODEL), lambda i, j: (i, j, 0)),
        pl.BlockSpec((n_seq, GLA_HEADS, GLA_DK, GLA_DV), lambda i, j: (i, 0, 0, 0)),
        pl.BlockSpec((n_seq, CONV_W - 1, CONV_DIM), lambda i, j: (i, 0, 0)),
    ]
    out_shape = [
        jax.ShapeDtypeStruct(x.shape, F32),
        jax.ShapeDtypeStruct((n_streams, GLA_HEADS, GLA_DK, GLA_DV), F32),
        jax.ShapeDtypeStruct((n_streams, CONV_W - 1, CONV_DIM), F32),
    ]
    scratch = [
        pltpu.VMEM((rows, D_MODEL), BF16),
        pltpu.VMEM((rows, GLA_KEY), F32),
        pltpu.VMEM((rows, GLA_KEY), F32),
        pltpu.VMEM((rows, GLA_VAL), BF16),
        pltpu.VMEM((rows, GLA_VAL), BF16),
        pltpu.VMEM((rows, GLA_KEY), F32),
        pltpu.VMEM((n_seq, tbl + SUBLANES, CONV_DIM), F32),
        pltpu.VMEM((rows, D_MODEL), BF16),
        pltpu.VMEM((rows, GLA_VAL), F32),
        pltpu.VMEM((rows, GLA_VAL), BF16),
        pltpu.VMEM((n_seq, GLA_HEADS, GLA_DV, GLA_DK), F32),
    ]
    return pl.pallas_call(
        kern, grid=grid, in_specs=in_specs, out_specs=out_specs, out_shape=out_shape,
        scratch_shapes=scratch,
        compiler_params=pltpu.CompilerParams(
            dimension_semantics=("arbitrary", "arbitrary"), vmem_limit_bytes=VMEM_LIMIT),
        name=name,
    )(x, mod, s0, cprev, n1g, win, walpha, balpha, gng, wgla, convw, convb, wconv, wo)


FF_GROUP = 256


def _ffn_kernel(x_ref, mod_ref, n2g_ref, win_ref, wout_ref, nfg_ref, y_ref, hb_s, act_s,
                *, n_seq, tbl):
    rows = n_seq * tbl
    x3 = x_ref[...]
    sh2 = mod_ref[:, 3:4, :]
    sc2 = mod_ref[:, 4:5, :]
    g2 = mod_ref[:, 5:6, :]
    ms = jnp.mean(x3 * x3, axis=-1, keepdims=True)
    h3 = x3 * lax.rsqrt(ms + EPS) * n2g_ref[...] * (1.0 + sc2) + sh2
    hb_s[...] = h3.reshape(rows, D_MODEL).astype(BF16)
    for c in range(D_FF // FF_GROUP):
        gt = _dot(hb_s[...], win_ref[:, c * FF_GROUP:(c + 1) * FF_GROUP])
        up = _dot(hb_s[...], win_ref[:, D_FF + c * FF_GROUP:D_FF + (c + 1) * FF_GROUP])
        act_s[:, c * FF_GROUP:(c + 1) * FF_GROUP] = (gt * _sigmoid(gt) * up).astype(BF16)
    f = _dot(act_s[...], wout_ref[...])
    x2 = x3 + g2 * f.reshape(n_seq, tbl, D_MODEL)
    ms2 = jnp.mean(x2 * x2, axis=-1, keepdims=True)
    y_ref[...] = x2 * lax.rsqrt(ms2 + EPS) * nfg_ref[...]


def _ffn(x, mod, n2g, win, wout, nfg, *, n_seq, tbl, name):
    n_streams, length, _ = x.shape
    grid = (n_streams // n_seq, length // tbl)
    rows = n_seq * tbl
    kern = functools.partial(_ffn_kernel, n_seq=n_seq, tbl=tbl)
    return pl.pallas_call(
        kern, grid=grid,
        in_specs=[
            pl.BlockSpec((n_seq, tbl, D_MODEL), lambda i, j: (i, j, 0)),
            pl.BlockSpec((n_seq, N_MOD, D_MODEL), lambda i, j: (i, 0, 0)),
            _const_spec(n2g.shape), _const_spec(win.shape), _const_spec(wout.shape),
            _const_spec(nfg.shape),
        ],
        out_specs=pl.BlockSpec((n_seq, tbl, D_MODEL), lambda i, j: (i, j, 0)),
        out_shape=jax.ShapeDtypeStruct(x.shape, F32),
        scratch_shapes=[pltpu.VMEM((rows, D_MODEL), BF16), pltpu.VMEM((rows, D_FF), BF16)],
        compiler_params=pltpu.CompilerParams(
            dimension_semantics=("arbitrary", "arbitrary"), vmem_limit_bytes=VMEM_LIMIT),
        name=name,
    )(x, mod, n2g, win, wout, nfg)


def _pack_w_in(w_in):
    a0 = 2 * GLA_KEY + 2 * GLA_VAL
    a1 = a0 + GLA_RANK
    pad = jnp.zeros((D_MODEL, RANK_PAD - GLA_RANK), w_in.dtype)
    return jnp.concatenate([w_in[:, :a1], pad, w_in[:, a1:]], axis=1).astype(BF16)


def kernel(x_prompt, x_sample, c_prompt, c_sample, state_gla, cache_conv, w_mod, b_mod, norm1_g,
           w_in, w_alpha, b_alpha, gla_norm_g, w_gla_out, conv_w, conv_b, w_conv_out, w_o,
           norm2_g, w_ffn_in, w_ffn_out, norm_f_g):
    bp = x_prompt.shape[0]
    bs = x_sample.shape[0]
    depth = w_mod.shape[0]
    assert depth == 1
    l = 0
    c_all = jnp.concatenate([c_prompt, c_sample], axis=0)
    n_c = c_all.shape[0]
    n_pad = -n_c % SUBLANES
    c_all = jnp.concatenate([c_all, jnp.zeros((n_pad, D_MODEL), F32)], axis=0)
    mod = _modulation(c_all, w_mod[l], b_mod[l]).reshape(n_c + n_pad, N_MOD, D_MODEL)
    mod_p, mod_s = mod[:bp], mod[bp:bp + bs]

    walpha = jnp.concatenate(
        [w_alpha[l], jnp.zeros((RANK_PAD - GLA_RANK, GLA_KEY), F32)], axis=0).astype(BF16)
    mix_w = (norm1_g[l].reshape(1, -1), _pack_w_in(w_in[l]), walpha, b_alpha[l].reshape(1, -1),
             gla_norm_g[l].reshape(1, -1), w_gla_out[l].astype(BF16), conv_w[l],
             conv_b[l].reshape(1, -1), w_conv_out[l].astype(BF16), w_o[l].astype(BF16))
    ffn_w = (norm2_g[l].reshape(1, -1), w_ffn_in[l].astype(BF16), w_ffn_out[l].astype(BF16),
             norm_f_g.reshape(1, -1))

    s0_p = jnp.zeros((bp, GLA_HEADS, GLA_DK, GLA_DV), F32)
    c0_p = jnp.zeros((bp, CONV_W - 1, CONV_DIM), F32)
    x1_p, st_p, cv_p = _mixer(x_prompt, mod_p, s0_p, c0_p, mix_w,
                              n_seq=1, tbl=512, chunk=128, name="mixer_prompt")
    x1_s, st_s, cv_s = _mixer(x_sample, mod_s, state_gla[l], cache_conv[l], mix_w,
                              n_seq=8, tbl=x_sample.shape[1], chunk=x_sample.shape[1],
                              name="mixer_sample")
    y_p = _ffn(x1_p, mod_p, *ffn_w, n_seq=1, tbl=512, name="ffn_prompt")
    y_s = _ffn(x1_s, mod_s, *ffn_w, n_seq=bs, tbl=x_sample.shape[1], name="ffn_sample")
    return (y_p, y_s, st_p[None], cv_p[None], st_s[None], cv_s[None])
```

```python
import functools

import jax
import jax.numpy as jnp
from jax import lax
from jax.experimental import pallas as pl
from jax.experimental.pallas import tpu as pltpu

F32 = jnp.float32
BF16 = jnp.bfloat16

D_MODEL = 1024
GLA_HEADS = 4
GLA_DK = 128
GLA_DV = 256
GLA_KEY = GLA_HEADS * GLA_DK
GLA_VAL = GLA_HEADS * GLA_DV
GLA_RANK = 16
GLA_TAU = 16.0
CONV_DIM = D_MODEL
CONV_W = 3
D_FF = 2816
N_MOD = 6
EPS = 1e-6

LANES = 128
SUBLANES = 8
RANK_PAD = LANES

OFF_Q = 0
OFF_K = OFF_Q + GLA_KEY
OFF_V = OFF_K + GLA_KEY
OFF_G = OFF_V + GLA_VAL
OFF_W = OFF_G + GLA_VAL
N_WIN = 5
IN_DIM = OFF_W + GLA_RANK + N_WIN * D_MODEL
IN_PAD = OFF_W + N_WIN * D_MODEL + LANES

SAFE_LOG_DECAY = 60.0

VMEM_LIMIT = 60 * 1024 * 1024


def _dot(a, b):
    return jnp.dot(a, b, preferred_element_type=F32)


def _dot_nt(a, b):
    return lax.dot_general(a, b, (((1,), (1,)), ((), ())), preferred_element_type=F32)


def _dot_tn(a, b):
    return lax.dot_general(a, b, (((0,), (0,)), ((), ())), preferred_element_type=F32)


def _sigmoid(x):
    return 1.0 / (1.0 + jnp.exp(-x))


def _mod_kernel(c_ref, w_ref, b_ref, o_ref):
    o_ref[...] = _dot(c_ref[...].astype(BF16), w_ref[...].astype(BF16)) + b_ref[...]


def _modulation(c_all, w_mod, b_mod):
    n = c_all.shape[0]
    nblk = N_MOD
    return pl.pallas_call(
        _mod_kernel,
        grid=(nblk,),
        in_specs=[
            pl.BlockSpec((n, D_MODEL), lambda j: (0, 0)),
            pl.BlockSpec((D_MODEL, D_MODEL), lambda j: (0, j)),
            pl.BlockSpec((1, D_MODEL), lambda j: (0, j)),
        ],
        out_specs=pl.BlockSpec((n, D_MODEL), lambda j: (0, j)),
        out_shape=jax.ShapeDtypeStruct((n, N_MOD * D_MODEL), F32),
        name="adaln_mod",
    )(c_all, w_mod, b_mod.reshape(1, -1))


def _mixer_kernel(x_ref, mod_ref, s0_ref, cprev_ref, n1g_ref, win_ref, walpha_ref, balpha_ref,
                  gng_ref, wgla_ref, convw_ref, convb_ref, wconv_ref, wo_ref,
                  x1_ref, sout_ref, cout_ref,
                  hb_s, q_s, k_s, v_s, sg_s, b_s, ubuf_s, cb_s, sga_s, yb_s, o_s, og_s, st_s,
                  *, n_seq, tbl, chunk, unroll):
    rows = n_seq * tbl
    nchunk = tbl // chunk
    j = pl.program_id(1)
    nj = pl.num_programs(1)

    @pl.when(j == 0)
    def _init():
        for s in range(n_seq):
            for h in range(GLA_HEADS):
                st_s[s, h] = s0_ref[s, h].T
        ubuf_s[:, SUBLANES - 2:SUBLANES, :] = cprev_ref[...]

    x3 = x_ref[...]
    sh1 = mod_ref[:, 0:1, :]
    sc1 = mod_ref[:, 1:2, :]
    g1 = mod_ref[:, 2:3, :]
    ms = jnp.mean(x3 * x3, axis=-1, keepdims=True)
    h3 = x3 * lax.rsqrt(ms + EPS) * n1g_ref[...] * (1.0 + sc1) + sh1
    hb_s[...] = h3.reshape(rows, D_MODEL).astype(BF16)

    def proj(off, n):
        return _dot(hb_s[...], win_ref[:, off:off + n])

    q_s[...] = proj(OFF_Q, GLA_KEY) * (GLA_DK ** -0.5)
    k_s[...] = proj(OFF_K, GLA_KEY)
    v_s[...] = proj(OFF_V, GLA_VAL).astype(BF16)
    g = proj(OFF_G, GLA_VAL)
    sg_s[...] = (g * _sigmoid(g)).astype(BF16)

    head16 = lax.broadcasted_iota(jnp.int32, (1, LANES), 1) < GLA_RANK

    def window(i, n=D_MODEL):
        return proj(OFF_W + i * D_MODEL, n)

    def first_tile(lo, hi):
        return jnp.where(head16, hi, lo)

    w0 = window(0)
    f0 = w0[:, :LANES]
    za = jnp.where(head16, f0, 0.0)
    xa = _dot(za.astype(BF16), walpha_ref[...]) + balpha_ref[...]
    la = (jnp.minimum(xa, 0.0) - jnp.log1p(jnp.exp(-jnp.abs(xa)))) * (1.0 / GLA_TAU)
    cb_s[...] = w0.astype(BF16)
    w1 = window(1)
    f1 = w1[:, :LANES]
    cb_s[:, :LANES] = first_tile(f0, f1).astype(BF16)
    w2 = window(2)
    f2 = w2[:, :LANES]
    ubuf_s[:, SUBLANES:, :] = (w1 * w2).reshape(n_seq, tbl, CONV_DIM)
    w3 = window(3)
    f3 = w3[:, :LANES]
    ubuf_s[:, SUBLANES:, :LANES] = first_tile(f1 * f2, f2 * f3).reshape(n_seq, tbl, LANES)
    sga_s[...] = _sigmoid(w3).astype(BF16)
    w4 = window(4)
    f4 = w4[:, :LANES]
    sga_s[:, :LANES] = _sigmoid(first_tile(f3, f4)).astype(BF16)
    yb_s[...] = _sigmoid(w4).astype(BF16)
    f5 = window(N_WIN, LANES)
    yb_s[:, :LANES] = _sigmoid(first_tile(f4, f5)).astype(BF16)

    conv = (ubuf_s[:, SUBLANES - 2:SUBLANES - 2 + tbl, :] * convw_ref[0:1, :]
            + ubuf_s[:, SUBLANES - 1:SUBLANES - 1 + tbl, :] * convw_ref[1:2, :]
            + ubuf_s[:, SUBLANES:SUBLANES + tbl, :] * convw_ref[2:3, :]
            + convb_ref[...])
    tail = ubuf_s[:, tbl + SUBLANES - 2:tbl + SUBLANES, :]
    ubuf_s[:, SUBLANES - 2:SUBLANES, :] = tail
    cout_ref[...] = tail
    cb = cb_s[...].astype(F32)
    ub = (cb.reshape(n_seq, tbl, CONV_DIM) * conv).reshape(rows, CONV_DIM).astype(BF16)
    yb = _dot(ub, wconv_ref[...])
    yb_s[...] = (yb_s[...].astype(F32) * yb).astype(BF16)

    ri = lax.broadcasted_iota(jnp.int32, (chunk, chunk), 0)
    ci = lax.broadcasted_iota(jnp.int32, (chunk, chunk), 1)
    tril = ri >= ci
    tmat = tril.astype(BF16)
    bmin = jnp.zeros((1, GLA_KEY), F32)
    for idx in range(n_seq * nchunk):
        la_c = la[idx * chunk:(idx + 1) * chunk, :]
        hi = la_c.astype(BF16)
        lo = (la_c - hi.astype(F32)).astype(BF16)
        b_c = _dot(tmat, hi) + _dot(tmat, lo)
        b_s[idx * chunk:(idx + 1) * chunk, :] = b_c
        bmin = jnp.minimum(bmin, b_c[chunk - 1:chunk, :])
    safe = jnp.min(bmin) > -SAFE_LOG_DECAY

    gng = gng_ref[...]
    n_idx = n_seq * nchunk

    def loop(body):
        if unroll:
            for idx in range(n_idx):
                body(idx * chunk, idx // nchunk)
        else:
            def fbody(idx, carry):
                body(pl.multiple_of(idx * chunk, chunk), idx // nchunk)
                return carry
            lax.fori_loop(0, n_idx, fbody, 0)

    def intra_fast(r0, s):
        q = q_s[pl.ds(r0, chunk), :]
        k = k_s[pl.ds(r0, chunk), :]
        b = b_s[pl.ds(r0, chunk), :]
        v = v_s[pl.ds(r0, chunk), :]
        bmid = b[chunk // 2 - 1:chunk // 2, :]
        qs = (q * jnp.exp(b - bmid)).astype(BF16)
        ks = (k * jnp.exp(bmid - b)).astype(BF16)
        for h in range(GLA_HEADS):
            ksl = slice(h * GLA_DK, (h + 1) * GLA_DK)
            vsl = slice(h * GLA_DV, (h + 1) * GLA_DV)
            a = jnp.where(tril, _dot_nt(qs[:, ksl], ks[:, ksl]), 0.0)
            o_s[pl.ds(r0, chunk), vsl] = _dot(a.astype(BF16), v[:, vsl])

    loop(intra_fast)

    @pl.when(jnp.logical_not(safe))
    def _pairwise():
        hr = lax.broadcasted_iota(jnp.int32, (GLA_KEY, GLA_VAL), 0) // GLA_DK
        hc = lax.broadcasted_iota(jnp.int32, (GLA_KEY, GLA_VAL), 1) // GLA_DV
        headsel = (hr == hc).astype(BF16)
        rowid = lax.broadcasted_iota(jnp.int32, (chunk, 1), 0)

        def cbody(idx, carry):
            r0 = pl.multiple_of(idx * chunk, chunk)
            q = q_s[pl.ds(r0, chunk), :]
            k = k_s[pl.ds(r0, chunk), :]
            b = b_s[pl.ds(r0, chunk), :]
            vf = v_s[pl.ds(r0, chunk), :].astype(F32)

            def jbody(jj, acc):
                pick = rowid == jj
                kj = jnp.sum(jnp.where(pick, k, 0.0), axis=0, keepdims=True)
                bj = jnp.sum(jnp.where(pick, b, 0.0), axis=0, keepdims=True)
                vj = jnp.sum(jnp.where(pick, vf, 0.0), axis=0, keepdims=True)
                dm = q * kj * jnp.exp(jnp.minimum(b - bj, 0.0))
                dm = jnp.where(rowid >= jj, dm, 0.0)
                return acc + _dot(dm.astype(BF16), headsel) * vj

            o_s[pl.ds(r0, chunk), :] = lax.fori_loop(
                0, chunk, jbody, jnp.zeros((chunk, GLA_VAL), F32))
            return carry

        lax.fori_loop(0, n_idx, cbody, 0)

    def state_step(r0, s):
        q = q_s[pl.ds(r0, chunk), :]
        k = k_s[pl.ds(r0, chunk), :]
        b = b_s[pl.ds(r0, chunk), :]
        v = v_s[pl.ds(r0, chunk), :]
        blast = b[chunk - 1:chunk, :]
        qb = (q * jnp.exp(b)).astype(BF16)
        kb = (k * jnp.exp(blast - b)).astype(BF16)
        dec = jnp.exp(blast)
        for h in range(GLA_HEADS):
            ksl = slice(h * GLA_DK, (h + 1) * GLA_DK)
            vsl = slice(h * GLA_DV, (h + 1) * GLA_DV)
            st = st_s[s, h]
            o_h = o_s[pl.ds(r0, chunk), vsl] + _dot_nt(qb[:, ksl], st.astype(BF16))
            st_s[s, h] = dec[:, ksl] * st + _dot_tn(v[:, vsl], kb[:, ksl])
            o_h = o_h * lax.rsqrt(jnp.mean(o_h * o_h, axis=-1, keepdims=True) + EPS) * gng
            og_s[pl.ds(r0, chunk), vsl] = (
                o_h * sg_s[pl.ds(r0, chunk), vsl].astype(F32)).astype(BF16)

    loop(state_step)

    @pl.when(j == nj - 1)
    def _final_state():
        for s in range(n_seq):
            for h in range(GLA_HEADS):
                sout_ref[s, h] = st_s[s, h].T

    ya = _dot(og_s[...], wgla_ref[...])
    merged = sga_s[...].astype(F32) * ya + yb_s[...].astype(F32)
    m = _dot(merged.astype(BF16), wo_ref[...])
    x1_ref[...] = x_ref[...] + g1 * m.reshape(n_seq, tbl, D_MODEL)


def _const_spec(shape):
    nd = len(shape)
    return pl.BlockSpec(shape, lambda i, j: (0,) * nd, pipeline_mode=pl.Buffered(1))


def _mixer(x, mod, s0, cprev, weights, *, n_seq, tbl, chunk, name):
    n_streams, length, _ = x.shape
    grid = (n_streams // n_seq, length // tbl)
    rows = n_seq * tbl
    (n1g, win, walpha, balpha, gng, wgla, convw, convb, wconv, wo) = weights
    kern = functools.partial(_mixer_kernel, n_seq=n_seq, tbl=tbl, chunk=chunk,
                             unroll=n_seq == 1)
    in_specs = [
        pl.BlockSpec((n_seq, tbl, D_MODEL), lambda i, j: (i, j, 0)),
        pl.BlockSpec((n_seq, N_MOD, D_MODEL), lambda i, j: (i, 0, 0)),
        pl.BlockSpec((n_seq, GLA_HEADS, GLA_DK, GLA_DV), lambda i, j: (i, 0, 0, 0)),
        pl.BlockSpec((n_seq, CONV_W - 1, CONV_DIM), lambda i, j: (i, 0, 0)),
        _const_spec(n1g.shape), _const_spec(win.shape), _const_spec(walpha.shape),
        _const_spec(balpha.shape), _const_spec(gng.shape), _const_spec(wgla.shape),
        _const_spec(convw.shape), _const_spec(convb.shape), _const_spec(wconv.shape),
        _const_spec(wo.shape),
    ]
    out_specs = [
        pl.BlockSpec((n_seq, tbl, D_MODEL), lambda i, j: (i, j, 0)),
        pl.BlockSpec((n_seq, GLA_HEADS, GLA_DK, GLA_DV), lambda i, j: (i, 0, 0, 0)),
        pl.BlockSpec((n_seq, CONV_W - 1, CONV_DIM), lambda i, j: (i, 0, 0)),
    ]
    out_shape = [
        jax.ShapeDtypeStruct(x.shape, F32),
        jax.ShapeDtypeStruct((n_streams, GLA_HEADS, GLA_DK, GLA_DV), F32),
        jax.ShapeDtypeStruct((n_streams, CONV_W - 1, CONV_DIM), F32),
    ]
    scratch = [
        pltpu.VMEM((rows, D_MODEL), BF16),
        pltpu.VMEM((rows, GLA_KEY), F32),
        pltpu.VMEM((rows, GLA_KEY), F32),
        pltpu.VMEM((rows, GLA_VAL), BF16),
        pltpu.VMEM((rows, GLA_VAL), BF16),
        pltpu.VMEM((rows, GLA_KEY), F32),
        pltpu.VMEM((n_seq, tbl + SUBLANES, CONV_DIM), F32),
        pltpu.VMEM((rows, CONV_DIM), BF16),
        pltpu.VMEM((rows, D_MODEL), BF16),
        pltpu.VMEM((rows, D_MODEL), BF16),
        pltpu.VMEM((rows, GLA_VAL), F32),
        pltpu.VMEM((rows, GLA_VAL), BF16),
        pltpu.VMEM((n_seq, GLA_HEADS, GLA_DV, GLA_DK), F32),
    ]
    return pl.pallas_call(
        kern, grid=grid, in_specs=in_specs, out_specs=out_specs, out_shape=out_shape,
        scratch_shapes=scratch,
        compiler_params=pltpu.CompilerParams(
            dimension_semantics=("arbitrary", "arbitrary"), vmem_limit_bytes=VMEM_LIMIT),
        name=name,
    )(x, mod, s0, cprev, n1g, win, walpha, balpha, gng, wgla, convw, convb, wconv, wo)


FF_GROUP = 256


def _ffn_kernel(x_ref, mod_ref, n2g_ref, win_ref, wout_ref, nfg_ref, y_ref, hb_s, act_s,
                *, n_seq, tbl):
    rows = n_seq * tbl
    x3 = x_ref[...]
    sh2 = mod_ref[:, 3:4, :]
    sc2 = mod_ref[:, 4:5, :]
    g2 = mod_ref[:, 5:6, :]
    ms = jnp.mean(x3 * x3, axis=-1, keepdims=True)
    h3 = x3 * lax.rsqrt(ms + EPS) * n2g_ref[...] * (1.0 + sc2) + sh2
    hb_s[...] = h3.reshape(rows, D_MODEL).astype(BF16)
    for c in range(D_FF // FF_GROUP):
        gt = _dot(hb_s[...], win_ref[:, c * FF_GROUP:(c + 1) * FF_GROUP])
        up = _dot(hb_s[...], win_ref[:, D_FF + c * FF_GROUP:D_FF + (c + 1) * FF_GROUP])
        act_s[:, c * FF_GROUP:(c + 1) * FF_GROUP] = (gt * _sigmoid(gt) * up).astype(BF16)
    f = _dot(act_s[...], wout_ref[...])
    x2 = x3 + g2 * f.reshape(n_seq, tbl, D_MODEL)
    ms2 = jnp.mean(x2 * x2, axis=-1, keepdims=True)
    y_ref[...] = x2 * lax.rsqrt(ms2 + EPS) * nfg_ref[...]


def _ffn(x, mod, n2g, win, wout, nfg, *, n_seq, tbl, name):
    n_streams, length, _ = x.shape
    grid = (n_streams // n_seq, length // tbl)
    rows = n_seq * tbl
    kern = functools.partial(_ffn_kernel, n_seq=n_seq, tbl=tbl)
    return pl.pallas_call(
        kern, grid=grid,
        in_specs=[
            pl.BlockSpec((n_seq, tbl, D_MODEL), lambda i, j: (i, j, 0)),
            pl.BlockSpec((n_seq, N_MOD, D_MODEL), lambda i, j: (i, 0, 0)),
            _const_spec(n2g.shape), _const_spec(win.shape), _const_spec(wout.shape),
            _const_spec(nfg.shape),
        ],
        out_specs=pl.BlockSpec((n_seq, tbl, D_MODEL), lambda i, j: (i, j, 0)),
        out_shape=jax.ShapeDtypeStruct(x.shape, F32),
        scratch_shapes=[pltpu.VMEM((rows, D_MODEL), BF16), pltpu.VMEM((rows, D_FF), BF16)],
        compiler_params=pltpu.CompilerParams(
            dimension_semantics=("arbitrary", "arbitrary"), vmem_limit_bytes=VMEM_LIMIT),
        name=name,
    )(x, mod, n2g, win, wout, nfg)


def _roll(a, axis):
    return jnp.roll(a, GLA_RANK, axis=axis)


def kernel(x_prompt, x_sample, c_prompt, c_sample, state_gla, cache_conv, w_mod, b_mod, norm1_g,
           w_in, w_alpha, b_alpha, gla_norm_g, w_gla_out, conv_w, conv_b, w_conv_out, w_o,
           norm2_g, w_ffn_in, w_ffn_out, norm_f_g):
    bp = x_prompt.shape[0]
    bs = x_sample.shape[0]
    depth = w_mod.shape[0]
    assert depth == 1
    l = 0
    c_all = jnp.concatenate([c_prompt, c_sample], axis=0)
    n_c = c_all.shape[0]
    n_pad = -n_c % SUBLANES
    c_all = jnp.concatenate([c_all, jnp.zeros((n_pad, D_MODEL), F32)], axis=0)
    mod = _modulation(c_all, w_mod[l], b_mod[l]).reshape(n_c + n_pad, N_MOD, D_MODEL)
    mod_p, mod_s = mod[:bp], mod[bp:bp + bs]

    walpha = jnp.concatenate(
        [w_alpha[l], jnp.zeros((RANK_PAD - GLA_RANK, GLA_KEY), F32)], axis=0).astype(BF16)
    assert w_in.shape[2] == IN_DIM
    win = jnp.pad(w_in[l].astype(BF16), ((0, 0), (0, IN_PAD - IN_DIM)))
    mix_w = (norm1_g[l].reshape(1, -1), win, walpha, b_alpha[l].reshape(1, -1),
             gla_norm_g[l].reshape(1, -1), _roll(w_gla_out[l], 1).astype(BF16),
             _roll(conv_w[l], 1), _roll(conv_b[l], 0).reshape(1, -1),
             _roll(_roll(w_conv_out[l], 0), 1).astype(BF16), _roll(w_o[l], 0).astype(BF16))
    ffn_w = (norm2_g[l].reshape(1, -1), w_ffn_in[l].astype(BF16), w_ffn_out[l].astype(BF16),
             norm_f_g.reshape(1, -1))

    s0_p = jnp.zeros((bp, GLA_HEADS, GLA_DK, GLA_DV), F32)
    c0_p = jnp.zeros((bp, CONV_W - 1, CONV_DIM), F32)
    x1_p, st_p, cv_p = _mixer(x_prompt, mod_p, s0_p, c0_p, mix_w,
                              n_seq=1, tbl=512, chunk=128, name="mixer_prompt")
    x1_s, st_s, cv_s = _mixer(x_sample, mod_s, state_gla[l], _roll(cache_conv[l], 2), mix_w,
                              n_seq=8, tbl=x_sample.shape[1], chunk=x_sample.shape[1],
                              name="mixer_sample")
    cv_p = jnp.roll(cv_p, -GLA_RANK, axis=2)
    cv_s = jnp.roll(cv_s, -GLA_RANK, axis=2)
    y_p = _ffn(x1_p, mod_p, *ffn_w, n_seq=1, tbl=512, name="ffn_prompt")
    y_s = _ffn(x1_s, mod_s, *ffn_w, n_seq=bs, tbl=x_sample.shape[1], name="ffn_sample")
    return (y_p, y_s, st_p[None], cv_p[None], st_s[None], cv_s[None])
```

```python
import functools

import jax
import jax.numpy as jnp
from jax import lax
from jax.experimental import pallas as pl
from jax.experimental.pallas import tpu as pltpu

F32 = jnp.float32
BF16 = jnp.bfloat16

D_MODEL = 1024
GLA_HEADS = 4
GLA_DK = 128
GLA_DV = 256
GLA_KEY = GLA_HEADS * GLA_DK
GLA_VAL = GLA_HEADS * GLA_DV
GLA_RANK = 16
GLA_TAU = 16.0
CONV_DIM = D_MODEL
CONV_W = 3
D_FF = 2816
N_MOD = 6
EPS = 1e-6

LANES = 128
SUBLANES = 8
RANK_PAD = LANES

OFF_Q = 0
OFF_K = OFF_Q + GLA_KEY
OFF_V = OFF_K + GLA_KEY
OFF_G = OFF_V + GLA_VAL
OFF_A = OFF_G + GLA_VAL
OFF_CB = OFF_A + RANK_PAD
OFF_CC = OFF_CB + CONV_DIM
OFF_CH = OFF_CC + CONV_DIM
OFF_GA = OFF_CH + CONV_DIM
OFF_GB = OFF_GA + D_MODEL
IN_PAD = OFF_GB + D_MODEL
IN_DIM = IN_PAD - RANK_PAD + GLA_RANK

SAFE_LOG_DECAY = 60.0

VMEM_LIMIT = 60 * 1024 * 1024


def _dot(a, b):
    return jnp.dot(a, b, preferred_element_type=F32)


def _dot_nt(a, b):
    return lax.dot_general(a, b, (((1,), (1,)), ((), ())), preferred_element_type=F32)


def _dot_tn(a, b):
    return lax.dot_general(a, b, (((0,), (0,)), ((), ())), preferred_element_type=F32)


def _sigmoid(x):
    return 1.0 / (1.0 + jnp.exp(-x))


def _mod_kernel(c_ref, w_ref, b_ref, o_ref):
    o_ref[...] = _dot(c_ref[...].astype(BF16), w_ref[...].astype(BF16)) + b_ref[...]


def _modulation(c_all, w_mod, b_mod):
    n = c_all.shape[0]
    nblk = N_MOD
    return pl.pallas_call(
        _mod_kernel,
        grid=(nblk,),
        in_specs=[
            pl.BlockSpec((n, D_MODEL), lambda j: (0, 0)),
            pl.BlockSpec((D_MODEL, D_MODEL), lambda j: (0, j)),
            pl.BlockSpec((1, D_MODEL), lambda j: (0, j)),
        ],
        out_specs=pl.BlockSpec((n, D_MODEL), lambda j: (0, j)),
        out_shape=jax.ShapeDtypeStruct((n, N_MOD * D_MODEL), F32),
        name="adaln_mod",
    )(c_all, w_mod, b_mod.reshape(1, -1))


def _mixer_kernel(x_ref, mod_ref, s0_ref, cprev_ref, n1g_ref, win_ref, walpha_ref, balpha_ref,
                  gng_ref, wgla_ref, convw_ref, convb_ref, wconv_ref, wo_ref,
                  x1_ref, sout_ref, cout_ref,
                  hb_s, q_s, k_s, v_s, sg_s, b_s, ubuf_s, yb_s, o_s, og_s, st_s,
                  *, n_seq, tbl, chunk, unroll):
    rows = n_seq * tbl
    nchunk = tbl // chunk
    j = pl.program_id(1)
    nj = pl.num_programs(1)

    @pl.when(j == 0)
    def _init():
        for s in range(n_seq):
            for h in range(GLA_HEADS):
                st_s[s, h] = s0_ref[s, h].T
        ubuf_s[:, SUBLANES - 2:SUBLANES, :] = cprev_ref[...]

    x3 = x_ref[...]
    sh1 = mod_ref[:, 0:1, :]
    sc1 = mod_ref[:, 1:2, :]
    g1 = mod_ref[:, 2:3, :]
    ms = jnp.mean(x3 * x3, axis=-1, keepdims=True)
    h3 = x3 * lax.rsqrt(ms + EPS) * n1g_ref[...] * (1.0 + sc1) + sh1
    hb_s[...] = h3.reshape(rows, D_MODEL).astype(BF16)

    def proj(off, n):
        return _dot(hb_s[...], win_ref[:, off:off + n])

    q_s[...] = proj(OFF_Q, GLA_KEY) * (GLA_DK ** -0.5)
    k_s[...] = proj(OFF_K, GLA_KEY)
    v_s[...] = proj(OFF_V, GLA_VAL).astype(BF16)
    g = proj(OFF_G, GLA_VAL)
    sg_s[...] = (g * _sigmoid(g)).astype(BF16)
    za = proj(OFF_A, RANK_PAD)
    xa = _dot(za.astype(BF16), walpha_ref[...]) + balpha_ref[...]
    la = (jnp.minimum(xa, 0.0) - jnp.log1p(jnp.exp(-jnp.abs(xa)))) * (1.0 / GLA_TAU)

    u = proj(OFF_CC, CONV_DIM) * proj(OFF_CH, CONV_DIM)
    ubuf_s[:, SUBLANES:, :] = u.reshape(n_seq, tbl, CONV_DIM)
    conv = (ubuf_s[:, SUBLANES - 2:SUBLANES - 2 + tbl, :] * convw_ref[0:1, :]
            + ubuf_s[:, SUBLANES - 1:SUBLANES - 1 + tbl, :] * convw_ref[1:2, :]
            + ubuf_s[:, SUBLANES:SUBLANES + tbl, :] * convw_ref[2:3, :]
            + convb_ref[...])
    tail = ubuf_s[:, tbl + SUBLANES - 2:tbl + SUBLANES, :]
    ubuf_s[:, SUBLANES - 2:SUBLANES, :] = tail
    cout_ref[...] = tail
    cb = proj(OFF_CB, CONV_DIM)
    ub = (cb.reshape(n_seq, tbl, CONV_DIM) * conv).reshape(rows, CONV_DIM).astype(BF16)
    yb = _dot(ub, wconv_ref[...])
    yb_s[...] = (_sigmoid(proj(OFF_GB, D_MODEL)) * yb).astype(BF16)

    ri = lax.broadcasted_iota(jnp.int32, (chunk, chunk), 0)
    ci = lax.broadcasted_iota(jnp.int32, (chunk, chunk), 1)
    tril = ri >= ci
    tmat = tril.astype(BF16)
    bmin = jnp.zeros((1, GLA_KEY), F32)
    for idx in range(n_seq * nchunk):
        la_c = la[idx * chunk:(idx + 1) * chunk, :]
        hi = la_c.astype(BF16)
        lo = (la_c - hi.astype(F32)).astype(BF16)
        b_c = _dot(tmat, hi) + _dot(tmat, lo)
        b_s[idx * chunk:(idx + 1) * chunk, :] = b_c
        bmin = jnp.minimum(bmin, b_c[chunk - 1:chunk, :])
    safe = jnp.min(bmin) > -SAFE_LOG_DECAY

    gng = gng_ref[...]
    n_idx = n_seq * nchunk

    def loop(body):
        if unroll:
            for idx in range(n_idx):
                body(idx * chunk, idx // nchunk)
        else:
            def fbody(idx, carry):
                body(pl.multiple_of(idx * chunk, chunk), idx // nchunk)
                return carry
            lax.fori_loop(0, n_idx, fbody, 0)

    def intra_fast(r0, s):
        q = q_s[pl.ds(r0, chunk), :]
        k = k_s[pl.ds(r0, chunk), :]
        b = b_s[pl.ds(r0, chunk), :]
        v = v_s[pl.ds(r0, chunk), :]
        bmid = b[chunk // 2 - 1:chunk // 2, :]
        qs = (q * jnp.exp(b - bmid)).astype(BF16)
        ks = (k * jnp.exp(bmid - b)).astype(BF16)
        for h in range(GLA_HEADS):
            ksl = slice(h * GLA_DK, (h + 1) * GLA_DK)
            vsl = slice(h * GLA_DV, (h + 1) * GLA_DV)
            a = jnp.where(tril, _dot_nt(qs[:, ksl], ks[:, ksl]), 0.0)
            o_s[pl.ds(r0, chunk), vsl] = _dot(a.astype(BF16), v[:, vsl])

    loop(intra_fast)

    @pl.when(jnp.logical_not(safe))
    def _pairwise():
        hr = lax.broadcasted_iota(jnp.int32, (GLA_KEY, GLA_VAL), 0) // GLA_DK
        hc = lax.broadcasted_iota(jnp.int32, (GLA_KEY, GLA_VAL), 1) // GLA_DV
        headsel = (hr == hc).astype(BF16)
        rowid = lax.broadcasted_iota(jnp.int32, (chunk, 1), 0)

        def cbody(idx, carry):
            r0 = pl.multiple_of(idx * chunk, chunk)
            q = q_s[pl.ds(r0, chunk), :]
            k = k_s[pl.ds(r0, chunk), :]
            b = b_s[pl.ds(r0, chunk), :]
            vf = v_s[pl.ds(r0, chunk), :].astype(F32)

            def jbody(jj, acc):
                pick = rowid == jj
                kj = jnp.sum(jnp.where(pick, k, 0.0), axis=0, keepdims=True)
                bj = jnp.sum(jnp.where(pick, b, 0.0), axis=0, keepdims=True)
                vj = jnp.sum(jnp.where(pick, vf, 0.0), axis=0, keepdims=True)
                dm = q * kj * jnp.exp(jnp.minimum(b - bj, 0.0))
                dm = jnp.where(rowid >= jj, dm, 0.0)
                return acc + _dot(dm.astype(BF16), headsel) * vj

            o_s[pl.ds(r0, chunk), :] = lax.fori_loop(
                0, chunk, jbody, jnp.zeros((chunk, GLA_VAL), F32))
            return carry

        lax.fori_loop(0, n_idx, cbody, 0)

    def state_step(r0, s):
        q = q_s[pl.ds(r0, chunk), :]
        k = k_s[pl.ds(r0, chunk), :]
        b = b_s[pl.ds(r0, chunk), :]
        v = v_s[pl.ds(r0, chunk), :]
        blast = b[chunk - 1:chunk, :]
        qb = (q * jnp.exp(b)).astype(BF16)
        kb = (k * jnp.exp(blast - b)).astype(BF16)
        dec = jnp.exp(blast)
        for h in range(GLA_HEADS):
            ksl = slice(h * GLA_DK, (h + 1) * GLA_DK)
            vsl = slice(h * GLA_DV, (h + 1) * GLA_DV)
            st = st_s[s, h]
            o_h = o_s[pl.ds(r0, chunk), vsl] + _dot_nt(qb[:, ksl], st.astype(BF16))
            st_s[s, h] = dec[:, ksl] * st + _dot_tn(v[:, vsl], kb[:, ksl])
            o_h = o_h * lax.rsqrt(jnp.mean(o_h * o_h, axis=-1, keepdims=True) + EPS) * gng
            og_s[pl.ds(r0, chunk), vsl] = (
                o_h * sg_s[pl.ds(r0, chunk), vsl].astype(F32)).astype(BF16)

    loop(state_step)

    @pl.when(j == nj - 1)
    def _final_state():
        for s in range(n_seq):
            for h in range(GLA_HEADS):
                sout_ref[s, h] = st_s[s, h].T

    ya = _dot(og_s[...], wgla_ref[...])
    merged = _sigmoid(proj(OFF_GA, D_MODEL)) * ya + yb_s[...].astype(F32)
    m = _dot(merged.astype(BF16), wo_ref[...])
    x1_ref[...] = x_ref[...] + g1 * m.reshape(n_seq, tbl, D_MODEL)


def _const_spec(shape):
    nd = len(shape)
    return pl.BlockSpec(shape, lambda i, j: (0,) * nd, pipeline_mode=pl.Buffered(1))


def _mixer(x, mod, s0, cprev, weights, *, n_seq, tbl, chunk, name):
    n_streams, length, _ = x.shape
    grid = (n_streams // n_seq, length // tbl)
    rows = n_seq * tbl
    (n1g, win, walpha, balpha, gng, wgla, convw, convb, wconv, wo) = weights
    kern = functools.partial(_mixer_kernel, n_seq=n_seq, tbl=tbl, chunk=chunk,
                             unroll=n_seq == 1)
    in_specs = [
        pl.BlockSpec((n_seq, tbl, D_MODEL), lambda i, j: (i, j, 0)),
        pl.BlockSpec((n_seq, N_MOD, D_MODEL), lambda i, j: (i, 0, 0)),
        pl.BlockSpec((n_seq, GLA_HEADS, GLA_DK, GLA_DV), lambda i, j: (i, 0, 0, 0)),
        pl.BlockSpec((n_seq, CONV_W - 1, CONV_DIM), lambda i, j: (i, 0, 0)),
        _const_spec(n1g.shape), _const_spec(win.shape), _const_spec(walpha.shape),
        _const_spec(balpha.shape), _const_spec(gng.shape), _const_spec(wgla.shape),
        _const_spec(convw.shape), _const_spec(convb.shape), _const_spec(wconv.shape),
        _const_spec(wo.shape),
    ]
    out_specs = [
        pl.BlockSpec((n_seq, tbl, D_MODEL), lambda i, j: (i, j, 0)),
        pl.BlockSpec((n_seq, GLA_HEADS, GLA_DK, GLA_DV), lambda i, j: (i, 0, 0, 0)),
        pl.BlockSpec((n_seq, CONV_W - 1, CONV_DIM), lambda i, j: (i, 0, 0)),
    ]
    out_shape = [
        jax.ShapeDtypeStruct(x.shape, F32),
        jax.ShapeDtypeStruct((n_streams, GLA_HEADS, GLA_DK, GLA_DV), F32),
        jax.ShapeDtypeStruct((n_streams, CONV_W - 1, CONV_DIM), F32),
    ]
    scratch = [
        pltpu.VMEM((rows, D_MODEL), BF16),
        pltpu.VMEM((rows, GLA_KEY), F32),
        pltpu.VMEM((rows, GLA_KEY), F32),
        pltpu.VMEM((rows, GLA_VAL), BF16),
        pltpu.VMEM((rows, GLA_VAL), BF16),
        pltpu.VMEM((rows, GLA_KEY), F32),
        pltpu.VMEM((n_seq, tbl + SUBLANES, CONV_DIM), F32),
        pltpu.VMEM((rows, D_MODEL), BF16),
        pltpu.VMEM((rows, GLA_VAL), F32),
        pltpu.VMEM((rows, GLA_VAL), BF16),
        pltpu.VMEM((n_seq, GLA_HEADS, GLA_DV, GLA_DK), F32),
    ]
    return pl.pallas_call(
        kern, grid=grid, in_specs=in_specs, out_specs=out_specs, out_shape=out_shape,
        scratch_shapes=scratch,
        compiler_params=pltpu.CompilerParams(
            dimension_semantics=("arbitrary", "arbitrary"), vmem_limit_bytes=VMEM_LIMIT),
        name=name,
    )(x, mod, s0, cprev, n1g, win, walpha, balpha, gng, wgla, convw, convb, wconv, wo)


FF_GROUP = 256


def _ffn_kernel(x_ref, mod_ref, n2g_ref, win_ref, wout_ref, nfg_ref, y_ref, hb_s, act_s,
                *, n_seq, tbl):
    rows = n_seq * tbl
    x3 = x_ref[...]
    sh2 = mod_ref[:, 3:4, :]
    sc2 = mod_ref[:, 4:5, :]
    g2 = mod_ref[:, 5:6, :]
    ms = jnp.mean(x3 * x3, axis=-1, keepdims=True)
    h3 = x3 * lax.rsqrt(ms + EPS) * n2g_ref[...] * (1.0 + sc2) + sh2
    hb_s[...] = h3.reshape(rows, D_MODEL).astype(BF16)
    for c in range(D_FF // FF_GROUP):
        gt = _dot(hb_s[...], win_ref[:, c * FF_GROUP:(c + 1) * FF_GROUP])
        up = _dot(hb_s[...], win_ref[:, D_FF + c * FF_GROUP:D_FF + (c + 1) * FF_GROUP])
        act_s[:, c * FF_GROUP:(c + 1) * FF_GROUP] = (gt * _sigmoid(gt) * up).astype(BF16)
    f = _dot(act_s[...], wout_ref[...])
    x2 = x3 + g2 * f.reshape(n_seq, tbl, D_MODEL)
    ms2 = jnp.mean(x2 * x2, axis=-1, keepdims=True)
    y_ref[...] = x2 * lax.rsqrt(ms2 + EPS) * nfg_ref[...]


def _ffn(x, mod, n2g, win, wout, nfg, *, n_seq, tbl, name):
    n_streams, length, _ = x.shape
    grid = (n_streams // n_seq, length // tbl)
    rows = n_seq * tbl
    kern = functools.partial(_ffn_kernel, n_seq=n_seq, tbl=tbl)
    return pl.pallas_call(
        kern, grid=grid,
        in_specs=[
            pl.BlockSpec((n_seq, tbl, D_MODEL), lambda i, j: (i, j, 0)),
            pl.BlockSpec((n_seq, N_MOD, D_MODEL), lambda i, j: (i, 0, 0)),
            _const_spec(n2g.shape), _const_spec(win.shape), _const_spec(wout.shape),
            _const_spec(nfg.shape),
        ],
        out_specs=pl.BlockSpec((n_seq, tbl, D_MODEL), lambda i, j: (i, j, 0)),
        out_shape=jax.ShapeDtypeStruct(x.shape, F32),
        scratch_shapes=[pltpu.VMEM((rows, D_MODEL), BF16), pltpu.VMEM((rows, D_FF), BF16)],
        compiler_params=pltpu.CompilerParams(
            dimension_semantics=("arbitrary", "arbitrary"), vmem_limit_bytes=VMEM_LIMIT),
        name=name,
    )(x, mod, n2g, win, wout, nfg)


PACK_ROWS = 128


def _pack_kernel(x_ref, o_ref):
    o_ref[:, :OFF_A] = x_ref[:, :OFF_A].astype(BF16)
    lane = lax.broadcasted_iota(jnp.int32, (1, LANES), 1)
    o_ref[:, OFF_A:OFF_CB] = jnp.where(lane < GLA_RANK, x_ref[:, OFF_A:OFF_CB], 0.0).astype(BF16)
    r = lax.broadcasted_iota(jnp.int32, (2 * LANES, LANES), 0)
    c = lax.broadcasted_iota(jnp.int32, (2 * LANES, LANES), 1)
    shift = (r == c + GLA_RANK).astype(BF16)
    n_tiles = (IN_PAD - OFF_CB) // LANES
    for t in range(n_tiles - 1):
        src = OFF_A + t * LANES
        win = x_ref[:, src:src + 2 * LANES].astype(BF16)
        o_ref[:, OFF_CB + t * LANES:OFF_CB + (t + 1) * LANES] = _dot(win, shift).astype(BF16)
    src = OFF_A + (n_tiles - 1) * LANES
    last = (_dot(x_ref[:, src:src + LANES].astype(BF16), shift[:LANES])
            + _dot(x_ref[:, src + LANES:IN_DIM].astype(BF16),
                   shift[LANES:LANES + GLA_RANK]))
    o_ref[:, IN_PAD - LANES:] = last.astype(BF16)


def _pack_w_in(w_in):
    assert w_in.shape == (D_MODEL, IN_DIM)
    return pl.pallas_call(
        _pack_kernel,
        grid=(D_MODEL // PACK_ROWS,),
        in_specs=[pl.BlockSpec((PACK_ROWS, IN_DIM), lambda i: (i, 0))],
        out_specs=pl.BlockSpec((PACK_ROWS, IN_PAD), lambda i: (i, 0)),
        out_shape=jax.ShapeDtypeStruct((D_MODEL, IN_PAD), BF16),
        compiler_params=pltpu.CompilerParams(vmem_limit_bytes=VMEM_LIMIT),
        name="pack_w_in",
    )(w_in)


def kernel(x_prompt, x_sample, c_prompt, c_sample, state_gla, cache_conv, w_mod, b_mod, norm1_g,
           w_in, w_alpha, b_alpha, gla_norm_g, w_gla_out, conv_w, conv_b, w_conv_out, w_o,
           norm2_g, w_ffn_in, w_ffn_out, norm_f_g):
    bp = x_prompt.shape[0]
    bs = x_sample.shape[0]
    depth = w_mod.shape[0]
    assert depth == 1
    l = 0
    c_all = jnp.concatenate([c_prompt, c_sample], axis=0)
    n_c = c_all.shape[0]
    n_pad = -n_c % SUBLANES
    c_all = jnp.concatenate([c_all, jnp.zeros((n_pad, D_MODEL), F32)], axis=0)
    mod = _modulation(c_all, w_mod[l], b_mod[l]).reshape(n_c + n_pad, N_MOD, D_MODEL)
    mod_p, mod_s = mod[:bp], mod[bp:bp + bs]

    walpha = jnp.concatenate(
        [w_alpha[l], jnp.zeros((RANK_PAD - GLA_RANK, GLA_KEY), F32)], axis=0).astype(BF16)
    mix_w = (norm1_g[l].reshape(1, -1), _pack_w_in(w_in[l]), walpha, b_alpha[l].reshape(1, -1),
             gla_norm_g[l].reshape(1, -1), w_gla_out[l].astype(BF16), conv_w[l],
             conv_b[l].reshape(1, -1), w_conv_out[l].astype(BF16), w_o[l].astype(BF16))
    ffn_w = (norm2_g[l].reshape(1, -1), w_ffn_in[l].astype(BF16), w_ffn_out[l].astype(BF16),
             norm_f_g.reshape(1, -1))

    s0_p = jnp.zeros((bp, GLA_HEADS, GLA_DK, GLA_DV), F32)
    c0_p = jnp.zeros((bp, CONV_W - 1, CONV_DIM), F32)
    x1_p, st_p, cv_p = _mixer(x_prompt, mod_p, s0_p, c0_p, mix_w,
                              n_seq=1, tbl=512, chunk=128, name="mixer_prompt")
    x1_s, st_s, cv_s = _mixer(x_sample, mod_s, state_gla[l], cache_conv[l], mix_w,
                              n_seq=8, tbl=x_sample.shape[1], chunk=x_sample.shape[1],
                              name="mixer_sample")
    y_p = _ffn(x1_p, mod_p, *ffn_w, n_seq=1, tbl=512, name="ffn_prompt")
    y_s = _ffn(x1_s, mod_s, *ffn_w, n_seq=bs, tbl=x_sample.shape[1], name="ffn_sample")
    return (y_p, y_s, st_p[None], cv_p[None], st_s[None], cv_s[None])
```

```python
import functools

import jax
import jax.numpy as jnp
from jax import lax
from jax.experimental import pallas as pl
from jax.experimental.pallas import tpu as pltpu

F32 = jnp.float32
BF16 = jnp.bfloat16

D_MODEL = 1024
GLA_HEADS = 4
GLA_DK = 128
GLA_DV = 256
GLA_KEY = GLA_HEADS * GLA_DK
GLA_VAL = GLA_HEADS * GLA_DV
GLA_RANK = 16
GLA_TAU = 16.0
CONV_DIM = D_MODEL
CONV_W = 3
D_FF = 2816
N_MOD = 6
EPS = 1e-6

LANES = 128
SUBLANES = 8

OFF_Q = 0
OFF_K = OFF_Q + GLA_KEY
OFF_V = OFF_K + GLA_KEY
OFF_G = OFF_V + GLA_VAL
OFF_A = OFF_G + GLA_VAL
OFF_CB = OFF_A + GLA_RANK
OFF_CC = OFF_CB + CONV_DIM
OFF_CH = OFF_CC + CONV_DIM
OFF_GA = OFF_CH + CONV_DIM
OFF_GB = OFF_GA + D_MODEL
IN_DIM = OFF_GB + D_MODEL

SAFE_LOG_DECAY = 60.0

VMEM_LIMIT = 60 * 1024 * 1024


def _dot(a, b):
    return jnp.dot(a, b, preferred_element_type=F32)


def _dot_nt(a, b):
    return lax.dot_general(a, b, (((1,), (1,)), ((), ())), preferred_element_type=F32)


def _dot_tn(a, b):
    return lax.dot_general(a, b, (((0,), (0,)), ((), ())), preferred_element_type=F32)


def _sigmoid(x):
    return 1.0 / (1.0 + jnp.exp(-x))


def _mod_kernel(c_ref, w_ref, b_ref, o_ref):
    o_ref[...] = _dot(c_ref[...].astype(BF16), w_ref[...].astype(BF16)) + b_ref[...]


def _modulation(c_all, w_mod, b_mod):
    n = c_all.shape[0]
    nblk = N_MOD
    return pl.pallas_call(
        _mod_kernel,
        grid=(nblk,),
        in_specs=[
            pl.BlockSpec((n, D_MODEL), lambda j: (0, 0)),
            pl.BlockSpec((D_MODEL, D_MODEL), lambda j: (0, j)),
            pl.BlockSpec((1, D_MODEL), lambda j: (0, j)),
        ],
        out_specs=pl.BlockSpec((n, D_MODEL), lambda j: (0, j)),
        out_shape=jax.ShapeDtypeStruct((n, N_MOD * D_MODEL), F32),
        name="adaln_mod",
    )(c_all, w_mod, b_mod.reshape(1, -1))


def _mixer_kernel(x_ref, mod_ref, s0_ref, cprev_ref, n1g_ref, win_ref, walpha_ref, balpha_ref,
                  gng_ref, wgla_ref, convw_ref, convb_ref, wconv_ref, wo_ref,
                  x1_ref, sout_ref, cout_ref,
                  hb_s, q_s, k_s, v_s, sg_s, b_s, ubuf_s, yb_s, o_s, og_s, st_s,
                  *, n_seq, tbl, chunk, unroll):
    rows = n_seq * tbl
    nchunk = tbl // chunk
    j = pl.program_id(1)
    nj = pl.num_programs(1)

    @pl.when(j == 0)
    def _init():
        for s in range(n_seq):
            for h in range(GLA_HEADS):
                st_s[s, h] = s0_ref[s, h].T
        ubuf_s[:, SUBLANES - 2:SUBLANES, :] = cprev_ref[...]

    x3 = x_ref[...]
    sh1 = mod_ref[:, 0:1, :]
    sc1 = mod_ref[:, 1:2, :]
    g1 = mod_ref[:, 2:3, :]
    ms = jnp.mean(x3 * x3, axis=-1, keepdims=True)
    h3 = x3 * lax.rsqrt(ms + EPS) * n1g_ref[...] * (1.0 + sc1) + sh1
    hb_s[...] = h3.reshape(rows, D_MODEL).astype(BF16)

    def proj(off, n):
        return _dot_nt(hb_s[...], win_ref[off:off + n, :])

    q_s[...] = proj(OFF_Q, GLA_KEY) * (GLA_DK ** -0.5)
    k_s[...] = proj(OFF_K, GLA_KEY)
    v_s[...] = proj(OFF_V, GLA_VAL).astype(BF16)
    g = proj(OFF_G, GLA_VAL)
    sg_s[...] = (g * _sigmoid(g)).astype(BF16)
    za = proj(OFF_A, GLA_RANK)
    xa = _dot(za.astype(BF16), walpha_ref[...]) + balpha_ref[...]
    la = (jnp.minimum(xa, 0.0) - jnp.log1p(jnp.exp(-jnp.abs(xa)))) * (1.0 / GLA_TAU)

    u = proj(OFF_CC, CONV_DIM) * proj(OFF_CH, CONV_DIM)
    ubuf_s[:, SUBLANES:, :] = u.reshape(n_seq, tbl, CONV_DIM)
    conv = (ubuf_s[:, SUBLANES - 2:SUBLANES - 2 + tbl, :] * convw_ref[0:1, :]
            + ubuf_s[:, SUBLANES - 1:SUBLANES - 1 + tbl, :] * convw_ref[1:2, :]
            + ubuf_s[:, SUBLANES:SUBLANES + tbl, :] * convw_ref[2:3, :]
            + convb_ref[...])
    tail = ubuf_s[:, tbl + SUBLANES - 2:tbl + SUBLANES, :]
    ubuf_s[:, SUBLANES - 2:SUBLANES, :] = tail
    cout_ref[...] = tail
    cb = proj(OFF_CB, CONV_DIM)
    ub = (cb.reshape(n_seq, tbl, CONV_DIM) * conv).reshape(rows, CONV_DIM).astype(BF16)
    yb = _dot(ub, wconv_ref[...])
    yb_s[...] = (_sigmoid(proj(OFF_GB, D_MODEL)) * yb).astype(BF16)

    ri = lax.broadcasted_iota(jnp.int32, (chunk, chunk), 0)
    ci = lax.broadcasted_iota(jnp.int32, (chunk, chunk), 1)
    tril = ri >= ci
    tmat = tril.astype(BF16)
    bmin = jnp.zeros((1, GLA_KEY), F32)
    for idx in range(n_seq * nchunk):
        la_c = la[idx * chunk:(idx + 1) * chunk, :]
        hi = la_c.astype(BF16)
        lo = (la_c - hi.astype(F32)).astype(BF16)
        b_c = _dot(tmat, hi) + _dot(tmat, lo)
        b_s[idx * chunk:(idx + 1) * chunk, :] = b_c
        bmin = jnp.minimum(bmin, b_c[chunk - 1:chunk, :])
    safe = jnp.min(bmin) > -SAFE_LOG_DECAY

    gng = gng_ref[...]
    n_idx = n_seq * nchunk

    def loop(body):
        if unroll:
            for idx in range(n_idx):
                body(idx * chunk, idx // nchunk)
        else:
            def fbody(idx, carry):
                body(pl.multiple_of(idx * chunk, chunk), idx // nchunk)
                return carry
            lax.fori_loop(0, n_idx, fbody, 0)

    def intra_fast(r0, s):
        q = q_s[pl.ds(r0, chunk), :]
        k = k_s[pl.ds(r0, chunk), :]
        b = b_s[pl.ds(r0, chunk), :]
        v = v_s[pl.ds(r0, chunk), :]
        bmid = b[chunk // 2 - 1:chunk // 2, :]
        qs = (q * jnp.exp(b - bmid)).astype(BF16)
        ks = (k * jnp.exp(bmid - b)).astype(BF16)
        for h in range(GLA_HEADS):
            ksl = slice(h * GLA_DK, (h + 1) * GLA_DK)
            vsl = slice(h * GLA_DV, (h + 1) * GLA_DV)
            a = jnp.where(tril, _dot_nt(qs[:, ksl], ks[:, ksl]), 0.0)
            o_s[pl.ds(r0, chunk), vsl] = _dot(a.astype(BF16), v[:, vsl])

    loop(intra_fast)

    @pl.when(jnp.logical_not(safe))
    def _pairwise():
        hr = lax.broadcasted_iota(jnp.int32, (GLA_KEY, GLA_VAL), 0) // GLA_DK
        hc = lax.broadcasted_iota(jnp.int32, (GLA_KEY, GLA_VAL), 1) // GLA_DV
        headsel = (hr == hc).astype(BF16)
        rowid = lax.broadcasted_iota(jnp.int32, (chunk, 1), 0)

        def cbody(idx, carry):
            r0 = pl.multiple_of(idx * chunk, chunk)
            q = q_s[pl.ds(r0, chunk), :]
            k = k_s[pl.ds(r0, chunk), :]
            b = b_s[pl.ds(r0, chunk), :]
            vf = v_s[pl.ds(r0, chunk), :].astype(F32)

            def jbody(jj, acc):
                pick = rowid == jj
                kj = jnp.sum(jnp.where(pick, k, 0.0), axis=0, keepdims=True)
                bj = jnp.sum(jnp.where(pick, b, 0.0), axis=0, keepdims=True)
                vj = jnp.sum(jnp.where(pick, vf, 0.0), axis=0, keepdims=True)
                dm = q * kj * jnp.exp(jnp.minimum(b - bj, 0.0))
                dm = jnp.where(rowid >= jj, dm, 0.0)
                return acc + _dot(dm.astype(BF16), headsel) * vj

            o_s[pl.ds(r0, chunk), :] = lax.fori_loop(
                0, chunk, jbody, jnp.zeros((chunk, GLA_VAL), F32))
            return carry

        lax.fori_loop(0, n_idx, cbody, 0)

    def state_step(r0, s):
        q = q_s[pl.ds(r0, chunk), :]
        k = k_s[pl.ds(r0, chunk), :]
        b = b_s[pl.ds(r0, chunk), :]
        v = v_s[pl.ds(r0, chunk), :]
        blast = b[chunk - 1:chunk, :]
        qb = (q * jnp.exp(b)).astype(BF16)
        kb = (k * jnp.exp(blast - b)).astype(BF16)
        dec = jnp.exp(blast)
        for h in range(GLA_HEADS):
            ksl = slice(h * GLA_DK, (h + 1) * GLA_DK)
            vsl = slice(h * GLA_DV, (h + 1) * GLA_DV)
            st = st_s[s, h]
            o_h = o_s[pl.ds(r0, chunk), vsl] + _dot_nt(qb[:, ksl], st.astype(BF16))
            st_s[s, h] = dec[:, ksl] * st + _dot_tn(v[:, vsl], kb[:, ksl])
            o_h = o_h * lax.rsqrt(jnp.mean(o_h * o_h, axis=-1, keepdims=True) + EPS) * gng
            og_s[pl.ds(r0, chunk), vsl] = (
                o_h * sg_s[pl.ds(r0, chunk), vsl].astype(F32)).astype(BF16)

    loop(state_step)

    @pl.when(j == nj - 1)
    def _final_state():
        for s in range(n_seq):
            for h in range(GLA_HEADS):
                sout_ref[s, h] = st_s[s, h].T

    ya = _dot(og_s[...], wgla_ref[...])
    merged = _sigmoid(proj(OFF_GA, D_MODEL)) * ya + yb_s[...].astype(F32)
    m = _dot(merged.astype(BF16), wo_ref[...])
    x1_ref[...] = x_ref[...] + g1 * m.reshape(n_seq, tbl, D_MODEL)


def _const_spec(shape):
    nd = len(shape)
    return pl.BlockSpec(shape, lambda i, j: (0,) * nd, pipeline_mode=pl.Buffered(1))


def _mixer(x, mod, s0, cprev, weights, *, n_seq, tbl, chunk, name):
    n_streams, length, _ = x.shape
    grid = (n_streams // n_seq, length // tbl)
    rows = n_seq * tbl
    (n1g, win, walpha, balpha, gng, wgla, convw, convb, wconv, wo) = weights
    kern = functools.partial(_mixer_kernel, n_seq=n_seq, tbl=tbl, chunk=chunk,
                             unroll=n_seq == 1)
    in_specs = [
        pl.BlockSpec((n_seq, tbl, D_MODEL), lambda i, j: (i, j, 0)),
        pl.BlockSpec((n_seq, N_MOD, D_MODEL), lambda i, j: (i, 0, 0)),
        pl.BlockSpec((n_seq, GLA_HEADS, GLA_DK, GLA_DV), lambda i, j: (i, 0, 0, 0)),
        pl.BlockSpec((n_seq, CONV_W - 1, CONV_DIM), lambda i, j: (i, 0, 0)),
        _const_spec(n1g.shape), _const_spec(win.shape), _const_spec(walpha.shape),
        _const_spec(balpha.shape), _const_spec(gng.shape), _const_spec(wgla.shape),
        _const_spec(convw.shape), _const_spec(convb.shape), _const_spec(wconv.shape),
        _const_spec(wo.shape),
    ]
    out_specs = [
        pl.BlockSpec((n_seq, tbl, D_MODEL), lambda i, j: (i, j, 0)),
        pl.BlockSpec((n_seq, GLA_HEADS, GLA_DK, GLA_DV), lambda i, j: (i, 0, 0, 0)),
        pl.BlockSpec((n_seq, CONV_W - 1, CONV_DIM), lambda i, j: (i, 0, 0)),
    ]
    out_shape = [
        jax.ShapeDtypeStruct(x.shape, F32),
        jax.ShapeDtypeStruct((n_streams, GLA_HEADS, GLA_DK, GLA_DV), F32),
        jax.ShapeDtypeStruct((n_streams, CONV_W - 1, CONV_DIM), F32),
    ]
    scratch = [
        pltpu.VMEM((rows, D_MODEL), BF16),
        pltpu.VMEM((rows, GLA_KEY), F32),
        pltpu.VMEM((rows, GLA_KEY), F32),
        pltpu.VMEM((rows, GLA_VAL), BF16),
        pltpu.VMEM((rows, GLA_VAL), BF16),
        pltpu.VMEM((rows, GLA_KEY), F32),
        pltpu.VMEM((n_seq, tbl + SUBLANES, CONV_DIM), F32),
        pltpu.VMEM((rows, D_MODEL), BF16),
        pltpu.VMEM((rows, GLA_VAL), F32),
        pltpu.VMEM((rows, GLA_VAL), BF16),
        pltpu.VMEM((n_seq, GLA_HEADS, GLA_DV, GLA_DK), F32),
    ]
    return pl.pallas_call(
        kern, grid=grid, in_specs=in_specs, out_specs=out_specs, out_shape=out_shape,
        scratch_shapes=scratch,
        compiler_params=pltpu.CompilerParams(
            dimension_semantics=("arbitrary", "arbitrary"), vmem_limit_bytes=VMEM_LIMIT),
        name=name,
    )(x, mod, s0, cprev, n1g, win, walpha, balpha, gng, wgla, convw, convb, wconv, wo)


FF_GROUP = 256


def _ffn_kernel(x_ref, mod_ref, n2g_ref, win_ref, wout_ref, nfg_ref, y_ref, hb_s, act_s,
                *, n_seq, tbl):
    rows = n_seq * tbl
    x3 = x_ref[...]
    sh2 = mod_ref[:, 3:4, :]
    sc2 = mod_ref[:, 4:5, :]
    g2 = mod_ref[:, 5:6, :]
    ms = jnp.mean(x3 * x3, axis=-1, keepdims=True)
    h3 = x3 * lax.rsqrt(ms + EPS) * n2g_ref[...] * (1.0 + sc2) + sh2
    hb_s[...] = h3.reshape(rows, D_MODEL).astype(BF16)
    for c in range(D_FF // FF_GROUP):
        gt = _dot(hb_s[...], win_ref[:, c * FF_GROUP:(c + 1) * FF_GROUP])
        up = _dot(hb_s[...], win_ref[:, D_FF + c * FF_GROUP:D_FF + (c + 1) * FF_GROUP])
        act_s[:, c * FF_GROUP:(c + 1) * FF_GROUP] = (gt * _sigmoid(gt) * up).astype(BF16)
    f = _dot(act_s[...], wout_ref[...])
    x2 = x3 + g2 * f.reshape(n_seq, tbl, D_MODEL)
    ms2 = jnp.mean(x2 * x2, axis=-1, keepdims=True)
    y_ref[...] = x2 * lax.rsqrt(ms2 + EPS) * nfg_ref[...]


def _ffn(x, mod, n2g, win, wout, nfg, *, n_seq, tbl, name):
    n_streams, length, _ = x.shape
    grid = (n_streams // n_seq, length // tbl)
    rows = n_seq * tbl
    kern = functools.partial(_ffn_kernel, n_seq=n_seq, tbl=tbl)
    return pl.pallas_call(
        kern, grid=grid,
        in_specs=[
            pl.BlockSpec((n_seq, tbl, D_MODEL), lambda i, j: (i, j, 0)),
            pl.BlockSpec((n_seq, N_MOD, D_MODEL), lambda i, j: (i, 0, 0)),
            _const_spec(n2g.shape), _const_spec(win.shape), _const_spec(wout.shape),
            _const_spec(nfg.shape),
        ],
        out_specs=pl.BlockSpec((n_seq, tbl, D_MODEL), lambda i, j: (i, j, 0)),
        out_shape=jax.ShapeDtypeStruct(x.shape, F32),
        scratch_shapes=[pltpu.VMEM((rows, D_MODEL), BF16), pltpu.VMEM((rows, D_FF), BF16)],
        compiler_params=pltpu.CompilerParams(
            dimension_semantics=("arbitrary", "arbitrary"), vmem_limit_bytes=VMEM_LIMIT),
        name=name,
    )(x, mod, n2g, win, wout, nfg)


def kernel(x_prompt, x_sample, c_prompt, c_sample, state_gla, cache_conv, w_mod, b_mod, norm1_g,
           w_in, w_alpha, b_alpha, gla_norm_g, w_gla_out, conv_w, conv_b, w_conv_out, w_o,
           norm2_g, w_ffn_in, w_ffn_out, norm_f_g):
    bp = x_prompt.shape[0]
    bs = x_sample.shape[0]
    depth = w_mod.shape[0]
    assert depth == 1
    l = 0
    c_all = jnp.concatenate([c_prompt, c_sample], axis=0)
    n_c = c_all.shape[0]
    n_pad = -n_c % SUBLANES
    c_all = jnp.concatenate([c_all, jnp.zeros((n_pad, D_MODEL), F32)], axis=0)
    mod = _modulation(c_all, w_mod[l], b_mod[l]).reshape(n_c + n_pad, N_MOD, D_MODEL)
    mod_p, mod_s = mod[:bp], mod[bp:bp + bs]

    assert w_in.shape[1:] == (D_MODEL, IN_DIM)
    win_t = jnp.swapaxes(w_in[l], 0, 1).astype(BF16)
    walpha = w_alpha[l].astype(BF16)
    mix_w = (norm1_g[l].reshape(1, -1), win_t, walpha, b_alpha[l].reshape(1, -1),
             gla_norm_g[l].reshape(1, -1), w_gla_out[l].astype(BF16), conv_w[l],
             conv_b[l].reshape(1, -1), w_conv_out[l].astype(BF16), w_o[l].astype(BF16))
    ffn_w = (norm2_g[l].reshape(1, -1), w_ffn_in[l].astype(BF16), w_ffn_out[l].astype(BF16),
             norm_f_g.reshape(1, -1))

    s0_p = jnp.zeros((bp, GLA_HEADS, GLA_DK, GLA_DV), F32)
    c0_p = jnp.zeros((bp, CONV_W - 1, CONV_DIM), F32)
    x1_p, st_p, cv_p = _mixer(x_prompt, mod_p, s0_p, c0_p, mix_w,
                              n_seq=1, tbl=512, chunk=128, name="mixer_prompt")
    x1_s, st_s, cv_s = _mixer(x_sample, mod_s, state_gla[l], cache_conv[l], mix_w,
                              n_seq=8, tbl=x_sample.shape[1], chunk=x_sample.shape[1],
                              name="mixer_sample")
    y_p = _ffn(x1_p, mod_p, *ffn_w, n_seq=1, tbl=512, name="ffn_prompt")
    y_s = _ffn(x1_s, mod_s, *ffn_w, n_seq=bs, tbl=x_sample.shape[1], name="ffn_sample")
    return (y_p, y_s, st_p[None], cv_p[None], st_s[None], cv_s[None])
```

```python
import functools

import jax
import jax.numpy as jnp
from jax import lax
from jax.experimental import pallas as pl
from jax.experimental.pallas import tpu as pltpu

F32 = jnp.float32
BF16 = jnp.bfloat16

D_MODEL = 1024
GLA_HEADS = 4
GLA_DK = 128
GLA_DV = 256
GLA_KEY = GLA_HEADS * GLA_DK
GLA_VAL = GLA_HEADS * GLA_DV
GLA_RANK = 16
GLA_TAU = 16.0
CONV_DIM = D_MODEL
CONV_W = 3
D_FF = 2816
N_MOD = 6
EPS = 1e-6

COL_GROUP = 256
LANES = 128
SUBLANES = 8

OFF_Q = 0
OFF_K = OFF_Q + GLA_KEY
OFF_V = OFF_K + GLA_KEY
OFF_G = OFF_V + GLA_VAL
OFF_A = OFF_G + GLA_VAL
OFF_CB = OFF_A + GLA_RANK
OFF_CC = OFF_CB + CONV_DIM
OFF_CH = OFF_CC + CONV_DIM
OFF_GA = OFF_CH + CONV_DIM
OFF_GB = OFF_GA + D_MODEL
IN_DIM = OFF_GB + D_MODEL

SAFE_LOG_DECAY = 60.0

VMEM_LIMIT = 60 * 1024 * 1024


def _dot(a, b):
    return jnp.dot(a, b, preferred_element_type=F32)


def _dot_nt(a, b):
    return lax.dot_general(a, b, (((1,), (1,)), ((), ())), preferred_element_type=F32)


def _dot_tn(a, b):
    return lax.dot_general(a, b, (((0,), (0,)), ((), ())), preferred_element_type=F32)


def _sigmoid(x):
    return 1.0 / (1.0 + jnp.exp(-x))


def _mod_kernel(c_ref, w_ref, b_ref, o_ref):
    o_ref[...] = _dot(c_ref[...].astype(BF16), w_ref[...].astype(BF16)) + b_ref[...]


def _modulation(c_all, w_mod, b_mod):
    n = c_all.shape[0]
    nblk = N_MOD
    return pl.pallas_call(
        _mod_kernel,
        grid=(nblk,),
        in_specs=[
            pl.BlockSpec((n, D_MODEL), lambda j: (0, 0)),
            pl.BlockSpec((D_MODEL, D_MODEL), lambda j: (0, j)),
            pl.BlockSpec((1, D_MODEL), lambda j: (0, j)),
        ],
        out_specs=pl.BlockSpec((n, D_MODEL), lambda j: (0, j)),
        out_shape=jax.ShapeDtypeStruct((n, N_MOD * D_MODEL), F32),
        name="adaln_mod",
    )(c_all, w_mod, b_mod.reshape(1, -1))


def _mixer_kernel(x_ref, mod_ref, s0_ref, cprev_ref, n1g_ref, win_ref, walpha_ref, balpha_ref,
                  gng_ref, wgla_ref, convw_ref, convb_ref, wconv_ref, wo_ref,
                  x1_ref, sout_ref, cout_ref,
                  hb_s, q_s, k_s, v_s, sg_s, b_s, ubuf_s, ub_s, yb_s, o_s, og_s, st_s,
                  *, n_seq, tbl, chunk, unroll):
    rows = n_seq * tbl
    nchunk = tbl // chunk
    j = pl.program_id(1)
    nj = pl.num_programs(1)

    @pl.when(j == 0)
    def _init():
        for s in range(n_seq):
            for h in range(GLA_HEADS):
                st_s[s, h] = s0_ref[s, h].T
        ubuf_s[:, SUBLANES - 2:SUBLANES, :] = cprev_ref[...]

    x3 = x_ref[...]
    sh1 = mod_ref[:, 0:1, :]
    sc1 = mod_ref[:, 1:2, :]
    g1 = mod_ref[:, 2:3, :]
    ms = jnp.mean(x3 * x3, axis=-1, keepdims=True)
    h3 = x3 * lax.rsqrt(ms + EPS) * n1g_ref[...] * (1.0 + sc1) + sh1
    hb_s[...] = h3.reshape(rows, D_MODEL).astype(BF16)

    def proj(off, c, n=COL_GROUP):
        return _dot_nt(hb_s[...], win_ref[off + c * n:off + (c + 1) * n, :])

    def cols(c):
        return slice(c * COL_GROUP, (c + 1) * COL_GROUP)

    za = proj(OFF_A, 0, GLA_RANK)
    xa = _dot(za.astype(BF16), walpha_ref[...]) + balpha_ref[...]
    for c in range(GLA_KEY // COL_GROUP):
        q_s[:, cols(c)] = proj(OFF_Q, c) * (GLA_DK ** -0.5)
    la = (jnp.minimum(xa, 0.0) - jnp.log(1.0 + jnp.exp(-jnp.abs(xa)))) * (1.0 / GLA_TAU)
    for c in range(GLA_KEY // COL_GROUP):
        k_s[:, cols(c)] = proj(OFF_K, c)

    ri = lax.broadcasted_iota(jnp.int32, (chunk, chunk), 0)
    ci = lax.broadcasted_iota(jnp.int32, (chunk, chunk), 1)
    tril = ri >= ci
    tmat = tril.astype(BF16)
    bmin = jnp.zeros((1, GLA_KEY), F32)
    for idx in range(n_seq * nchunk):
        la_c = la[idx * chunk:(idx + 1) * chunk, :]
        hi = la_c.astype(BF16)
        lo = (la_c - hi.astype(F32)).astype(BF16)
        b_c = _dot(tmat, hi) + _dot(tmat, lo)
        b_s[idx * chunk:(idx + 1) * chunk, :] = b_c
        bmin = jnp.minimum(bmin, b_c[chunk - 1:chunk, :])
    safe = jnp.min(bmin) > -SAFE_LOG_DECAY

    for c in range(GLA_VAL // COL_GROUP):
        v_s[:, cols(c)] = proj(OFF_V, c).astype(BF16)
        g = proj(OFF_G, c)
        sg_s[:, cols(c)] = (g * _sigmoid(g)).astype(BF16)

    gng = gng_ref[...]
    n_idx = n_seq * nchunk

    def loop(body):
        if unroll:
            for idx in range(n_idx):
                body(idx * chunk, idx // nchunk)
        else:
            def fbody(idx, carry):
                body(pl.multiple_of(idx * chunk, chunk), idx // nchunk)
                return carry
            lax.fori_loop(0, n_idx, fbody, 0)

    def intra_fast(r0, s):
        q = q_s[pl.ds(r0, chunk), :]
        k = k_s[pl.ds(r0, chunk), :]
        b = b_s[pl.ds(r0, chunk), :]
        v = v_s[pl.ds(r0, chunk), :]
        bmid = b[chunk // 2 - 1:chunk // 2, :]
        qs = (q * jnp.exp(b - bmid)).astype(BF16)
        ks = (k * jnp.exp(bmid - b)).astype(BF16)
        for h in range(GLA_HEADS):
            ksl = slice(h * GLA_DK, (h + 1) * GLA_DK)
            vsl = slice(h * GLA_DV, (h + 1) * GLA_DV)
            a = jnp.where(tril, _dot_nt(qs[:, ksl], ks[:, ksl]), 0.0)
            o_s[pl.ds(r0, chunk), vsl] = _dot(a.astype(BF16), v[:, vsl])

    def conv_group(c):
        u = proj(OFF_CC, c) * proj(OFF_CH, c)
        ubuf_s[:, SUBLANES:, cols(c)] = u.reshape(n_seq, tbl, COL_GROUP)
        conv = (ubuf_s[:, SUBLANES - 2:SUBLANES - 2 + tbl, cols(c)] * convw_ref[0:1, cols(c)]
                + ubuf_s[:, SUBLANES - 1:SUBLANES - 1 + tbl, cols(c)] * convw_ref[1:2, cols(c)]
                + u.reshape(n_seq, tbl, COL_GROUP) * convw_ref[2:3, cols(c)]
                + convb_ref[:, cols(c)])
        cb = proj(OFF_CB, c)
        ub_s[:, cols(c)] = (cb.reshape(n_seq, tbl, COL_GROUP) * conv).reshape(
            rows, COL_GROUP).astype(BF16)

    loop(intra_fast)

    @pl.when(jnp.logical_not(safe))
    def _pairwise():
        hr = lax.broadcasted_iota(jnp.int32, (GLA_KEY, GLA_VAL), 0) // GLA_DK
        hc = lax.broadcasted_iota(jnp.int32, (GLA_KEY, GLA_VAL), 1) // GLA_DV
        headsel = (hr == hc).astype(BF16)
        rowid = lax.broadcasted_iota(jnp.int32, (chunk, 1), 0)

        def cbody(idx, carry):
            r0 = pl.multiple_of(idx * chunk, chunk)
            q = q_s[pl.ds(r0, chunk), :]
            k = k_s[pl.ds(r0, chunk), :]
            b = b_s[pl.ds(r0, chunk), :]
            vf = v_s[pl.ds(r0, chunk), :].astype(F32)

            def jbody(jj, acc):
                pick = rowid == jj
                kj = jnp.sum(jnp.where(pick, k, 0.0), axis=0, keepdims=True)
                bj = jnp.sum(jnp.where(pick, b, 0.0), axis=0, keepdims=True)
                vj = jnp.sum(jnp.where(pick, vf, 0.0), axis=0, keepdims=True)
                dm = q * kj * jnp.exp(jnp.minimum(b - bj, 0.0))
                dm = jnp.where(rowid >= jj, dm, 0.0)
                return acc + _dot(dm.astype(BF16), headsel) * vj

            o_s[pl.ds(r0, chunk), :] = lax.fori_loop(
                0, chunk, jbody, jnp.zeros((chunk, GLA_VAL), F32))
            return carry

        lax.fori_loop(0, n_idx, cbody, 0)

    def state_step(r0, s):
        q = q_s[pl.ds(r0, chunk), :]
        k = k_s[pl.ds(r0, chunk), :]
        b = b_s[pl.ds(r0, chunk), :]
        v = v_s[pl.ds(r0, chunk), :]
        blast = b[chunk - 1:chunk, :]
        qb = (q * jnp.exp(b)).astype(BF16)
        kb = (k * jnp.exp(blast - b)).astype(BF16)
        dec = jnp.exp(blast)
        for h in range(GLA_HEADS):
            ksl = slice(h * GLA_DK, (h + 1) * GLA_DK)
            vsl = slice(h * GLA_DV, (h + 1) * GLA_DV)
            st = st_s[s, h]
            o_h = o_s[pl.ds(r0, chunk), vsl] + _dot_nt(qb[:, ksl], st.astype(BF16))
            st_s[s, h] = dec[:, ksl] * st + _dot_tn(v[:, vsl], kb[:, ksl])
            o_h = o_h * lax.rsqrt(jnp.mean(o_h * o_h, axis=-1, keepdims=True) + EPS) * gng
            og_s[pl.ds(r0, chunk), vsl] = (
                o_h * sg_s[pl.ds(r0, chunk), vsl].astype(F32)).astype(BF16)

    loop(state_step)

    for c in range(CONV_DIM // COL_GROUP):
        conv_group(c)
    tail = ubuf_s[:, tbl + SUBLANES - 2:tbl + SUBLANES, :]
    ubuf_s[:, SUBLANES - 2:SUBLANES, :] = tail
    cout_ref[...] = tail
    for c in range(D_MODEL // COL_GROUP):
        yb = _dot(ub_s[...], wconv_ref[:, cols(c)])
        yb_s[:, cols(c)] = (_sigmoid(proj(OFF_GB, c)) * yb).astype(BF16)

    for c in range(D_MODEL // COL_GROUP):
        ya = _dot(og_s[...], wgla_ref[:, cols(c)])
        merged = _sigmoid(proj(OFF_GA, c)) * ya + yb_s[:, cols(c)].astype(F32)
        ub_s[:, cols(c)] = merged.astype(BF16)
    for c in range(D_MODEL // COL_GROUP):
        m = _dot(ub_s[...], wo_ref[:, cols(c)])
        x1_ref[:, :, cols(c)] = (x_ref[:, :, cols(c)]
                                 + g1[:, :, cols(c)] * m.reshape(n_seq, tbl, COL_GROUP))

    @pl.when(j == nj - 1)
    def _final_state():
        for s in range(n_seq):
            for h in range(GLA_HEADS):
                sout_ref[s, h] = st_s[s, h].T


def _const_spec(shape):
    nd = len(shape)
    return pl.BlockSpec(shape, lambda i, j: (0,) * nd, pipeline_mode=pl.Buffered(1))


def _mixer(x, mod, s0, cprev, weights, *, n_seq, tbl, chunk, name):
    n_streams, length, _ = x.shape
    assert n_streams % n_seq == 0 and length % tbl == 0
    grid = (n_streams // n_seq, length // tbl)
    rows = n_seq * tbl
    (n1g, win, walpha, balpha, gng, wgla, convw, convb, wconv, wo) = weights
    kern = functools.partial(_mixer_kernel, n_seq=n_seq, tbl=tbl, chunk=chunk,
                             unroll=n_seq == 1)
    in_specs = [
        pl.BlockSpec((n_seq, tbl, D_MODEL), lambda i, j: (i, j, 0)),
        pl.BlockSpec((n_seq, N_MOD, D_MODEL), lambda i, j: (i, 0, 0)),
        pl.BlockSpec((n_seq, GLA_HEADS, GLA_DK, GLA_DV), lambda i, j: (i, 0, 0, 0)),
        pl.BlockSpec((n_seq, CONV_W - 1, CONV_DIM), lambda i, j: (i, 0, 0)),
        _const_spec(n1g.shape), _const_spec(win.shape), _const_spec(walpha.shape),
        _const_spec(balpha.shape), _const_spec(gng.shape), _const_spec(wgla.shape),
        _const_spec(convw.shape), _const_spec(convb.shape), _const_spec(wconv.shape),
        _const_spec(wo.shape),
    ]
    out_specs = [
        pl.BlockSpec((n_seq, tbl, D_MODEL), lambda i, j: (i, j, 0)),
        pl.BlockSpec((n_seq, GLA_HEADS, GLA_DK, GLA_DV), lambda i, j: (i, 0, 0, 0)),
        pl.BlockSpec((n_seq, CONV_W - 1, CONV_DIM), lambda i, j: (i, 0, 0)),
    ]
    out_shape = [
        jax.ShapeDtypeStruct(x.shape, F32),
        jax.ShapeDtypeStruct((n_streams, GLA_HEADS, GLA_DK, GLA_DV), F32),
        jax.ShapeDtypeStruct((n_streams, CONV_W - 1, CONV_DIM), F32),
    ]
    scratch = [
        pltpu.VMEM((rows, D_MODEL), BF16),
        pltpu.VMEM((rows, GLA_KEY), F32),
        pltpu.VMEM((rows, GLA_KEY), F32),
        pltpu.VMEM((rows, GLA_VAL), BF16),
        pltpu.VMEM((rows, GLA_VAL), BF16),
        pltpu.VMEM((rows, GLA_KEY), F32),
        pltpu.VMEM((n_seq, tbl + SUBLANES, CONV_DIM), F32),
        pltpu.VMEM((rows, D_MODEL), BF16),
        pltpu.VMEM((rows, D_MODEL), BF16),
        pltpu.VMEM((rows, GLA_VAL), F32),
        pltpu.VMEM((rows, GLA_VAL), BF16),
        pltpu.VMEM((n_seq, GLA_HEADS, GLA_DV, GLA_DK), F32),
    ]
    return pl.pallas_call(
        kern, grid=grid, in_specs=in_specs, out_specs=out_specs, out_shape=out_shape,
        scratch_shapes=scratch,
        compiler_params=pltpu.CompilerParams(
            dimension_semantics=("arbitrary", "arbitrary"), vmem_limit_bytes=VMEM_LIMIT),
        name=name,
    )(x, mod, s0, cprev, n1g, win, walpha, balpha, gng, wgla, convw, convb, wconv, wo)


FF_GROUP = 256


def _ffn_kernel(x_ref, mod_ref, n2g_ref, win_ref, wout_ref, nfg_ref, y_ref, hb_s, act_s,
                *, n_seq, tbl, sub):
    sh2 = mod_ref[:, 3:4, :]
    sc2 = mod_ref[:, 4:5, :]
    g2 = mod_ref[:, 5:6, :]
    for t0 in range(0, tbl, sub):
        rows = n_seq * sub
        r0 = (t0 // sub) * rows
        x3 = x_ref[:, t0:t0 + sub, :]
        ms = jnp.mean(x3 * x3, axis=-1, keepdims=True)
        h3 = x3 * lax.rsqrt(ms + EPS) * n2g_ref[...] * (1.0 + sc2) + sh2
        hb_s[r0:r0 + rows, :] = h3.reshape(rows, D_MODEL).astype(BF16)
        hb = hb_s[r0:r0 + rows, :]
        for c in range(D_FF // FF_GROUP):
            gt = _dot(hb, win_ref[:, c * FF_GROUP:(c + 1) * FF_GROUP])
            up = _dot(hb, win_ref[:, D_FF + c * FF_GROUP:D_FF + (c + 1) * FF_GROUP])
            act_s[r0:r0 + rows, c * FF_GROUP:(c + 1) * FF_GROUP] = (
                gt * _sigmoid(gt) * up).astype(BF16)
        f = _dot(act_s[r0:r0 + rows, :], wout_ref[...])
        x2 = x3 + g2 * f.reshape(n_seq, sub, D_MODEL)
        ms2 = jnp.mean(x2 * x2, axis=-1, keepdims=True)
        y_ref[:, t0:t0 + sub, :] = x2 * lax.rsqrt(ms2 + EPS) * nfg_ref[...]


def _ffn(x, mod, n2g, win, wout, nfg, *, n_seq, tbl, sub, name):
    n_streams, length, _ = x.shape
    assert n_streams % n_seq == 0 and length % tbl == 0
    grid = (n_streams // n_seq, length // tbl)
    rows = n_seq * tbl
    kern = functools.partial(_ffn_kernel, n_seq=n_seq, tbl=tbl, sub=sub)
    return pl.pallas_call(
        kern, grid=grid,
        in_specs=[
            pl.BlockSpec((n_seq, tbl, D_MODEL), lambda i, j: (i, j, 0)),
            pl.BlockSpec((n_seq, N_MOD, D_MODEL), lambda i, j: (i, 0, 0)),
            _const_spec(n2g.shape), _const_spec(win.shape), _const_spec(wout.shape),
            _const_spec(nfg.shape),
        ],
        out_specs=pl.BlockSpec((n_seq, tbl, D_MODEL), lambda i, j: (i, j, 0)),
        out_shape=jax.ShapeDtypeStruct(x.shape, F32),
        scratch_shapes=[pltpu.VMEM((rows, D_MODEL), BF16), pltpu.VMEM((rows, D_FF), BF16)],
        compiler_params=pltpu.CompilerParams(
            dimension_semantics=("arbitrary", "arbitrary"), vmem_limit_bytes=VMEM_LIMIT),
        name=name,
    )(x, mod, n2g, win, wout, nfg)


def kernel(x_prompt, x_sample, c_prompt, c_sample, state_gla, cache_conv, w_mod, b_mod, norm1_g,
           w_in, w_alpha, b_alpha, gla_norm_g, w_gla_out, conv_w, conv_b, w_conv_out, w_o,
           norm2_g, w_ffn_in, w_ffn_out, norm_f_g):
    bp = x_prompt.shape[0]
    bs = x_sample.shape[0]
    depth = w_mod.shape[0]
    assert depth == 1
    l = 0
    c_all = jnp.concatenate([c_prompt, c_sample], axis=0)
    n_c = c_all.shape[0]
    n_pad = -n_c % SUBLANES
    c_all = jnp.concatenate([c_all, jnp.zeros((n_pad, D_MODEL), F32)], axis=0)
    mod = _modulation(c_all, w_mod[l], b_mod[l]).reshape(n_c + n_pad, N_MOD, D_MODEL)
    mod_p, mod_s = mod[:bp], mod[bp:bp + bs]

    assert w_in.shape[1:] == (D_MODEL, IN_DIM)
    win_t = jnp.swapaxes(w_in[l], 0, 1).astype(BF16)
    walpha = w_alpha[l].astype(BF16)
    mix_w = (norm1_g[l].reshape(1, -1), win_t, walpha, b_alpha[l].reshape(1, -1),
             gla_norm_g[l].reshape(1, -1), w_gla_out[l].astype(BF16), conv_w[l],
             conv_b[l].reshape(1, -1), w_conv_out[l].astype(BF16), w_o[l].astype(BF16))
    ffn_w = (norm2_g[l].reshape(1, -1), w_ffn_in[l].astype(BF16), w_ffn_out[l].astype(BF16),
             norm_f_g.reshape(1, -1))

    s0_p = jnp.zeros((bp, GLA_HEADS, GLA_DK, GLA_DV), F32)
    c0_p = jnp.zeros((bp, CONV_W - 1, CONV_DIM), F32)
    x1_p, st_p, cv_p = _mixer(x_prompt, mod_p, s0_p, c0_p, mix_w,
                              n_seq=1, tbl=512, chunk=256, name="mixer_prompt")
    x1_s, st_s, cv_s = _mixer(x_sample, mod_s, state_gla[l], cache_conv[l], mix_w,
                              n_seq=8, tbl=x_sample.shape[1], chunk=x_sample.shape[1],
                              name="mixer_sample")
    y_p = _ffn(x1_p, mod_p, *ffn_w, n_seq=1, tbl=1024, sub=256, name="ffn_prompt")
    y_s = _ffn(x1_s, mod_s, *ffn_w, n_seq=bs, tbl=x_sample.shape[1], sub=x_sample.shape[1],
               name="ffn_sample")
    return (y_p, y_s, st_p[None], cv_p[None], st_s[None], cv_s[None])
```

```python
import functools

import jax
import jax.numpy as jnp
from jax import lax
from jax.experimental import pallas as pl
from jax.experimental.pallas import tpu as pltpu

F32 = jnp.float32
BF16 = jnp.bfloat16

D_MODEL = 1024
GLA_HEADS = 4
GLA_DK = 128
GLA_DV = 256
GLA_KEY = GLA_HEADS * GLA_DK
GLA_VAL = GLA_HEADS * GLA_DV
GLA_RANK = 16
GLA_TAU = 16.0
CONV_DIM = D_MODEL
CONV_W = 3
D_FF = 2816
N_MOD = 6
EPS = 1e-6

COL_GROUP = 256
LANES = 128
SUBLANES = 8

OFF_Q = 0
OFF_K = OFF_Q + GLA_KEY
OFF_V = OFF_K + GLA_KEY
OFF_G = OFF_V + GLA_VAL
OFF_A = OFF_G + GLA_VAL
OFF_CB = OFF_A + GLA_RANK
OFF_CC = OFF_CB + CONV_DIM
OFF_CH = OFF_CC + CONV_DIM
OFF_GA = OFF_CH + CONV_DIM
OFF_GB = OFF_GA + D_MODEL
IN_DIM = OFF_GB + D_MODEL

SAFE_LOG_DECAY = 60.0

VMEM_LIMIT = 60 * 1024 * 1024


def _dot(a, b):
    return jnp.dot(a, b, preferred_element_type=F32)


def _dot_nt(a, b):
    return lax.dot_general(a, b, (((1,), (1,)), ((), ())), preferred_element_type=F32)


def _dot_tn(a, b):
    return lax.dot_general(a, b, (((0,), (0,)), ((), ())), preferred_element_type=F32)


def _sigmoid(x):
    return 1.0 / (1.0 + jnp.exp(-x))


def _mod_kernel(c_ref, w_ref, b_ref, o_ref):
    o_ref[...] = _dot(c_ref[...].astype(BF16), w_ref[...].astype(BF16)) + b_ref[...]


def _modulation(c_all, w_mod, b_mod):
    n = c_all.shape[0]
    nblk = N_MOD
    return pl.pallas_call(
        _mod_kernel,
        grid=(nblk,),
        in_specs=[
            pl.BlockSpec((n, D_MODEL), lambda j: (0, 0)),
            pl.BlockSpec((D_MODEL, D_MODEL), lambda j: (0, j)),
            pl.BlockSpec((1, D_MODEL), lambda j: (0, j)),
        ],
        out_specs=pl.BlockSpec((n, D_MODEL), lambda j: (0, j)),
        out_shape=jax.ShapeDtypeStruct((n, N_MOD * D_MODEL), F32),
        name="adaln_mod",
    )(c_all, w_mod, b_mod.reshape(1, -1))


def _mixer_kernel(*refs, n_seq, tbl, chunk, unroll, n_cast):
    (x_ref, mod_ref, s0_ref, cprev_ref, n1g_ref, win_ref, walpha_ref, balpha_ref,
     gng_ref, wgla_ref, convw_ref, convb_ref, wconv_ref, wo_ref) = refs[:14]
    cast_in = refs[14:14 + n_cast]
    x1_ref, sout_ref, cout_ref = refs[14 + n_cast:17 + n_cast]
    cast_out = refs[17 + n_cast:17 + 2 * n_cast]
    (hb_s, q_s, k_s, v_s, sg_s, b_s, ubuf_s, ub_s, yb_s, o_s, og_s,
     st_s) = refs[17 + 2 * n_cast:]
    rows = n_seq * tbl
    nchunk = tbl // chunk
    j = pl.program_id(1)
    nj = pl.num_programs(1)

    @pl.when(j == 0)
    def _init():
        for s in range(n_seq):
            for h in range(GLA_HEADS):
                st_s[s, h] = s0_ref[s, h].T
        ubuf_s[:, SUBLANES - 2:SUBLANES, :] = cprev_ref[...]

    x3 = x_ref[...]
    sh1 = mod_ref[:, 0:1, :]
    sc1 = mod_ref[:, 1:2, :]
    g1 = mod_ref[:, 2:3, :]
    ms = jnp.mean(x3 * x3, axis=-1, keepdims=True)
    h3 = x3 * lax.rsqrt(ms + EPS) * n1g_ref[...] * (1.0 + sc1) + sh1
    hb_s[...] = h3.reshape(rows, D_MODEL).astype(BF16)

    def proj(off, c, n=COL_GROUP):
        return _dot_nt(hb_s[...], win_ref[off + c * n:off + (c + 1) * n, :])

    def cols(c):
        return slice(c * COL_GROUP, (c + 1) * COL_GROUP)

    za = proj(OFF_A, 0, GLA_RANK)
    xa = _dot(za.astype(BF16), walpha_ref[...]) + balpha_ref[...]
    for c in range(GLA_KEY // COL_GROUP):
        q_s[:, cols(c)] = proj(OFF_Q, c) * (GLA_DK ** -0.5)
    la = (jnp.minimum(xa, 0.0) - jnp.log(1.0 + jnp.exp(-jnp.abs(xa)))) * (1.0 / GLA_TAU)
    for c in range(GLA_KEY // COL_GROUP):
        k_s[:, cols(c)] = proj(OFF_K, c)

    ri = lax.broadcasted_iota(jnp.int32, (chunk, chunk), 0)
    ci = lax.broadcasted_iota(jnp.int32, (chunk, chunk), 1)
    tril = ri >= ci
    tmat = tril.astype(BF16)
    bmin = jnp.zeros((1, GLA_KEY), F32)
    for idx in range(n_seq * nchunk):
        la_c = la[idx * chunk:(idx + 1) * chunk, :]
        hi = la_c.astype(BF16)
        lo = (la_c - hi.astype(F32)).astype(BF16)
        b_c = _dot(tmat, hi) + _dot(tmat, lo)
        b_s[idx * chunk:(idx + 1) * chunk, :] = b_c
        bmin = jnp.minimum(bmin, b_c[chunk - 1:chunk, :])
    safe = jnp.min(bmin) > -SAFE_LOG_DECAY

    for c in range(GLA_VAL // COL_GROUP):
        v_s[:, cols(c)] = proj(OFF_V, c).astype(BF16)
        g = proj(OFF_G, c)
        sg_s[:, cols(c)] = (g * _sigmoid(g)).astype(BF16)

    gng = gng_ref[...]
    n_idx = n_seq * nchunk

    def loop(body):
        if unroll:
            for idx in range(n_idx):
                body(idx * chunk, idx // nchunk)
        else:
            def fbody(idx, carry):
                body(pl.multiple_of(idx * chunk, chunk), idx // nchunk)
                return carry
            lax.fori_loop(0, n_idx, fbody, 0)

    def intra_fast(r0, s):
        q = q_s[pl.ds(r0, chunk), :]
        k = k_s[pl.ds(r0, chunk), :]
        b = b_s[pl.ds(r0, chunk), :]
        v = v_s[pl.ds(r0, chunk), :]
        bmid = b[chunk // 2 - 1:chunk // 2, :]
        qs = (q * jnp.exp(b - bmid)).astype(BF16)
        ks = (k * jnp.exp(bmid - b)).astype(BF16)
        for h in range(GLA_HEADS):
            ksl = slice(h * GLA_DK, (h + 1) * GLA_DK)
            vsl = slice(h * GLA_DV, (h + 1) * GLA_DV)
            a = jnp.where(tril, _dot_nt(qs[:, ksl], ks[:, ksl]), 0.0)
            o_s[pl.ds(r0, chunk), vsl] = _dot(a.astype(BF16), v[:, vsl])

    def conv_group(c):
        u = proj(OFF_CC, c) * proj(OFF_CH, c)
        ubuf_s[:, SUBLANES:, cols(c)] = u.reshape(n_seq, tbl, COL_GROUP)
        conv = (ubuf_s[:, SUBLANES - 2:SUBLANES - 2 + tbl, cols(c)] * convw_ref[0:1, cols(c)]
                + ubuf_s[:, SUBLANES - 1:SUBLANES - 1 + tbl, cols(c)] * convw_ref[1:2, cols(c)]
                + u.reshape(n_seq, tbl, COL_GROUP) * convw_ref[2:3, cols(c)]
                + convb_ref[:, cols(c)])
        cb = proj(OFF_CB, c)
        ub_s[:, cols(c)] = (cb.reshape(n_seq, tbl, COL_GROUP) * conv).reshape(
            rows, COL_GROUP).astype(BF16)

    loop(intra_fast)

    @pl.when(jnp.logical_not(safe))
    def _pairwise():
        hr = lax.broadcasted_iota(jnp.int32, (GLA_KEY, GLA_VAL), 0) // GLA_DK
        hc = lax.broadcasted_iota(jnp.int32, (GLA_KEY, GLA_VAL), 1) // GLA_DV
        headsel = (hr == hc).astype(BF16)
        rowid = lax.broadcasted_iota(jnp.int32, (chunk, 1), 0)

        def cbody(idx, carry):
            r0 = pl.multiple_of(idx * chunk, chunk)
            q = q_s[pl.ds(r0, chunk), :]
            k = k_s[pl.ds(r0, chunk), :]
            b = b_s[pl.ds(r0, chunk), :]
            vf = v_s[pl.ds(r0, chunk), :].astype(F32)

            def jbody(jj, acc):
                pick = rowid == jj
                kj = jnp.sum(jnp.where(pick, k, 0.0), axis=0, keepdims=True)
                bj = jnp.sum(jnp.where(pick, b, 0.0), axis=0, keepdims=True)
                vj = jnp.sum(jnp.where(pick, vf, 0.0), axis=0, keepdims=True)
                dm = q * kj * jnp.exp(jnp.minimum(b - bj, 0.0))
                dm = jnp.where(rowid >= jj, dm, 0.0)
                return acc + _dot(dm.astype(BF16), headsel) * vj

            o_s[pl.ds(r0, chunk), :] = lax.fori_loop(
                0, chunk, jbody, jnp.zeros((chunk, GLA_VAL), F32))
            return carry

        lax.fori_loop(0, n_idx, cbody, 0)

    def state_step(r0, s):
        q = q_s[pl.ds(r0, chunk), :]
        k = k_s[pl.ds(r0, chunk), :]
        b = b_s[pl.ds(r0, chunk), :]
        v = v_s[pl.ds(r0, chunk), :]
        blast = b[chunk - 1:chunk, :]
        qb = (q * jnp.exp(b)).astype(BF16)
        kb = (k * jnp.exp(blast - b)).astype(BF16)
        dec = jnp.exp(blast)
        for h in range(GLA_HEADS):
            ksl = slice(h * GLA_DK, (h + 1) * GLA_DK)
            vsl = slice(h * GLA_DV, (h + 1) * GLA_DV)
            st = st_s[s, h]
            o_h = o_s[pl.ds(r0, chunk), vsl] + _dot_nt(qb[:, ksl], st.astype(BF16))
            st_s[s, h] = dec[:, ksl] * st + _dot_tn(v[:, vsl], kb[:, ksl])
            o_h = o_h * lax.rsqrt(jnp.mean(o_h * o_h, axis=-1, keepdims=True) + EPS) * gng
            og_s[pl.ds(r0, chunk), vsl] = (
                o_h * sg_s[pl.ds(r0, chunk), vsl].astype(F32)).astype(BF16)

    loop(state_step)

    for c in range(CONV_DIM // COL_GROUP):
        conv_group(c)
    tail = ubuf_s[:, tbl + SUBLANES - 2:tbl + SUBLANES, :]
    ubuf_s[:, SUBLANES - 2:SUBLANES, :] = tail
    cout_ref[...] = tail
    for c in range(D_MODEL // COL_GROUP):
        yb = _dot(ub_s[...], wconv_ref[:, cols(c)])
        yb_s[:, cols(c)] = (_sigmoid(proj(OFF_GB, c)) * yb).astype(BF16)

    for src, dst in zip(cast_in, cast_out):
        dst[...] = src[...].astype(BF16)

    for c in range(D_MODEL // COL_GROUP):
        ya = _dot(og_s[...], wgla_ref[:, cols(c)])
        merged = _sigmoid(proj(OFF_GA, c)) * ya + yb_s[:, cols(c)].astype(F32)
        ub_s[:, cols(c)] = merged.astype(BF16)
    for c in range(D_MODEL // COL_GROUP):
        m = _dot(ub_s[...], wo_ref[:, cols(c)])
        x1_ref[:, :, cols(c)] = (x_ref[:, :, cols(c)]
                                 + g1[:, :, cols(c)] * m.reshape(n_seq, tbl, COL_GROUP))

    @pl.when(j == nj - 1)
    def _final_state():
        for s in range(n_seq):
            for h in range(GLA_HEADS):
                sout_ref[s, h] = st_s[s, h].T


def _const_spec(shape):
    nd = len(shape)
    return pl.BlockSpec(shape, lambda i, j: (0,) * nd, pipeline_mode=pl.Buffered(1))


BF16_SUBLANES = 16


def _cast_blocking(n_rows, n_steps):
    for n_blocks in range(n_steps, 0, -1):
        if (n_steps % n_blocks == 0 and n_rows % n_blocks == 0
                and (n_rows // n_blocks) % BF16_SUBLANES == 0):
            return n_rows // n_blocks, n_steps // n_blocks
    raise ValueError((n_rows, n_steps))


def _mixer(x, mod, s0, cprev, weights, *, n_seq, tbl, chunk, name, cast=()):
    n_streams, length, _ = x.shape
    assert n_streams % n_seq == 0 and length % tbl == 0
    grid = (n_streams // n_seq, length // tbl)
    rows = n_seq * tbl
    (n1g, win, walpha, balpha, gng, wgla, convw, convb, wconv, wo) = weights
    kern = functools.partial(_mixer_kernel, n_seq=n_seq, tbl=tbl, chunk=chunk,
                             unroll=n_seq == 1, n_cast=len(cast))
    cast_specs = []
    for w in cast:
        rb, steps = _cast_blocking(w.shape[0], grid[0] * grid[1])
        cast_specs.append(pl.BlockSpec(
            (rb, w.shape[1]), lambda i, j, steps=steps: ((i * grid[1] + j) // steps, 0)))
    in_specs = [
        pl.BlockSpec((n_seq, tbl, D_MODEL), lambda i, j: (i, j, 0)),
        pl.BlockSpec((n_seq, N_MOD, D_MODEL), lambda i, j: (i, 0, 0)),
        pl.BlockSpec((n_seq, GLA_HEADS, GLA_DK, GLA_DV), lambda i, j: (i, 0, 0, 0)),
        pl.BlockSpec((n_seq, CONV_W - 1, CONV_DIM), lambda i, j: (i, 0, 0)),
        _const_spec(n1g.shape), _const_spec(win.shape), _const_spec(walpha.shape),
        _const_spec(balpha.shape), _const_spec(gng.shape), _const_spec(wgla.shape),
        _const_spec(convw.shape), _const_spec(convb.shape), _const_spec(wconv.shape),
        _const_spec(wo.shape),
    ] + cast_specs
    out_specs = [
        pl.BlockSpec((n_seq, tbl, D_MODEL), lambda i, j: (i, j, 0)),
        pl.BlockSpec((n_seq, GLA_HEADS, GLA_DK, GLA_DV), lambda i, j: (i, 0, 0, 0)),
        pl.BlockSpec((n_seq, CONV_W - 1, CONV_DIM), lambda i, j: (i, 0, 0)),
    ] + cast_specs
    out_shape = [
        jax.ShapeDtypeStruct(x.shape, F32),
        jax.ShapeDtypeStruct((n_streams, GLA_HEADS, GLA_DK, GLA_DV), F32),
        jax.ShapeDtypeStruct((n_streams, CONV_W - 1, CONV_DIM), F32),
    ] + [jax.ShapeDtypeStruct(w.shape, BF16) for w in cast]
    scratch = [
        pltpu.VMEM((rows, D_MODEL), BF16),
        pltpu.VMEM((rows, GLA_KEY), F32),
        pltpu.VMEM((rows, GLA_KEY), F32),
        pltpu.VMEM((rows, GLA_VAL), BF16),
        pltpu.VMEM((rows, GLA_VAL), BF16),
        pltpu.VMEM((rows, GLA_KEY), F32),
        pltpu.VMEM((n_seq, tbl + SUBLANES, CONV_DIM), F32),
        pltpu.VMEM((rows, D_MODEL), BF16),
        pltpu.VMEM((rows, D_MODEL), BF16),
        pltpu.VMEM((rows, GLA_VAL), F32),
        pltpu.VMEM((rows, GLA_VAL), BF16),
        pltpu.VMEM((n_seq, GLA_HEADS, GLA_DV, GLA_DK), F32),
    ]
    return pl.pallas_call(
        kern, grid=grid, in_specs=in_specs, out_specs=out_specs, out_shape=out_shape,
        scratch_shapes=scratch,
        compiler_params=pltpu.CompilerParams(
            dimension_semantics=("arbitrary", "arbitrary"), vmem_limit_bytes=VMEM_LIMIT),
        name=name,
    )(x, mod, s0, cprev, n1g, win, walpha, balpha, gng, wgla, convw, convb, wconv, wo, *cast)


FF_GROUP = 256


def _ffn_kernel(x_ref, mod_ref, n2g_ref, win_ref, wout_ref, nfg_ref, y_ref, hb_s, act_s,
                *, n_seq, tbl, sub):
    sh2 = mod_ref[:, 3:4, :]
    sc2 = mod_ref[:, 4:5, :]
    g2 = mod_ref[:, 5:6, :]
    for t0 in range(0, tbl, sub):
        rows = n_seq * sub
        r0 = (t0 // sub) * rows
        x3 = x_ref[:, t0:t0 + sub, :]
        ms = jnp.mean(x3 * x3, axis=-1, keepdims=True)
        h3 = x3 * lax.rsqrt(ms + EPS) * n2g_ref[...] * (1.0 + sc2) + sh2
        hb_s[r0:r0 + rows, :] = h3.reshape(rows, D_MODEL).astype(BF16)
        hb = hb_s[r0:r0 + rows, :]
        for c in range(D_FF // FF_GROUP):
            gt = _dot(hb, win_ref[:, c * FF_GROUP:(c + 1) * FF_GROUP])
            up = _dot(hb, win_ref[:, D_FF + c * FF_GROUP:D_FF + (c + 1) * FF_GROUP])
            act_s[r0:r0 + rows, c * FF_GROUP:(c + 1) * FF_GROUP] = (
                gt * _sigmoid(gt) * up).astype(BF16)
        f = _dot(act_s[r0:r0 + rows, :], wout_ref[...])
        x2 = x3 + g2 * f.reshape(n_seq, sub, D_MODEL)
        ms2 = jnp.mean(x2 * x2, axis=-1, keepdims=True)
        y_ref[:, t0:t0 + sub, :] = x2 * lax.rsqrt(ms2 + EPS) * nfg_ref[...]


def _ffn(x, mod, n2g, win, wout, nfg, *, n_seq, tbl, sub, name):
    n_streams, length, _ = x.shape
    assert n_streams % n_seq == 0 and length % tbl == 0
    grid = (n_streams // n_seq, length // tbl)
    rows = n_seq * tbl
    kern = functools.partial(_ffn_kernel, n_seq=n_seq, tbl=tbl, sub=sub)
    return pl.pallas_call(
        kern, grid=grid,
        in_specs=[
            pl.BlockSpec((n_seq, tbl, D_MODEL), lambda i, j: (i, j, 0)),
            pl.BlockSpec((n_seq, N_MOD, D_MODEL), lambda i, j: (i, 0, 0)),
            _const_spec(n2g.shape), _const_spec(win.shape), _const_spec(wout.shape),
            _const_spec(nfg.shape),
        ],
        out_specs=pl.BlockSpec((n_seq, tbl, D_MODEL), lambda i, j: (i, j, 0)),
        out_shape=jax.ShapeDtypeStruct(x.shape, F32),
        scratch_shapes=[pltpu.VMEM((rows, D_MODEL), BF16), pltpu.VMEM((rows, D_FF), BF16)],
        compiler_params=pltpu.CompilerParams(
            dimension_semantics=("arbitrary", "arbitrary"), vmem_limit_bytes=VMEM_LIMIT),
        name=name,
    )(x, mod, n2g, win, wout, nfg)


def kernel(x_prompt, x_sample, c_prompt, c_sample, state_gla, cache_conv, w_mod, b_mod, norm1_g,
           w_in, w_alpha, b_alpha, gla_norm_g, w_gla_out, conv_w, conv_b, w_conv_out, w_o,
           norm2_g, w_ffn_in, w_ffn_out, norm_f_g):
    bp = x_prompt.shape[0]
    bs = x_sample.shape[0]
    depth = w_mod.shape[0]
    assert depth == 1
    l = 0
    c_all = jnp.concatenate([c_prompt, c_sample], axis=0)
    n_c = c_all.shape[0]
    n_pad = -n_c % SUBLANES
    c_all = jnp.concatenate([c_all, jnp.zeros((n_pad, D_MODEL), F32)], axis=0)
    mod = _modulation(c_all, w_mod[l], b_mod[l]).reshape(n_c + n_pad, N_MOD, D_MODEL)
    mod_p, mod_s = mod[:bp], mod[bp:bp + bs]

    assert w_in.shape[1:] == (D_MODEL, IN_DIM)
    win_t = jnp.swapaxes(w_in[l], 0, 1).astype(BF16)
    walpha = w_alpha[l].astype(BF16)
    mix_w = (norm1_g[l].reshape(1, -1), win_t, walpha, b_alpha[l].reshape(1, -1),
             gla_norm_g[l].reshape(1, -1), w_gla_out[l].astype(BF16), conv_w[l],
             conv_b[l].reshape(1, -1), w_conv_out[l].astype(BF16), w_o[l].astype(BF16))

    s0_p = jnp.zeros((bp, GLA_HEADS, GLA_DK, GLA_DV), F32)
    c0_p = jnp.zeros((bp, CONV_W - 1, CONV_DIM), F32)
    x1_p, st_p, cv_p, wffn_in, wffn_out = _mixer(
        x_prompt, mod_p, s0_p, c0_p, mix_w, n_seq=1, tbl=512, chunk=256, name="mixer_prompt",
        cast=(w_ffn_in[l], w_ffn_out[l]))
    ffn_w = (norm2_g[l].reshape(1, -1), wffn_in, wffn_out, norm_f_g.reshape(1, -1))
    x1_s, st_s, cv_s = _mixer(x_sample, mod_s, state_gla[l], cache_conv[l], mix_w,
                              n_seq=8, tbl=x_sample.shape[1], chunk=x_sample.shape[1],
                              name="mixer_sample")
    y_p = _ffn(x1_p, mod_p, *ffn_w, n_seq=1, tbl=1024, sub=256, name="ffn_prompt")
    y_s = _ffn(x1_s, mod_s, *ffn_w, n_seq=bs, tbl=x_sample.shape[1], sub=x_sample.shape[1],
               name="ffn_sample")
    return (y_p, y_s, st_p[None], cv_p[None], st_s[None], cv_s[None])
```

```python
import functools

import jax
import jax.numpy as jnp
from jax import lax
from jax.experimental import pallas as pl
from jax.experimental.pallas import tpu as pltpu

F32 = jnp.float32
BF16 = jnp.bfloat16

D_MODEL = 1024
GLA_HEADS = 4
GLA_DK = 128
GLA_DV = 256
GLA_KEY = GLA_HEADS * GLA_DK
GLA_VAL = GLA_HEADS * GLA_DV
GLA_RANK = 16
GLA_TAU = 16.0
CONV_DIM = D_MODEL
CONV_W = 3
D_FF = 2816
N_MOD = 6
EPS = 1e-6

COL_GROUP = 256
LANES = 128
SUBLANES = 8

OFF_Q = 0
OFF_K = OFF_Q + GLA_KEY
OFF_V = OFF_K + GLA_KEY
OFF_G = OFF_V + GLA_VAL
OFF_A = OFF_G + GLA_VAL
OFF_CB = OFF_A + GLA_RANK
OFF_CC = OFF_CB + CONV_DIM
OFF_CH = OFF_CC + CONV_DIM
OFF_GA = OFF_CH + CONV_DIM
OFF_GB = OFF_GA + D_MODEL
IN_DIM = OFF_GB + D_MODEL

SAFE_LOG_DECAY = 60.0

VMEM_LIMIT = 60 * 1024 * 1024


def _dot(a, b):
    return jnp.dot(a, b, preferred_element_type=F32)


def _dot_nt(a, b):
    return lax.dot_general(a, b, (((1,), (1,)), ((), ())), preferred_element_type=F32)


def _dot_tn(a, b):
    return lax.dot_general(a, b, (((0,), (0,)), ((), ())), preferred_element_type=F32)


def _sigmoid(x):
    return 1.0 / (1.0 + jnp.exp(-x))


def _mod_kernel(c_ref, w_ref, b_ref, o_ref):
    o_ref[...] = _dot(c_ref[...].astype(BF16), w_ref[...].astype(BF16)) + b_ref[...]


def _modulation(c_all, w_mod, b_mod):
    n = c_all.shape[0]
    nblk = N_MOD
    return pl.pallas_call(
        _mod_kernel,
        grid=(nblk,),
        in_specs=[
            pl.BlockSpec((n, D_MODEL), lambda j: (0, 0)),
            pl.BlockSpec((D_MODEL, D_MODEL), lambda j: (0, j)),
            pl.BlockSpec((1, D_MODEL), lambda j: (0, j)),
        ],
        out_specs=pl.BlockSpec((n, D_MODEL), lambda j: (0, j)),
        out_shape=jax.ShapeDtypeStruct((n, N_MOD * D_MODEL), F32),
        name="adaln_mod",
    )(c_all, w_mod, b_mod.reshape(1, -1))


def _mixer_kernel(*refs, n_seq, tbl, chunk, unroll, n_cast):
    (x_ref, mod_ref, s0_ref, cprev_ref, n1g_ref, win_ref, walpha_ref, balpha_ref,
     gng_ref, wgla_ref, convw_ref, convb_ref, wconv_ref, wo_ref) = refs[:14]
    cast_in = refs[14:14 + n_cast]
    x1_ref, sout_ref, cout_ref = refs[14 + n_cast:17 + n_cast]
    cast_out = refs[17 + n_cast:17 + 2 * n_cast]
    (hb_s, q_s, k_s, v_s, sg_s, b_s, ubuf_s, ub_s, yb_s, o_s, og_s,
     st_s) = refs[17 + 2 * n_cast:]
    rows = n_seq * tbl
    nchunk = tbl // chunk
    j = pl.program_id(1)
    nj = pl.num_programs(1)

    @pl.when(j == 0)
    def _init():
        for s in range(n_seq):
            for h in range(GLA_HEADS):
                st_s[s, h] = s0_ref[s, h].T
        ubuf_s[:, SUBLANES - 2:SUBLANES, :] = cprev_ref[...]

    x3 = x_ref[...]
    sh1 = mod_ref[:, 0:1, :]
    sc1 = mod_ref[:, 1:2, :]
    g1 = mod_ref[:, 2:3, :]
    ms = jnp.mean(x3 * x3, axis=-1, keepdims=True)
    h3 = x3 * lax.rsqrt(ms + EPS) * n1g_ref[...] * (1.0 + sc1) + sh1
    hb_s[...] = h3.reshape(rows, D_MODEL).astype(BF16)

    def proj(off, c, n=COL_GROUP):
        return _dot_nt(hb_s[...], win_ref[off + c * n:off + (c + 1) * n, :])

    def cols(c):
        return slice(c * COL_GROUP, (c + 1) * COL_GROUP)

    za = proj(OFF_A, 0, GLA_RANK)
    xa = _dot(za.astype(BF16), walpha_ref[...]) + balpha_ref[...]
    for c in range(GLA_KEY // COL_GROUP):
        q_s[:, cols(c)] = proj(OFF_Q, c) * (GLA_DK ** -0.5)
    la = (jnp.minimum(xa, 0.0) - jnp.log(1.0 + jnp.exp(-jnp.abs(xa)))) * (1.0 / GLA_TAU)
    for c in range(GLA_KEY // COL_GROUP):
        k_s[:, cols(c)] = proj(OFF_K, c)

    ri = lax.broadcasted_iota(jnp.int32, (chunk, chunk), 0)
    ci = lax.broadcasted_iota(jnp.int32, (chunk, chunk), 1)
    tril = ri >= ci
    tmat = tril.astype(BF16)
    bmin = jnp.zeros((1, GLA_KEY), F32)
    for idx in range(n_seq * nchunk):
        la_c = la[idx * chunk:(idx + 1) * chunk, :]
        hi = la_c.astype(BF16)
        lo = (la_c - hi.astype(F32)).astype(BF16)
        b_c = _dot(tmat, hi) + _dot(tmat, lo)
        b_s[idx * chunk:(idx + 1) * chunk, :] = b_c
        bmin = jnp.minimum(bmin, b_c[chunk - 1:chunk, :])
    safe = jnp.min(bmin) > -SAFE_LOG_DECAY

    for c in range(GLA_VAL // COL_GROUP):
        v_s[:, cols(c)] = proj(OFF_V, c).astype(BF16)
        g = proj(OFF_G, c)
        sg_s[:, cols(c)] = (g * _sigmoid(g)).astype(BF16)

    gng = gng_ref[...]
    n_idx = n_seq * nchunk

    def loop(body):
        if unroll:
            for idx in range(n_idx):
                body(idx * chunk, idx // nchunk)
        else:
            def fbody(idx, carry):
                body(pl.multiple_of(idx * chunk, chunk), idx // nchunk)
                return carry
            lax.fori_loop(0, n_idx, fbody, 0)

    def intra_fast(r0, s):
        q = q_s[pl.ds(r0, chunk), :]
        k = k_s[pl.ds(r0, chunk), :]
        b = b_s[pl.ds(r0, chunk), :]
        v = v_s[pl.ds(r0, chunk), :]
        bmid = b[chunk // 2 - 1:chunk // 2, :]
        qs = (q * jnp.exp(b - bmid)).astype(BF16)
        ks = (k * jnp.exp(bmid - b)).astype(BF16)
        for h in range(GLA_HEADS):
            ksl = slice(h * GLA_DK, (h + 1) * GLA_DK)
            vsl = slice(h * GLA_DV, (h + 1) * GLA_DV)
            a = jnp.where(tril, _dot_nt(qs[:, ksl], ks[:, ksl]), 0.0)
            o_s[pl.ds(r0, chunk), vsl] = _dot(a.astype(BF16), v[:, vsl])

    def conv_group(c):
        u = proj(OFF_CC, c) * proj(OFF_CH, c)
        ubuf_s[:, SUBLANES:, cols(c)] = u.reshape(n_seq, tbl, COL_GROUP)
        conv = (ubuf_s[:, SUBLANES - 2:SUBLANES - 2 + tbl, cols(c)] * convw_ref[0:1, cols(c)]
                + ubuf_s[:, SUBLANES - 1:SUBLANES - 1 + tbl, cols(c)] * convw_ref[1:2, cols(c)]
                + u.reshape(n_seq, tbl, COL_GROUP) * convw_ref[2:3, cols(c)]
                + convb_ref[:, cols(c)])
        cb = proj(OFF_CB, c)
        ub_s[:, cols(c)] = (cb.reshape(n_seq, tbl, COL_GROUP) * conv).reshape(
            rows, COL_GROUP).astype(BF16)

    loop(intra_fast)

    @pl.when(jnp.logical_not(safe))
    def _pairwise():
        hr = lax.broadcasted_iota(jnp.int32, (GLA_KEY, GLA_VAL), 0) // GLA_DK
        hc = lax.broadcasted_iota(jnp.int32, (GLA_KEY, GLA_VAL), 1) // GLA_DV
        headsel = (hr == hc).astype(BF16)
        rowid = lax.broadcasted_iota(jnp.int32, (chunk, 1), 0)

        def cbody(idx, carry):
            r0 = pl.multiple_of(idx * chunk, chunk)
            q = q_s[pl.ds(r0, chunk), :]
            k = k_s[pl.ds(r0, chunk), :]
            b = b_s[pl.ds(r0, chunk), :]
            vf = v_s[pl.ds(r0, chunk), :].astype(F32)

            def jbody(jj, acc):
                pick = rowid == jj
                kj = jnp.sum(jnp.where(pick, k, 0.0), axis=0, keepdims=True)
                bj = jnp.sum(jnp.where(pick, b, 0.0), axis=0, keepdims=True)
                vj = jnp.sum(jnp.where(pick, vf, 0.0), axis=0, keepdims=True)
                dm = q * kj * jnp.exp(jnp.minimum(b - bj, 0.0))
                dm = jnp.where(rowid >= jj, dm, 0.0)
                return acc + _dot(dm.astype(BF16), headsel) * vj

            o_s[pl.ds(r0, chunk), :] = lax.fori_loop(
                0, chunk, jbody, jnp.zeros((chunk, GLA_VAL), F32))
            return carry

        lax.fori_loop(0, n_idx, cbody, 0)

    def state_step(r0, s):
        q = q_s[pl.ds(r0, chunk), :]
        k = k_s[pl.ds(r0, chunk), :]
        b = b_s[pl.ds(r0, chunk), :]
        v = v_s[pl.ds(r0, chunk), :]
        blast = b[chunk - 1:chunk, :]
        qb = (q * jnp.exp(b)).astype(BF16)
        kb = (k * jnp.exp(blast - b)).astype(BF16)
        dec = jnp.exp(blast)
        for h in range(GLA_HEADS):
            ksl = slice(h * GLA_DK, (h + 1) * GLA_DK)
            vsl = slice(h * GLA_DV, (h + 1) * GLA_DV)
            st = st_s[s, h]
            o_h = o_s[pl.ds(r0, chunk), vsl] + _dot_nt(qb[:, ksl], st.astype(BF16))
            st_s[s, h] = dec[:, ksl] * st + _dot_tn(v[:, vsl], kb[:, ksl])
            o_h = o_h * lax.rsqrt(jnp.mean(o_h * o_h, axis=-1, keepdims=True) + EPS) * gng
            og_s[pl.ds(r0, chunk), vsl] = (
                o_h * sg_s[pl.ds(r0, chunk), vsl].astype(F32)).astype(BF16)

    loop(state_step)

    for c in range(CONV_DIM // COL_GROUP):
        conv_group(c)
    tail = ubuf_s[:, tbl + SUBLANES - 2:tbl + SUBLANES, :]
    ubuf_s[:, SUBLANES - 2:SUBLANES, :] = tail
    cout_ref[...] = tail
    for c in range(D_MODEL // COL_GROUP):
        yb = _dot(ub_s[...], wconv_ref[:, cols(c)])
        yb_s[:, cols(c)] = (_sigmoid(proj(OFF_GB, c)) * yb).astype(BF16)

    for src, dst in zip(cast_in, cast_out):
        dst[...] = src[...].astype(BF16)

    for c in range(D_MODEL // COL_GROUP):
        ya = _dot(og_s[...], wgla_ref[:, cols(c)])
        merged = _sigmoid(proj(OFF_GA, c)) * ya + yb_s[:, cols(c)].astype(F32)
        ub_s[:, cols(c)] = merged.astype(BF16)
    for c in range(D_MODEL // COL_GROUP):
        m = _dot(ub_s[...], wo_ref[:, cols(c)])
        x1_ref[:, :, cols(c)] = (x_ref[:, :, cols(c)]
                                 + g1[:, :, cols(c)] * m.reshape(n_seq, tbl, COL_GROUP))

    @pl.when(j == nj - 1)
    def _final_state():
        for s in range(n_seq):
            for h in range(GLA_HEADS):
                sout_ref[s, h] = st_s[s, h].T


def _const_spec(shape):
    nd = len(shape)
    return pl.BlockSpec(shape, lambda i, j: (0,) * nd, pipeline_mode=pl.Buffered(1))


BF16_SUBLANES = 16


def _cast_blocking(n_rows, n_steps):
    for n_blocks in range(n_steps, 0, -1):
        if (n_steps % n_blocks == 0 and n_rows % n_blocks == 0
                and (n_rows // n_blocks) % BF16_SUBLANES == 0):
            return n_rows // n_blocks, n_steps // n_blocks
    raise ValueError((n_rows, n_steps))


def _mixer(x, mod, s0, cprev, weights, *, n_seq, tbl, chunk, name, cast=()):
    n_streams, length, _ = x.shape
    assert n_streams % n_seq == 0 and length % tbl == 0
    grid = (n_streams // n_seq, length // tbl)
    rows = n_seq * tbl
    (n1g, win, walpha, balpha, gng, wgla, convw, convb, wconv, wo) = weights
    kern = functools.partial(_mixer_kernel, n_seq=n_seq, tbl=tbl, chunk=chunk,
                             unroll=True, n_cast=len(cast))
    cast_specs = []
    for w in cast:
        rb, steps = _cast_blocking(w.shape[0], grid[0] * grid[1])
        cast_specs.append(pl.BlockSpec(
            (rb, w.shape[1]), lambda i, j, steps=steps: ((i * grid[1] + j) // steps, 0)))
    in_specs = [
        pl.BlockSpec((n_seq, tbl, D_MODEL), lambda i, j: (i, j, 0)),
        pl.BlockSpec((n_seq, N_MOD, D_MODEL), lambda i, j: (i, 0, 0)),
        pl.BlockSpec((n_seq, GLA_HEADS, GLA_DK, GLA_DV), lambda i, j: (i, 0, 0, 0)),
        pl.BlockSpec((n_seq, CONV_W - 1, CONV_DIM), lambda i, j: (i, 0, 0)),
        _const_spec(n1g.shape), _const_spec(win.shape), _const_spec(walpha.shape),
        _const_spec(balpha.shape), _const_spec(gng.shape), _const_spec(wgla.shape),
        _const_spec(convw.shape), _const_spec(convb.shape), _const_spec(wconv.shape),
        _const_spec(wo.shape),
    ] + cast_specs
    out_specs = [
        pl.BlockSpec((n_seq, tbl, D_MODEL), lambda i, j: (i, j, 0)),
        pl.BlockSpec((n_seq, GLA_HEADS, GLA_DK, GLA_DV), lambda i, j: (i, 0, 0, 0)),
        pl.BlockSpec((n_seq, CONV_W - 1, CONV_DIM), lambda i, j: (i, 0, 0)),
    ] + cast_specs
    out_shape = [
        jax.ShapeDtypeStruct(x.shape, F32),
        jax.ShapeDtypeStruct((n_streams, GLA_HEADS, GLA_DK, GLA_DV), F32),
        jax.ShapeDtypeStruct((n_streams, CONV_W - 1, CONV_DIM), F32),
    ] + [jax.ShapeDtypeStruct(w.shape, BF16) for w in cast]
    scratch = [
        pltpu.VMEM((rows, D_MODEL), BF16),
        pltpu.VMEM((rows, GLA_KEY), F32),
        pltpu.VMEM((rows, GLA_KEY), F32),
        pltpu.VMEM((rows, GLA_VAL), BF16),
        pltpu.VMEM((rows, GLA_VAL), BF16),
        pltpu.VMEM((rows, GLA_KEY), F32),
        pltpu.VMEM((n_seq, tbl + SUBLANES, CONV_DIM), F32),
        pltpu.VMEM((rows, D_MODEL), BF16),
        pltpu.VMEM((rows, D_MODEL), BF16),
        pltpu.VMEM((rows, GLA_VAL), F32),
        pltpu.VMEM((rows, GLA_VAL), BF16),
        pltpu.VMEM((n_seq, GLA_HEADS, GLA_DV, GLA_DK), F32),
    ]
    return pl.pallas_call(
        kern, grid=grid, in_specs=in_specs, out_specs=out_specs, out_shape=out_shape,
        scratch_shapes=scratch,
        compiler_params=pltpu.CompilerParams(
            dimension_semantics=("arbitrary", "arbitrary"), vmem_limit_bytes=VMEM_LIMIT),
        name=name,
    )(x, mod, s0, cprev, n1g, win, walpha, balpha, gng, wgla, convw, convb, wconv, wo, *cast)


FF_GROUP = 256


def _ffn_kernel(xp_ref, modp_ref, xs_ref, mods_ref, n2g_ref, win_ref, wout_ref, nfg_ref,
                yp_ref, ys_ref, hb_s, act_s, *, n_prompt_steps, tbl, sub):
    t = pl.program_id(0)
    weights = (n2g_ref, win_ref, wout_ref, nfg_ref, hb_s, act_s)

    @pl.when(t < n_prompt_steps)
    def _prompt():
        _ffn_block(xp_ref, modp_ref, yp_ref, *weights, n_seq=1, tbl=tbl, sub=sub)

    @pl.when(t == n_prompt_steps)
    def _sample():
        n_seq, length, _ = xs_ref.shape
        _ffn_block(xs_ref, mods_ref, ys_ref, *weights, n_seq=n_seq, tbl=length, sub=length)


def _ffn_block(x_ref, mod_ref, y_ref, n2g_ref, win_ref, wout_ref, nfg_ref, hb_s, act_s,
               *, n_seq, tbl, sub):
    sh2 = mod_ref[:, 3:4, :]
    sc2 = mod_ref[:, 4:5, :]
    g2 = mod_ref[:, 5:6, :]
    for t0 in range(0, tbl, sub):
        rows = n_seq * sub
        r0 = (t0 // sub) * rows
        x3 = x_ref[:, t0:t0 + sub, :]
        ms = jnp.mean(x3 * x3, axis=-1, keepdims=True)
        h3 = x3 * lax.rsqrt(ms + EPS) * n2g_ref[...] * (1.0 + sc2) + sh2
        hb_s[r0:r0 + rows, :] = h3.reshape(rows, D_MODEL).astype(BF16)
        hb = hb_s[r0:r0 + rows, :]
        for c in range(D_FF // FF_GROUP):
            gt = _dot(hb, win_ref[:, c * FF_GROUP:(c + 1) * FF_GROUP])
            up = _dot(hb, win_ref[:, D_FF + c * FF_GROUP:D_FF + (c + 1) * FF_GROUP])
            act_s[r0:r0 + rows, c * FF_GROUP:(c + 1) * FF_GROUP] = (
                gt * _sigmoid(gt) * up).astype(BF16)
        f = _dot(act_s[r0:r0 + rows, :], wout_ref[...])
        x2 = x3 + g2 * f.reshape(n_seq, sub, D_MODEL)
        ms2 = jnp.mean(x2 * x2, axis=-1, keepdims=True)
        y_ref[:, t0:t0 + sub, :] = x2 * lax.rsqrt(ms2 + EPS) * nfg_ref[...]


def _ffn(xp, modp, xs, mods, n2g, win, wout, nfg, *, tbl, sub, name):
    n_streams, length, _ = xp.shape
    assert length % tbl == 0
    per_stream = length // tbl
    n_prompt_steps = n_streams * per_stream
    rows = max(tbl, xs.shape[0] * xs.shape[1])
    kern = functools.partial(_ffn_kernel, n_prompt_steps=n_prompt_steps, tbl=tbl, sub=sub)

    def prompt_block(t):
        tc = jnp.minimum(t, n_prompt_steps - 1)
        return tc // per_stream, tc % per_stream

    def const(shape):
        nd = len(shape)
        return pl.BlockSpec(shape, lambda t: (0,) * nd, pipeline_mode=pl.Buffered(1))

    return pl.pallas_call(
        kern, grid=(n_prompt_steps + 1,),
        in_specs=[
            pl.BlockSpec((1, tbl, D_MODEL), lambda t: (*prompt_block(t), 0)),
            pl.BlockSpec((1, N_MOD, D_MODEL), lambda t: (prompt_block(t)[0], 0, 0)),
            const(xs.shape), const(mods.shape),
            const(n2g.shape), const(win.shape), const(wout.shape), const(nfg.shape),
        ],
        out_specs=[
            pl.BlockSpec((1, tbl, D_MODEL), lambda t: (*prompt_block(t), 0)),
            pl.BlockSpec(xs.shape, lambda t: (0, 0, 0)),
        ],
        out_shape=[jax.ShapeDtypeStruct(xp.shape, F32), jax.ShapeDtypeStruct(xs.shape, F32)],
        scratch_shapes=[pltpu.VMEM((rows, D_MODEL), BF16), pltpu.VMEM((rows, D_FF), BF16)],
        compiler_params=pltpu.CompilerParams(
            dimension_semantics=("arbitrary",), vmem_limit_bytes=VMEM_LIMIT),
        name=name,
    )(xp, modp, xs, mods, n2g, win, wout, nfg)


def kernel(x_prompt, x_sample, c_prompt, c_sample, state_gla, cache_conv, w_mod, b_mod, norm1_g,
           w_in, w_alpha, b_alpha, gla_norm_g, w_gla_out, conv_w, conv_b, w_conv_out, w_o,
           norm2_g, w_ffn_in, w_ffn_out, norm_f_g):
    bp = x_prompt.shape[0]
    bs = x_sample.shape[0]
    depth = w_mod.shape[0]
    assert depth == 1
    l = 0
    c_all = jnp.concatenate([c_prompt, c_sample], axis=0)
    n_c = c_all.shape[0]
    n_pad = -n_c % SUBLANES
    c_all = jnp.concatenate([c_all, jnp.zeros((n_pad, D_MODEL), F32)], axis=0)
    mod = _modulation(c_all, w_mod[l], b_mod[l]).reshape(n_c + n_pad, N_MOD, D_MODEL)
    mod_p, mod_s = mod[:bp], mod[bp:bp + bs]

    assert w_in.shape[1:] == (D_MODEL, IN_DIM)
    win_t = jnp.swapaxes(w_in[l], 0, 1).astype(BF16)
    walpha = w_alpha[l].astype(BF16)
    mix_w = (norm1_g[l].reshape(1, -1), win_t, walpha, b_alpha[l].reshape(1, -1),
             gla_norm_g[l].reshape(1, -1), w_gla_out[l].astype(BF16), conv_w[l],
             conv_b[l].reshape(1, -1), w_conv_out[l].astype(BF16), w_o[l].astype(BF16))

    s0_p = jnp.zeros((bp, GLA_HEADS, GLA_DK, GLA_DV), F32)
    c0_p = jnp.zeros((bp, CONV_W - 1, CONV_DIM), F32)
    x1_p, st_p, cv_p, wffn_in, wffn_out = _mixer(
        x_prompt, mod_p, s0_p, c0_p, mix_w, n_seq=1, tbl=512, chunk=256, name="mixer_prompt",
        cast=(w_ffn_in[l], w_ffn_out[l]))
    ffn_w = (norm2_g[l].reshape(1, -1), wffn_in, wffn_out, norm_f_g.reshape(1, -1))
    x1_s, st_s, cv_s = _mixer(x_sample, mod_s, state_gla[l], cache_conv[l], mix_w,
                              n_seq=8, tbl=x_sample.shape[1], chunk=x_sample.shape[1],
                              name="mixer_sample")
    y_p, y_s = _ffn(x1_p, mod_p, x1_s, mod_s, *ffn_w, tbl=1024, sub=256, name="ffn")
    return (y_p, y_s, st_p[None], cv_p[None], st_s[None], cv_s[None])
```

```python
import functools

import jax
import jax.numpy as jnp
from jax import lax
from jax.experimental import pallas as pl
from jax.experimental.pallas import tpu as pltpu

F32 = jnp.float32
BF16 = jnp.bfloat16

D_MODEL = 1024
GLA_HEADS = 4
GLA_DK = 128
GLA_DV = 256
GLA_KEY = GLA_HEADS * GLA_DK
GLA_VAL = GLA_HEADS * GLA_DV
GLA_RANK = 16
GLA_TAU = 16.0
CONV_DIM = D_MODEL
CONV_W = 3
D_FF = 2816
N_MOD = 6
EPS = 1e-6

COL_GROUP = 256
LANES = 128
SUBLANES = 8

OFF_Q = 0
OFF_K = OFF_Q + GLA_KEY
OFF_V = OFF_K + GLA_KEY
OFF_G = OFF_V + GLA_VAL
OFF_A = OFF_G + GLA_VAL
OFF_CB = OFF_A + GLA_RANK
OFF_CC = OFF_CB + CONV_DIM
OFF_CH = OFF_CC + CONV_DIM
OFF_GA = OFF_CH + CONV_DIM
OFF_GB = OFF_GA + D_MODEL
IN_DIM = OFF_GB + D_MODEL

SAFE_LOG_DECAY = 60.0

VMEM_LIMIT = 60 * 1024 * 1024


def _dot(a, b):
    return jnp.dot(a, b, preferred_element_type=F32)


def _dot_nt(a, b):
    return lax.dot_general(a, b, (((1,), (1,)), ((), ())), preferred_element_type=F32)


def _dot_tn(a, b):
    return lax.dot_general(a, b, (((0,), (0,)), ((), ())), preferred_element_type=F32)


def _sigmoid(x):
    return 1.0 / (1.0 + jnp.exp(-x))


def _mod_kernel(c_ref, w_ref, b_ref, o_ref):
    o_ref[...] = _dot(c_ref[...].astype(BF16), w_ref[...].astype(BF16)) + b_ref[...]


def _modulation(c_all, w_mod, b_mod):
    n = c_all.shape[0]
    nblk = N_MOD
    return pl.pallas_call(
        _mod_kernel,
        grid=(nblk,),
        in_specs=[
            pl.BlockSpec((n, D_MODEL), lambda j: (0, 0)),
            pl.BlockSpec((D_MODEL, D_MODEL), lambda j: (0, j)),
            pl.BlockSpec((1, D_MODEL), lambda j: (0, j)),
        ],
        out_specs=pl.BlockSpec((n, D_MODEL), lambda j: (0, j)),
        out_shape=jax.ShapeDtypeStruct((n, N_MOD * D_MODEL), F32),
        name="adaln_mod",
    )(c_all, w_mod, b_mod.reshape(1, -1))


def _mixer_kernel(*refs, n_seq, tbl, chunk, unroll, n_cast):
    (x_ref, mod_ref, s0_ref, cprev_ref, n1g_ref, win_ref, walpha_ref, balpha_ref,
     gng_ref, wgla_ref, convw_ref, convb_ref, wconv_ref, wo_ref) = refs[:14]
    cast_in = refs[14:14 + n_cast]
    x1_ref, sout_ref, cout_ref = refs[14 + n_cast:17 + n_cast]
    cast_out = refs[17 + n_cast:17 + 2 * n_cast]
    (hb_s, q_s, k_s, v_s, sg_s, b_s, ubuf_s, ub_s, yb_s, o_s, og_s,
     st_s) = refs[17 + 2 * n_cast:]
    rows = n_seq * tbl
    nchunk = tbl // chunk
    j = pl.program_id(1)
    nj = pl.num_programs(1)

    @pl.when(j == 0)
    def _init():
        for s in range(n_seq):
            for h in range(GLA_HEADS):
                st_s[s, h] = s0_ref[s, h].T
        ubuf_s[:, SUBLANES - 2:SUBLANES, :] = cprev_ref[...]

    x3 = x_ref[...]
    sh1 = mod_ref[:, 0:1, :]
    sc1 = mod_ref[:, 1:2, :]
    g1 = mod_ref[:, 2:3, :]
    ms = jnp.mean(x3 * x3, axis=-1, keepdims=True)
    h3 = x3 * lax.rsqrt(ms + EPS) * n1g_ref[...] * (1.0 + sc1) + sh1
    hb_s[...] = h3.reshape(rows, D_MODEL).astype(BF16)

    def proj(off, c, n=COL_GROUP):
        return _dot_nt(hb_s[...], win_ref[off + c * n:off + (c + 1) * n, :])

    def cols(c):
        return slice(c * COL_GROUP, (c + 1) * COL_GROUP)

    za = proj(OFF_A, 0, GLA_RANK)
    xa = _dot(za.astype(BF16), walpha_ref[...]) + balpha_ref[...]
    for c in range(GLA_KEY // COL_GROUP):
        q_s[:, cols(c)] = proj(OFF_Q, c) * (GLA_DK ** -0.5)
    la = (jnp.minimum(xa, 0.0) - jnp.log(1.0 + jnp.exp(-jnp.abs(xa)))) * (1.0 / GLA_TAU)
    for c in range(GLA_KEY // COL_GROUP):
        k_s[:, cols(c)] = proj(OFF_K, c)

    ri = lax.broadcasted_iota(jnp.int32, (chunk, chunk), 0)
    ci = lax.broadcasted_iota(jnp.int32, (chunk, chunk), 1)
    tril = ri >= ci
    tmat = tril.astype(BF16)
    bmin = jnp.zeros((1, GLA_KEY), F32)
    for idx in range(n_seq * nchunk):
        la_c = la[idx * chunk:(idx + 1) * chunk, :]
        hi = la_c.astype(BF16)
        lo = (la_c - hi.astype(F32)).astype(BF16)
        b_c = _dot(tmat, hi) + _dot(tmat, lo)
        b_s[idx * chunk:(idx + 1) * chunk, :] = b_c
        bmin = jnp.minimum(bmin, b_c[chunk - 1:chunk, :])
    safe = jnp.min(bmin) > -SAFE_LOG_DECAY

    for c in range(GLA_VAL // COL_GROUP):
        v_s[:, cols(c)] = proj(OFF_V, c).astype(BF16)
        g = proj(OFF_G, c)
        sg_s[:, cols(c)] = (g * _sigmoid(g)).astype(BF16)

    gng = gng_ref[...]
    n_idx = n_seq * nchunk

    def loop(body):
        if unroll:
            for idx in range(n_idx):
                body(idx * chunk, idx // nchunk)
        else:
            def fbody(idx, carry):
                body(pl.multiple_of(idx * chunk, chunk), idx // nchunk)
                return carry
            lax.fori_loop(0, n_idx, fbody, 0)

    def intra_fast(r0, s):
        q = q_s[pl.ds(r0, chunk), :]
        k = k_s[pl.ds(r0, chunk), :]
        b = b_s[pl.ds(r0, chunk), :]
        v = v_s[pl.ds(r0, chunk), :]
        bmid = b[chunk // 2 - 1:chunk // 2, :]
        qs = (q * jnp.exp(b - bmid)).astype(BF16)
        ks = (k * jnp.exp(bmid - b)).astype(BF16)
        for h in range(GLA_HEADS):
            ksl = slice(h * GLA_DK, (h + 1) * GLA_DK)
            vsl = slice(h * GLA_DV, (h + 1) * GLA_DV)
            a = jnp.where(tril, _dot_nt(qs[:, ksl], ks[:, ksl]), 0.0)
            o_s[pl.ds(r0, chunk), vsl] = _dot(a.astype(BF16), v[:, vsl])

    def conv_group(c):
        u = proj(OFF_CC, c) * proj(OFF_CH, c)
        ubuf_s[:, SUBLANES:, cols(c)] = u.reshape(n_seq, tbl, COL_GROUP)
        conv = (ubuf_s[:, SUBLANES - 2:SUBLANES - 2 + tbl, cols(c)] * convw_ref[0:1, cols(c)]
                + ubuf_s[:, SUBLANES - 1:SUBLANES - 1 + tbl, cols(c)] * convw_ref[1:2, cols(c)]
                + u.reshape(n_seq, tbl, COL_GROUP) * convw_ref[2:3, cols(c)]
                + convb_ref[:, cols(c)])
        cb = proj(OFF_CB, c)
        ub_s[:, cols(c)] = (cb.reshape(n_seq, tbl, COL_GROUP) * conv).reshape(
            rows, COL_GROUP).astype(BF16)

    loop(intra_fast)

    @pl.when(jnp.logical_not(safe))
    def _pairwise():
        hr = lax.broadcasted_iota(jnp.int32, (GLA_KEY, GLA_VAL), 0) // GLA_DK
        hc = lax.broadcasted_iota(jnp.int32, (GLA_KEY, GLA_VAL), 1) // GLA_DV
        headsel = (hr == hc).astype(BF16)
        rowid = lax.broadcasted_iota(jnp.int32, (chunk, 1), 0)

        def cbody(idx, carry):
            r0 = pl.multiple_of(idx * chunk, chunk)
            q = q_s[pl.ds(r0, chunk), :]
            k = k_s[pl.ds(r0, chunk), :]
            b = b_s[pl.ds(r0, chunk), :]
            vf = v_s[pl.ds(r0, chunk), :].astype(F32)

            def jbody(jj, acc):
                pick = rowid == jj
                kj = jnp.sum(jnp.where(pick, k, 0.0), axis=0, keepdims=True)
                bj = jnp.sum(jnp.where(pick, b, 0.0), axis=0, keepdims=True)
                vj = jnp.sum(jnp.where(pick, vf, 0.0), axis=0, keepdims=True)
                dm = q * kj * jnp.exp(jnp.minimum(b - bj, 0.0))
                dm = jnp.where(rowid >= jj, dm, 0.0)
                return acc + _dot(dm.astype(BF16), headsel) * vj

            o_s[pl.ds(r0, chunk), :] = lax.fori_loop(
                0, chunk, jbody, jnp.zeros((chunk, GLA_VAL), F32))
            return carry

        lax.fori_loop(0, n_idx, cbody, 0)

    def state_step(r0, s):
        q = q_s[pl.ds(r0, chunk), :]
        k = k_s[pl.ds(r0, chunk), :]
        b = b_s[pl.ds(r0, chunk), :]
        v = v_s[pl.ds(r0, chunk), :]
        blast = b[chunk - 1:chunk, :]
        qb = (q * jnp.exp(b)).astype(BF16)
        kb = (k * jnp.exp(blast - b)).astype(BF16)
        dec = jnp.exp(blast)
        for h in range(GLA_HEADS):
            ksl = slice(h * GLA_DK, (h + 1) * GLA_DK)
            vsl = slice(h * GLA_DV, (h + 1) * GLA_DV)
            st = st_s[s, h]
            o_h = o_s[pl.ds(r0, chunk), vsl] + _dot_nt(qb[:, ksl], st.astype(BF16))
            st_s[s, h] = dec[:, ksl] * st + _dot_tn(v[:, vsl], kb[:, ksl])
            o_h = o_h * lax.rsqrt(jnp.mean(o_h * o_h, axis=-1, keepdims=True) + EPS) * gng
            og_s[pl.ds(r0, chunk), vsl] = (
                o_h * sg_s[pl.ds(r0, chunk), vsl].astype(F32)).astype(BF16)

    loop(state_step)

    for c in range(CONV_DIM // COL_GROUP):
        conv_group(c)
    tail = ubuf_s[:, tbl + SUBLANES - 2:tbl + SUBLANES, :]
    ubuf_s[:, SUBLANES - 2:SUBLANES, :] = tail
    cout_ref[...] = tail
    for c in range(D_MODEL // COL_GROUP):
        yb = _dot(ub_s[...], wconv_ref[:, cols(c)])
        yb_s[:, cols(c)] = (_sigmoid(proj(OFF_GB, c)) * yb).astype(BF16)

    for src, dst in zip(cast_in, cast_out):
        dst[...] = src[...].astype(BF16)

    for c in range(D_MODEL // COL_GROUP):
        ya = _dot(og_s[...], wgla_ref[:, cols(c)])
        merged = _sigmoid(proj(OFF_GA, c)) * ya + yb_s[:, cols(c)].astype(F32)
        ub_s[:, cols(c)] = merged.astype(BF16)
    for c in range(D_MODEL // COL_GROUP):
        m = _dot(ub_s[...], wo_ref[:, cols(c)])
        x1_ref[:, :, cols(c)] = (x_ref[:, :, cols(c)]
                                 + g1[:, :, cols(c)] * m.reshape(n_seq, tbl, COL_GROUP))

    @pl.when(j == nj - 1)
    def _final_state():
        for s in range(n_seq):
            for h in range(GLA_HEADS):
                sout_ref[s, h] = st_s[s, h].T


def _const_spec(shape):
    nd = len(shape)
    return pl.BlockSpec(shape, lambda i, j: (0,) * nd, pipeline_mode=pl.Buffered(1))


BF16_SUBLANES = 16


def _cast_blocking(n_rows, n_steps):
    for n_blocks in range(n_steps, 0, -1):
        if (n_steps % n_blocks == 0 and n_rows % n_blocks == 0
                and (n_rows // n_blocks) % BF16_SUBLANES == 0):
            return n_rows // n_blocks, n_steps // n_blocks
    raise ValueError((n_rows, n_steps))


def _mixer(x, mod, s0, cprev, weights, *, n_seq, tbl, chunk, name, cast=()):
    n_streams, length, _ = x.shape
    assert n_streams % n_seq == 0 and length % tbl == 0
    grid = (n_streams // n_seq, length // tbl)
    rows = n_seq * tbl
    (n1g, win, walpha, balpha, gng, wgla, convw, convb, wconv, wo) = weights
    kern = functools.partial(_mixer_kernel, n_seq=n_seq, tbl=tbl, chunk=chunk,
                             unroll=True, n_cast=len(cast))
    cast_specs = []
    for w in cast:
        rb, steps = _cast_blocking(w.shape[0], grid[0] * grid[1])
        cast_specs.append(pl.BlockSpec(
            (rb, w.shape[1]), lambda i, j, steps=steps: ((i * grid[1] + j) // steps, 0)))
    in_specs = [
        pl.BlockSpec((n_seq, tbl, D_MODEL), lambda i, j: (i, j, 0)),
        pl.BlockSpec((n_seq, N_MOD, D_MODEL), lambda i, j: (i, 0, 0)),
        pl.BlockSpec((n_seq, GLA_HEADS, GLA_DK, GLA_DV), lambda i, j: (i, 0, 0, 0)),
        pl.BlockSpec((n_seq, CONV_W - 1, CONV_DIM), lambda i, j: (i, 0, 0)),
        _const_spec(n1g.shape), _const_spec(win.shape), _const_spec(walpha.shape),
        _const_spec(balpha.shape), _const_spec(gng.shape), _const_spec(wgla.shape),
        _const_spec(convw.shape), _const_spec(convb.shape), _const_spec(wconv.shape),
        _const_spec(wo.shape),
    ] + cast_specs
    out_specs = [
        pl.BlockSpec((n_seq, tbl, D_MODEL), lambda i, j: (i, j, 0)),
        pl.BlockSpec((n_seq, GLA_HEADS, GLA_DK, GLA_DV), lambda i, j: (i, 0, 0, 0)),
        pl.BlockSpec((n_seq, CONV_W - 1, CONV_DIM), lambda i, j: (i, 0, 0)),
    ] + cast_specs
    out_shape = [
        jax.ShapeDtypeStruct(x.shape, F32),
        jax.ShapeDtypeStruct((n_streams, GLA_HEADS, GLA_DK, GLA_DV), F32),
        jax.ShapeDtypeStruct((n_streams, CONV_W - 1, CONV_DIM), F32),
    ] + [jax.ShapeDtypeStruct(w.shape, BF16) for w in cast]
    scratch = [
        pltpu.VMEM((rows, D_MODEL), BF16),
        pltpu.VMEM((rows, GLA_KEY), F32),
        pltpu.VMEM((rows, GLA_KEY), F32),
        pltpu.VMEM((rows, GLA_VAL), BF16),
        pltpu.VMEM((rows, GLA_VAL), BF16),
        pltpu.VMEM((rows, GLA_KEY), F32),
        pltpu.VMEM((n_seq, tbl + SUBLANES, CONV_DIM), F32),
        pltpu.VMEM((rows, D_MODEL), BF16),
        pltpu.VMEM((rows, D_MODEL), BF16),
        pltpu.VMEM((rows, GLA_VAL), F32),
        pltpu.VMEM((rows, GLA_VAL), BF16),
        pltpu.VMEM((n_seq, GLA_HEADS, GLA_DV, GLA_DK), F32),
    ]
    return pl.pallas_call(
        kern, grid=grid, in_specs=in_specs, out_specs=out_specs, out_shape=out_shape,
        scratch_shapes=scratch,
        compiler_params=pltpu.CompilerParams(
            dimension_semantics=("arbitrary", "arbitrary"), vmem_limit_bytes=VMEM_LIMIT),
        name=name,
    )(x, mod, s0, cprev, n1g, win, walpha, balpha, gng, wgla, convw, convb, wconv, wo, *cast)


FF_GROUP = 256


def _ffn_kernel(x_ref, mod_ref, n2g_ref, win_ref, wout_ref, nfg_ref, y_ref, hb_s, act_s,
                *, n_seq, tbl, sub):
    sh2 = mod_ref[:, 3:4, :]
    sc2 = mod_ref[:, 4:5, :]
    g2 = mod_ref[:, 5:6, :]
    for t0 in range(0, tbl, sub):
        rows = n_seq * sub
        r0 = (t0 // sub) * rows
        x3 = x_ref[:, t0:t0 + sub, :]
        ms = jnp.mean(x3 * x3, axis=-1, keepdims=True)
        h3 = x3 * lax.rsqrt(ms + EPS) * n2g_ref[...] * (1.0 + sc2) + sh2
        hb_s[r0:r0 + rows, :] = h3.reshape(rows, D_MODEL).astype(BF16)
        hb = hb_s[r0:r0 + rows, :]
        for c in range(D_FF // FF_GROUP):
            gt = _dot(hb, win_ref[:, c * FF_GROUP:(c + 1) * FF_GROUP])
            up = _dot(hb, win_ref[:, D_FF + c * FF_GROUP:D_FF + (c + 1) * FF_GROUP])
            act_s[r0:r0 + rows, c * FF_GROUP:(c + 1) * FF_GROUP] = (
                gt * _sigmoid(gt) * up).astype(BF16)
        f = _dot(act_s[r0:r0 + rows, :], wout_ref[...])
        x2 = x3 + g2 * f.reshape(n_seq, sub, D_MODEL)
        ms2 = jnp.mean(x2 * x2, axis=-1, keepdims=True)
        y_ref[:, t0:t0 + sub, :] = x2 * lax.rsqrt(ms2 + EPS) * nfg_ref[...]


def _ffn(x, mod, n2g, win, wout, nfg, *, n_seq, tbl, sub, name):
    n_streams, length, _ = x.shape
    assert n_streams % n_seq == 0 and length % tbl == 0
    grid = (n_streams // n_seq, length // tbl)
    rows = n_seq * tbl
    kern = functools.partial(_ffn_kernel, n_seq=n_seq, tbl=tbl, sub=sub)
    return pl.pallas_call(
        kern, grid=grid,
        in_specs=[
            pl.BlockSpec((n_seq, tbl, D_MODEL), lambda i, j: (i, j, 0)),
            pl.BlockSpec((n_seq, N_MOD, D_MODEL), lambda i, j: (i, 0, 0)),
            _const_spec(n2g.shape), _const_spec(win.shape), _const_spec(wout.shape),
            _const_spec(nfg.shape),
        ],
        out_specs=pl.BlockSpec((n_seq, tbl, D_MODEL), lambda i, j: (i, j, 0)),
        out_shape=jax.ShapeDtypeStruct(x.shape, F32),
        scratch_shapes=[pltpu.VMEM((rows, D_MODEL), BF16), pltpu.VMEM((rows, D_FF), BF16)],
        compiler_params=pltpu.CompilerParams(
            dimension_semantics=("arbitrary", "arbitrary"), vmem_limit_bytes=VMEM_LIMIT),
        name=name,
    )(x, mod, n2g, win, wout, nfg)


def kernel(x_prompt, x_sample, c_prompt, c_sample, state_gla, cache_conv, w_mod, b_mod, norm1_g,
           w_in, w_alpha, b_alpha, gla_norm_g, w_gla_out, conv_w, conv_b, w_conv_out, w_o,
           norm2_g, w_ffn_in, w_ffn_out, norm_f_g):
    bp = x_prompt.shape[0]
    bs = x_sample.shape[0]
    depth = w_mod.shape[0]
    assert depth == 1
    l = 0
    c_all = jnp.concatenate([c_prompt, c_sample], axis=0)
    n_c = c_all.shape[0]
    n_pad = -n_c % SUBLANES
    c_all = jnp.concatenate([c_all, jnp.zeros((n_pad, D_MODEL), F32)], axis=0)
    mod = _modulation(c_all, w_mod[l], b_mod[l]).reshape(n_c + n_pad, N_MOD, D_MODEL)
    mod_p, mod_s = mod[:bp], mod[bp:bp + bs]

    assert w_in.shape[1:] == (D_MODEL, IN_DIM)
    win_t = jnp.swapaxes(w_in[l], 0, 1).astype(BF16)
    walpha = w_alpha[l].astype(BF16)
    mix_w = (norm1_g[l].reshape(1, -1), win_t, walpha, b_alpha[l].reshape(1, -1),
             gla_norm_g[l].reshape(1, -1), w_gla_out[l].astype(BF16), conv_w[l],
             conv_b[l].reshape(1, -1), w_conv_out[l].astype(BF16), w_o[l].astype(BF16))

    s0_p = jnp.zeros((bp, GLA_HEADS, GLA_DK, GLA_DV), F32)
    c0_p = jnp.zeros((bp, CONV_W - 1, CONV_DIM), F32)
    x1_p, st_p, cv_p, wffn_in, wffn_out = _mixer(
        x_prompt, mod_p, s0_p, c0_p, mix_w, n_seq=1, tbl=512, chunk=256, name="mixer_prompt",
        cast=(w_ffn_in[l], w_ffn_out[l]))
    ffn_w = (norm2_g[l].reshape(1, -1), wffn_in, wffn_out, norm_f_g.reshape(1, -1))
    x1_s, st_s, cv_s = _mixer(x_sample, mod_s, state_gla[l], cache_conv[l], mix_w,
                              n_seq=8, tbl=x_sample.shape[1], chunk=x_sample.shape[1],
                              name="mixer_sample")
    y_p = _ffn(x1_p, mod_p, *ffn_w, n_seq=1, tbl=1024, sub=256, name="ffn_prompt")
    y_s = _ffn(x1_s, mod_s, *ffn_w, n_seq=bs, tbl=x_sample.shape[1], sub=x_sample.shape[1],
               name="ffn_sample")
    return (y_p, y_s, st_p[None], cv_p[None], st_s[None], cv_s[None])
```

```python
import functools

import jax
import jax.numpy as jnp
from jax import lax
from jax.experimental import pallas as pl
from jax.experimental.pallas import tpu as pltpu

F32 = jnp.float32
BF16 = jnp.bfloat16

D_MODEL = 1024
GLA_HEADS = 4
GLA_DK = 128
GLA_DV = 256
GLA_KEY = GLA_HEADS * GLA_DK
GLA_VAL = GLA_HEADS * GLA_DV
GLA_RANK = 16
GLA_TAU = 16.0
CONV_DIM = D_MODEL
CONV_W = 3
D_FF = 2816
N_MOD = 6
EPS = 1e-6

COL_GROUP = 256
LANES = 128
SUBLANES = 8

OFF_Q = 0
OFF_K = OFF_Q + GLA_KEY
OFF_V = OFF_K + GLA_KEY
OFF_G = OFF_V + GLA_VAL
OFF_A = OFF_G + GLA_VAL
OFF_CB = OFF_A + GLA_RANK
OFF_CC = OFF_CB + CONV_DIM
OFF_CH = OFF_CC + CONV_DIM
OFF_GA = OFF_CH + CONV_DIM
OFF_GB = OFF_GA + D_MODEL
IN_DIM = OFF_GB + D_MODEL

SAFE_LOG_DECAY = 60.0

VMEM_LIMIT = 63 * 1024 * 1024


def _dot(a, b):
    return jnp.dot(a, b, preferred_element_type=F32)


def _dot_nt(a, b):
    return lax.dot_general(a, b, (((1,), (1,)), ((), ())), preferred_element_type=F32)


def _dot_tn(a, b):
    return lax.dot_general(a, b, (((0,), (0,)), ((), ())), preferred_element_type=F32)


def _sigmoid(x):
    return 1.0 / (1.0 + jnp.exp(-x))


def _mod_kernel(c_ref, w_ref, b_ref, o_ref):
    o_ref[...] = _dot(c_ref[...].astype(BF16), w_ref[...].astype(BF16)) + b_ref[...]


def _modulation(c_all, w_mod, b_mod):
    n = c_all.shape[0]
    nblk = N_MOD
    return pl.pallas_call(
        _mod_kernel,
        grid=(nblk,),
        in_specs=[
            pl.BlockSpec((n, D_MODEL), lambda j: (0, 0)),
            pl.BlockSpec((D_MODEL, D_MODEL), lambda j: (0, j)),
            pl.BlockSpec((1, D_MODEL), lambda j: (0, j)),
        ],
        out_specs=pl.BlockSpec((n, D_MODEL), lambda j: (0, j)),
        out_shape=jax.ShapeDtypeStruct((n, N_MOD * D_MODEL), F32),
        name="adaln_mod",
    )(c_all, w_mod, b_mod.reshape(1, -1))


def _mixer_kernel(*refs, n_seq, tbl, chunk, unroll, n_cast, prefetch):
    n_in = 15 if prefetch else 14
    x_ref = refs[0]
    xn_ref = refs[1] if prefetch else None
    (mod_ref, s0_ref, cprev_ref, n1g_ref, win_ref, walpha_ref, balpha_ref,
     gng_ref, wgla_ref, convw_ref, convb_ref, wconv_ref, wo_ref) = refs[n_in - 13:n_in]
    cast_in = refs[n_in:n_in + n_cast]
    x1_ref, sout_ref, cout_ref = refs[n_in + n_cast:n_in + 3 + n_cast]
    cast_out = refs[n_in + 3 + n_cast:n_in + 3 + 2 * n_cast]
    (hb_s, q_s, k_s, v_s, sg_s, b_s, ubuf_s, ub_s, yb_s, o_s, og_s,
     st_s) = refs[n_in + 3 + 2 * n_cast:]
    rows = n_seq * tbl
    nchunk = tbl // chunk
    j = pl.program_id(1)
    nj = pl.num_programs(1)

    @pl.when(j == 0)
    def _init():
        for s in range(n_seq):
            for h in range(GLA_HEADS):
                st_s[s, h] = s0_ref[s, h].T
        ubuf_s[:, SUBLANES - 2:SUBLANES, :] = cprev_ref[...]

    sh1 = mod_ref[:, 0:1, :]
    sc1 = mod_ref[:, 1:2, :]
    g1 = mod_ref[:, 2:3, :]

    def prenorm(xr):
        x3 = xr[...]
        ms = jnp.mean(x3 * x3, axis=-1, keepdims=True)
        h3 = x3 * lax.rsqrt(ms + EPS) * n1g_ref[...] * (1.0 + sc1) + sh1
        return h3.reshape(rows, D_MODEL).astype(BF16)

    if prefetch:
        slot = j % 2

        @pl.when(j == 0)
        def _first_block():
            hb_s[0] = prenorm(x_ref)
    else:
        slot = 0
        hb_s[0] = prenorm(x_ref)
    hb_cur = hb_s.at[slot]

    def proj(off, c, n=COL_GROUP):
        return _dot_nt(hb_cur[...], win_ref[off + c * n:off + (c + 1) * n, :])

    def cols(c):
        return slice(c * COL_GROUP, (c + 1) * COL_GROUP)

    za = proj(OFF_A, 0, GLA_RANK)
    for c in range(GLA_KEY // COL_GROUP):
        q_s[:, cols(c)] = proj(OFF_Q, c) * (GLA_DK ** -0.5)
    xa = _dot(za.astype(BF16), walpha_ref[...]) + balpha_ref[...]
    la =(jnp.minimum(xa, 0.0) - jnp.log(1.0 + jnp.exp(-jnp.abs(xa)))) * (1.0 / GLA_TAU)
    for c in range(GLA_KEY // COL_GROUP):
        k_s[:, cols(c)] = proj(OFF_K, c)

    ri = lax.broadcasted_iota(jnp.int32, (chunk, chunk), 0)
    ci = lax.broadcasted_iota(jnp.int32, (chunk, chunk), 1)
    tril = ri >= ci
    tmat = tril.astype(BF16)
    bmin = jnp.zeros((1, GLA_KEY), F32)
    for idx in range(n_seq * nchunk):
        la_c = la[idx * chunk:(idx + 1) * chunk, :]
        hi = la_c.astype(BF16)
        lo = (la_c - hi.astype(F32)).astype(BF16)
        b_c = _dot(tmat, hi) + _dot(tmat, lo)
        b_s[idx * chunk:(idx + 1) * chunk, :] = b_c
        bmin = jnp.minimum(bmin, b_c[chunk - 1:chunk, :])
    safe = jnp.min(bmin) > -SAFE_LOG_DECAY

    for c in range(GLA_VAL // COL_GROUP):
        v_s[:, cols(c)] = proj(OFF_V, c).astype(BF16)
        g = proj(OFF_G, c)
        sg_s[:, cols(c)] = (g * _sigmoid(g)).astype(BF16)

    gng = gng_ref[...]
    n_idx = n_seq * nchunk

    def loop(body):
        if unroll:
            for idx in range(n_idx):
                body(idx * chunk, idx // nchunk)
        else:
            def fbody(idx, carry):
                body(pl.multiple_of(idx * chunk, chunk), idx // nchunk)
                return carry
            lax.fori_loop(0, n_idx, fbody, 0)

    def intra_fast(r0, s):
        q = q_s[pl.ds(r0, chunk), :]
        k = k_s[pl.ds(r0, chunk), :]
        b = b_s[pl.ds(r0, chunk), :]
        v = v_s[pl.ds(r0, chunk), :]
        bmid = b[chunk // 2 - 1:chunk // 2, :]
        qs = (q * jnp.exp(b - bmid)).astype(BF16)
        ks = (k * jnp.exp(bmid - b)).astype(BF16)
        for h in range(GLA_HEADS):
            ksl = slice(h * GLA_DK, (h + 1) * GLA_DK)
            vsl = slice(h * GLA_DV, (h + 1) * GLA_DV)
            a = jnp.where(tril, _dot_nt(qs[:, ksl], ks[:, ksl]), 0.0)
            o_s[pl.ds(r0, chunk), vsl] = _dot(a.astype(BF16), v[:, vsl])

    def conv_group(c):
        u = proj(OFF_CC, c) * proj(OFF_CH, c)
        ubuf_s[:, SUBLANES:, cols(c)] = u.reshape(n_seq, tbl, COL_GROUP)
        conv = (ubuf_s[:, SUBLANES - 2:SUBLANES - 2 + tbl, cols(c)] * convw_ref[0:1, cols(c)]
                + ubuf_s[:, SUBLANES - 1:SUBLANES - 1 + tbl, cols(c)] * convw_ref[1:2, cols(c)]
                + u.reshape(n_seq, tbl, COL_GROUP) * convw_ref[2:3, cols(c)]
                + convb_ref[:, cols(c)])
        cb = proj(OFF_CB, c)
        ub_s[:, cols(c)] = (cb.reshape(n_seq, tbl, COL_GROUP) * conv).reshape(
            rows, COL_GROUP).astype(BF16)

    loop(intra_fast)

    @pl.when(jnp.logical_not(safe))
    def _pairwise():
        hr = lax.broadcasted_iota(jnp.int32, (GLA_KEY, GLA_VAL), 0) // GLA_DK
        hc = lax.broadcasted_iota(jnp.int32, (GLA_KEY, GLA_VAL), 1) // GLA_DV
        headsel = (hr == hc).astype(BF16)
        rowid = lax.broadcasted_iota(jnp.int32, (chunk, 1), 0)

        def cbody(idx, carry):
            r0 = pl.multiple_of(idx * chunk, chunk)
            q = q_s[pl.ds(r0, chunk), :]
            k = k_s[pl.ds(r0, chunk), :]
            b = b_s[pl.ds(r0, chunk), :]
            vf = v_s[pl.ds(r0, chunk), :].astype(F32)

            def jbody(jj, acc):
                pick = rowid == jj
                kj = jnp.sum(jnp.where(pick, k, 0.0), axis=0, keepdims=True)
                bj = jnp.sum(jnp.where(pick, b, 0.0), axis=0, keepdims=True)
                vj = jnp.sum(jnp.where(pick, vf, 0.0), axis=0, keepdims=True)
                dm = q * kj * jnp.exp(jnp.minimum(b - bj, 0.0))
                dm = jnp.where(rowid >= jj, dm, 0.0)
                return acc + _dot(dm.astype(BF16), headsel) * vj

            o_s[pl.ds(r0, chunk), :] = lax.fori_loop(
                0, chunk, jbody, jnp.zeros((chunk, GLA_VAL), F32))
            return carry

        lax.fori_loop(0, n_idx, cbody, 0)

    def state_step(r0, s):
        q = q_s[pl.ds(r0, chunk), :]
        k = k_s[pl.ds(r0, chunk), :]
        b = b_s[pl.ds(r0, chunk), :]
        v = v_s[pl.ds(r0, chunk), :]
        blast = b[chunk - 1:chunk, :]
        qb = (q * jnp.exp(b)).astype(BF16)
        kb = (k * jnp.exp(blast - b)).astype(BF16)
        dec = jnp.exp(blast)
        for h in range(GLA_HEADS):
            ksl = slice(h * GLA_DK, (h + 1) * GLA_DK)
            vsl = slice(h * GLA_DV, (h + 1) * GLA_DV)
            st = st_s[s, h]
            o_h = o_s[pl.ds(r0, chunk), vsl] + _dot_nt(qb[:, ksl], st.astype(BF16))
            st_s[s, h] = dec[:, ksl] * st + _dot_tn(v[:, vsl], kb[:, ksl])
            o_h = o_h * lax.rsqrt(jnp.mean(o_h * o_h, axis=-1, keepdims=True) + EPS) * gng
            og_s[pl.ds(r0, chunk), vsl] = (
                o_h * sg_s[pl.ds(r0, chunk), vsl].astype(F32)).astype(BF16)

    loop(state_step)

    for c in range(CONV_DIM // COL_GROUP):
        conv_group(c)
    tail = ubuf_s[:, tbl + SUBLANES - 2:tbl + SUBLANES, :]
    ubuf_s[:, SUBLANES - 2:SUBLANES, :] = tail
    cout_ref[...] = tail
    for c in range(D_MODEL // COL_GROUP):
        yb = _dot(ub_s[...], wconv_ref[:, cols(c)])
        yb_s[:, cols(c)] = (_sigmoid(proj(OFF_GB, c)) * yb).astype(BF16)

    for src, dst in zip(cast_in, cast_out):
        dst[...] = src[...].astype(BF16)

    if prefetch:
        hb_s[1 - slot] = prenorm(xn_ref)

    for c in range(D_MODEL // COL_GROUP):
        ya = _dot(og_s[...], wgla_ref[:, cols(c)])
        merged = _sigmoid(proj(OFF_GA, c)) * ya + yb_s[:, cols(c)].astype(F32)
        ub_s[:, cols(c)] = merged.astype(BF16)
    for c in range(D_MODEL // COL_GROUP):
        m = _dot(ub_s[...], wo_ref[:, cols(c)])
        x1_ref[:, :, cols(c)] = (x_ref[:, :, cols(c)]
                                 + g1[:, :, cols(c)] * m.reshape(n_seq, tbl, COL_GROUP))

    @pl.when(j == nj - 1)
    def _final_state():
        for s in range(n_seq):
            for h in range(GLA_HEADS):
                sout_ref[s, h] = st_s[s, h].T


def _const_spec(shape):
    nd = len(shape)
    return pl.BlockSpec(shape, lambda i, j: (0,) * nd, pipeline_mode=pl.Buffered(1))


BF16_SUBLANES = 16


def _cast_blocking(n_rows, n_steps):
    for n_blocks in range(n_steps, 0, -1):
        if (n_steps % n_blocks == 0 and n_rows % n_blocks == 0
                and (n_rows // n_blocks) % BF16_SUBLANES == 0):
            return n_rows // n_blocks, n_steps // n_blocks
    raise ValueError((n_rows, n_steps))


def _mixer(x, mod, s0, cprev, weights, *, n_seq, tbl, chunk, name, cast=()):
    n_streams, length, _ = x.shape
    assert n_streams % n_seq == 0 and length % tbl == 0
    grid = (n_streams // n_seq, length // tbl)
    rows = n_seq * tbl
    (n1g, win, walpha, balpha, gng, wgla, convw, convb, wconv, wo) = weights
    prefetch = grid[1] > 1
    kern = functools.partial(_mixer_kernel, n_seq=n_seq, tbl=tbl, chunk=chunk,
                             unroll=True, n_cast=len(cast), prefetch=prefetch)
    cast_specs = []
    for w in cast:
        rb, steps = _cast_blocking(w.shape[0], grid[0] * grid[1])
        cast_specs.append(pl.BlockSpec(
            (rb, w.shape[1]), lambda i, j, steps=steps: ((i * grid[1] + j) // steps, 0)))
    x_spec = pl.BlockSpec((n_seq, tbl, D_MODEL), lambda i, j: (i, j, 0))
    x_next_spec = pl.BlockSpec(
        (n_seq, tbl, D_MODEL), lambda i, j: (i, jnp.minimum(j + 1, grid[1] - 1), 0))
    in_specs = ([x_spec, x_next_spec] if prefetch else [x_spec]) + [
        pl.BlockSpec((n_seq, N_MOD, D_MODEL), lambda i, j: (i, 0, 0)),
        pl.BlockSpec((n_seq, GLA_HEADS, GLA_DK, GLA_DV), lambda i, j: (i, 0, 0, 0)),
        pl.BlockSpec((n_seq, CONV_W - 1, CONV_DIM), lambda i, j: (i, 0, 0)),
        _const_spec(n1g.shape), _const_spec(win.shape), _const_spec(walpha.shape),
        _const_spec(balpha.shape), _const_spec(gng.shape), _const_spec(wgla.shape),
        _const_spec(convw.shape), _const_spec(convb.shape), _const_spec(wconv.shape),
        _const_spec(wo.shape),
    ] + cast_specs
    out_specs = [
        pl.BlockSpec((n_seq, tbl, D_MODEL), lambda i, j: (i, j, 0)),
        pl.BlockSpec((n_seq, GLA_HEADS, GLA_DK, GLA_DV), lambda i, j: (i, 0, 0, 0)),
        pl.BlockSpec((n_seq, CONV_W - 1, CONV_DIM), lambda i, j: (i, 0, 0)),
    ] + cast_specs
    out_shape = [
        jax.ShapeDtypeStruct(x.shape, F32),
        jax.ShapeDtypeStruct((n_streams, GLA_HEADS, GLA_DK, GLA_DV), F32),
        jax.ShapeDtypeStruct((n_streams, CONV_W - 1, CONV_DIM), F32),
    ] + [jax.ShapeDtypeStruct(w.shape, BF16) for w in cast]
    scratch = [
        pltpu.VMEM((2 if prefetch else 1, rows, D_MODEL), BF16),
        pltpu.VMEM((rows, GLA_KEY), F32),
        pltpu.VMEM((rows, GLA_KEY), F32),
        pltpu.VMEM((rows, GLA_VAL), BF16),
        pltpu.VMEM((rows, GLA_VAL), BF16),
        pltpu.VMEM((rows, GLA_KEY), F32),
        pltpu.VMEM((n_seq, tbl + SUBLANES, CONV_DIM), F32),
        pltpu.VMEM((rows, D_MODEL), BF16),
        pltpu.VMEM((rows, D_MODEL), BF16),
        pltpu.VMEM((rows, GLA_VAL), F32),
        pltpu.VMEM((rows, GLA_VAL), BF16),
        pltpu.VMEM((n_seq, GLA_HEADS, GLA_DV, GLA_DK), F32),
    ]
    return pl.pallas_call(
        kern, grid=grid, in_specs=in_specs, out_specs=out_specs, out_shape=out_shape,
        scratch_shapes=scratch,
        compiler_params=pltpu.CompilerParams(
            dimension_semantics=("arbitrary", "arbitrary"), vmem_limit_bytes=VMEM_LIMIT),
        name=name,
    )(*((x, x) if prefetch else (x,)), mod, s0, cprev, n1g, win, walpha, balpha, gng, wgla,
      convw, convb, wconv, wo, *cast)


FF_GROUP = 256


def _ffn_kernel(x_ref, mod_ref, n2g_ref, win_ref, wout_ref, nfg_ref, y_ref, hb_s, act_s,
                *, n_seq, tbl, sub):
    sh2 = mod_ref[:, 3:4, :]
    sc2 = mod_ref[:, 4:5, :]
    g2 = mod_ref[:, 5:6, :]
    for t0 in range(0, tbl, sub):
        rows = n_seq * sub
        r0 = (t0 // sub) * rows
        x3 = x_ref[:, t0:t0 + sub, :]
        ms = jnp.mean(x3 * x3, axis=-1, keepdims=True)
        h3 = x3 * lax.rsqrt(ms + EPS) * n2g_ref[...] * (1.0 + sc2) + sh2
        hb_s[r0:r0 + rows, :] = h3.reshape(rows, D_MODEL).astype(BF16)
        hb = hb_s[r0:r0 + rows, :]
        for c in range(D_FF // FF_GROUP):
            gt = _dot(hb, win_ref[:, c * FF_GROUP:(c + 1) * FF_GROUP])
            up = _dot(hb, win_ref[:, D_FF + c * FF_GROUP:D_FF + (c + 1) * FF_GROUP])
            act_s[r0:r0 + rows, c * FF_GROUP:(c + 1) * FF_GROUP] = (
                gt * _sigmoid(gt) * up).astype(BF16)
        f = _dot(act_s[r0:r0 + rows, :], wout_ref[...])
        x2 = x3 + g2 * f.reshape(n_seq, sub, D_MODEL)
        ms2 = jnp.mean(x2 * x2, axis=-1, keepdims=True)
        y_ref[:, t0:t0 + sub, :] = x2 * lax.rsqrt(ms2 + EPS) * nfg_ref[...]


def _ffn(x, mod, n2g, win, wout, nfg, *, n_seq, tbl, sub, name):
    n_streams, length, _ = x.shape
    assert n_streams % n_seq == 0 and length % tbl == 0
    grid = (n_streams // n_seq, length // tbl)
    rows = n_seq * tbl
    kern = functools.partial(_ffn_kernel, n_seq=n_seq, tbl=tbl, sub=sub)
    return pl.pallas_call(
        kern, grid=grid,
        in_specs=[
            pl.BlockSpec((n_seq, tbl, D_MODEL), lambda i, j: (i, j, 0)),
            pl.BlockSpec((n_seq, N_MOD, D_MODEL), lambda i, j: (i, 0, 0)),
            _const_spec(n2g.shape), _const_spec(win.shape), _const_spec(wout.shape),
            _const_spec(nfg.shape),
        ],
        out_specs=pl.BlockSpec((n_seq, tbl, D_MODEL), lambda i, j: (i, j, 0)),
        out_shape=jax.ShapeDtypeStruct(x.shape, F32),
        scratch_shapes=[pltpu.VMEM((rows, D_MODEL), BF16), pltpu.VMEM((rows, D_FF), BF16)],
        compiler_params=pltpu.CompilerParams(
            dimension_semantics=("arbitrary", "arbitrary"), vmem_limit_bytes=VMEM_LIMIT),
        name=name,
    )(x, mod, n2g, win, wout, nfg)


def kernel(x_prompt, x_sample, c_prompt, c_sample, state_gla, cache_conv, w_mod, b_mod, norm1_g,
           w_in, w_alpha, b_alpha, gla_norm_g, w_gla_out, conv_w, conv_b, w_conv_out, w_o,
           norm2_g, w_ffn_in, w_ffn_out, norm_f_g):
    bp = x_prompt.shape[0]
    bs = x_sample.shape[0]
    depth = w_mod.shape[0]
    assert depth == 1
    l = 0
    c_all = jnp.concatenate([c_prompt, c_sample], axis=0)
    n_c = c_all.shape[0]
    n_pad = -n_c % SUBLANES
    c_all = jnp.concatenate([c_all, jnp.zeros((n_pad, D_MODEL), F32)], axis=0)
    mod = _modulation(c_all, w_mod[l], b_mod[l]).reshape(n_c + n_pad, N_MOD, D_MODEL)
    mod_p, mod_s = mod[:bp], mod[bp:bp + bs]

    assert w_in.shape[1:] == (D_MODEL, IN_DIM)
    win_t = jnp.swapaxes(w_in[l], 0, 1).astype(BF16)
    walpha = w_alpha[l].astype(BF16)
    mix_w = (norm1_g[l].reshape(1, -1), win_t, walpha, b_alpha[l].reshape(1, -1),
             gla_norm_g[l].reshape(1, -1), w_gla_out[l].astype(BF16), conv_w[l],
             conv_b[l].reshape(1, -1), w_conv_out[l].astype(BF16), w_o[l].astype(BF16))

    s0_p = jnp.zeros((bp, GLA_HEADS, GLA_DK, GLA_DV), F32)
    c0_p = jnp.zeros((bp, CONV_W - 1, CONV_DIM), F32)
    x1_p, st_p, cv_p, wffn_in, wffn_out = _mixer(
        x_prompt, mod_p, s0_p, c0_p, mix_w, n_seq=1, tbl=512, chunk=256, name="mixer_prompt",
        cast=(w_ffn_in[l], w_ffn_out[l]))
    ffn_w = (norm2_g[l].reshape(1, -1), wffn_in, wffn_out, norm_f_g.reshape(1, -1))
    x1_s, st_s, cv_s = _mixer(x_sample, mod_s, state_gla[l], cache_conv[l], mix_w,
                              n_seq=8, tbl=x_sample.shape[1], chunk=x_sample.shape[1],
                              name="mixer_sample")
    y_p = _ffn(x1_p, mod_p, *ffn_w, n_seq=1, tbl=1024, sub=256, name="ffn_prompt")
    y_s = _ffn(x1_s, mod_s, *ffn_w, n_seq=bs, tbl=x_sample.shape[1], sub=x_sample.shape[1],
               name="ffn_sample")
    return (y_p, y_s, st_p[None], cv_p[None], st_s[None], cv_s[None])
```

```python
import functools

import jax
import jax.numpy as jnp
from jax import lax
from jax.experimental import pallas as pl
from jax.experimental.pallas import tpu as pltpu

F32 = jnp.float32
BF16 = jnp.bfloat16

D_MODEL = 1024
GLA_HEADS = 4
GLA_DK = 128
GLA_DV = 256
GLA_KEY = GLA_HEADS * GLA_DK
GLA_VAL = GLA_HEADS * GLA_DV
GLA_RANK = 16
GLA_TAU = 16.0
CONV_DIM = D_MODEL
CONV_W = 3
D_FF = 2816
N_MOD = 6
EPS = 1e-6

COL_GROUP = 256
LANES = 128
SUBLANES = 8

OFF_Q = 0
OFF_K = OFF_Q + GLA_KEY
OFF_V = OFF_K + GLA_KEY
OFF_G = OFF_V + GLA_VAL
OFF_A = OFF_G + GLA_VAL
OFF_CB = OFF_A + GLA_RANK
OFF_CC = OFF_CB + CONV_DIM
OFF_CH = OFF_CC + CONV_DIM
OFF_GA = OFF_CH + CONV_DIM
OFF_GB = OFF_GA + D_MODEL
IN_DIM = OFF_GB + D_MODEL

SAFE_LOG_DECAY = 60.0

VMEM_LIMIT = 63 * 1024 * 1024


def _dot(a, b):
    return jnp.dot(a, b, preferred_element_type=F32)


def _dot_nt(a, b):
    return lax.dot_general(a, b, (((1,), (1,)), ((), ())), preferred_element_type=F32)


def _dot_tn(a, b):
    return lax.dot_general(a, b, (((0,), (0,)), ((), ())), preferred_element_type=F32)


def _sigmoid(x):
    return 1.0 / (1.0 + jnp.exp(-x))


def _mod_kernel(c_ref, w_ref, b_ref, o_ref):
    o_ref[...] = _dot(c_ref[...].astype(BF16), w_ref[...].astype(BF16)) + b_ref[...]


def _modulation(c_all, w_mod, b_mod):
    n = c_all.shape[0]
    nblk = N_MOD
    return pl.pallas_call(
        _mod_kernel,
        grid=(nblk,),
        in_specs=[
            pl.BlockSpec((n, D_MODEL), lambda j: (0, 0)),
            pl.BlockSpec((D_MODEL, D_MODEL), lambda j: (0, j)),
            pl.BlockSpec((1, D_MODEL), lambda j: (0, j)),
        ],
        out_specs=pl.BlockSpec((n, D_MODEL), lambda j: (0, j)),
        out_shape=jax.ShapeDtypeStruct((n, N_MOD * D_MODEL), F32),
        name="adaln_mod",
    )(c_all, w_mod, b_mod.reshape(1, -1))


def _mixer_kernel(*refs, n_seq, tbl, chunk, unroll, n_cast, prefetch):
    n_in = 15 if prefetch else 14
    x_ref = refs[0]
    xn_ref = refs[1] if prefetch else None
    (mod_ref, s0_ref, cprev_ref, n1g_ref, win_ref, walpha_ref, balpha_ref,
     gng_ref, wgla_ref, convw_ref, convb_ref, wconv_ref, wo_ref) = refs[n_in - 13:n_in]
    cast_in = refs[n_in:n_in + n_cast]
    x1_ref, sout_ref, cout_ref = refs[n_in + n_cast:n_in + 3 + n_cast]
    cast_out = refs[n_in + 3 + n_cast:n_in + 3 + 2 * n_cast]
    (hb_s, q_s, k_s, v_s, sg_s, b_s, ubuf_s, ub_s, yb_s, o_s, og_s,
     st_s) = refs[n_in + 3 + 2 * n_cast:]
    rows = n_seq * tbl
    nchunk = tbl // chunk
    j = pl.program_id(1)
    nj = pl.num_programs(1)

    @pl.when(j == 0)
    def _init():
        for s in range(n_seq):
            for h in range(GLA_HEADS):
                st_s[s, h] = s0_ref[s, h].T
        ubuf_s[:, SUBLANES - 2:SUBLANES, :] = cprev_ref[...]

    sh1 = mod_ref[:, 0:1, :]
    sc1 = mod_ref[:, 1:2, :]
    g1 = mod_ref[:, 2:3, :]

    def prenorm(xr):
        x3 = xr[...]
        ms = jnp.mean(x3 * x3, axis=-1, keepdims=True)
        h3 = x3 * lax.rsqrt(ms + EPS) * n1g_ref[...] * (1.0 + sc1) + sh1
        return h3.reshape(rows, D_MODEL).astype(BF16)

    if prefetch:
        slot = j % 2

        @pl.when(j == 0)
        def _first_block():
            hb_s[0] = prenorm(x_ref)
    else:
        slot = 0
        hb_s[0] = prenorm(x_ref)
    hb_cur = hb_s.at[slot]

    def proj(off, c, n=COL_GROUP):
        return _dot_nt(hb_cur[...], win_ref[off + c * n:off + (c + 1) * n, :])

    def cols(c):
        return slice(c * COL_GROUP, (c + 1) * COL_GROUP)

    za = proj(OFF_A, 0, GLA_RANK)
    for c in range(GLA_KEY // COL_GROUP):
        q_s[:, cols(c)] = proj(OFF_Q, c) * (GLA_DK ** -0.5)
    xa = _dot(za.astype(BF16), walpha_ref[...]) + balpha_ref[...]
    la =(jnp.minimum(xa, 0.0) - jnp.log(1.0 + jnp.exp(-jnp.abs(xa)))) * (1.0 / GLA_TAU)
    for c in range(GLA_KEY // COL_GROUP):
        k_s[:, cols(c)] = proj(OFF_K, c)

    ri = lax.broadcasted_iota(jnp.int32, (chunk, chunk), 0)
    ci = lax.broadcasted_iota(jnp.int32, (chunk, chunk), 1)
    tril = ri >= ci
    tmat = tril.astype(BF16)
    bmin = jnp.zeros((1, GLA_KEY), F32)
    for idx in range(n_seq * nchunk):
        la_c = la[idx * chunk:(idx + 1) * chunk, :]
        hi = la_c.astype(BF16)
        lo = (la_c - hi.astype(F32)).astype(BF16)
        b_c = _dot(tmat, hi) + _dot(tmat, lo)
        b_s[idx * chunk:(idx + 1) * chunk, :] = b_c
        bmin = jnp.minimum(bmin, b_c[chunk - 1:chunk, :])
    safe = jnp.min(bmin) > -SAFE_LOG_DECAY

    for c in range(GLA_VAL // COL_GROUP):
        v_s[:, cols(c)] = proj(OFF_V, c).astype(BF16)
        g = proj(OFF_G, c)
        sg_s[:, cols(c)] = (g * _sigmoid(g)).astype(BF16)

    gng = gng_ref[...]
    n_idx = n_seq * nchunk

    def loop(body):
        if unroll:
            for idx in range(n_idx):
                body(idx * chunk, idx // nchunk)
        else:
            def fbody(idx, carry):
                body(pl.multiple_of(idx * chunk, chunk), idx // nchunk)
                return carry
            lax.fori_loop(0, n_idx, fbody, 0)

    def intra_fast(r0, s):
        q = q_s[pl.ds(r0, chunk), :]
        k = k_s[pl.ds(r0, chunk), :]
        b = b_s[pl.ds(r0, chunk), :]
        v = v_s[pl.ds(r0, chunk), :]
        bmid = b[chunk // 2 - 1:chunk // 2, :]
        qs = (q * jnp.exp(b - bmid)).astype(BF16)
        ks = (k * jnp.exp(bmid - b)).astype(BF16)
        for h in range(GLA_HEADS):
            ksl = slice(h * GLA_DK, (h + 1) * GLA_DK)
            vsl = slice(h * GLA_DV, (h + 1) * GLA_DV)
            a = jnp.where(tril, _dot_nt(qs[:, ksl], ks[:, ksl]), 0.0)
            o_s[pl.ds(r0, chunk), vsl] = _dot(a.astype(BF16), v[:, vsl])

    def conv_group(c):
        u = proj(OFF_CC, c) * proj(OFF_CH, c)
        ubuf_s[:, SUBLANES:, cols(c)] = u.reshape(n_seq, tbl, COL_GROUP)
        conv = (ubuf_s[:, SUBLANES - 2:SUBLANES - 2 + tbl, cols(c)] * convw_ref[0:1, cols(c)]
                + ubuf_s[:, SUBLANES - 1:SUBLANES - 1 + tbl, cols(c)] * convw_ref[1:2, cols(c)]
                + u.reshape(n_seq, tbl, COL_GROUP) * convw_ref[2:3, cols(c)]
                + convb_ref[:, cols(c)])
        cb = proj(OFF_CB, c)
        ub_s[:, cols(c)] = (cb.reshape(n_seq, tbl, COL_GROUP) * conv).reshape(
            rows, COL_GROUP).astype(BF16)

    loop(intra_fast)

    @pl.when(jnp.logical_not(safe))
    def _pairwise():
        hr = lax.broadcasted_iota(jnp.int32, (GLA_KEY, GLA_VAL), 0) // GLA_DK
        hc = lax.broadcasted_iota(jnp.int32, (GLA_KEY, GLA_VAL), 1) // GLA_DV
        headsel = (hr == hc).astype(BF16)
        rowid = lax.broadcasted_iota(jnp.int32, (chunk, 1), 0)

        def cbody(idx, carry):
            r0 = pl.multiple_of(idx * chunk, chunk)
            q = q_s[pl.ds(r0, chunk), :]
            k = k_s[pl.ds(r0, chunk), :]
            b = b_s[pl.ds(r0, chunk), :]
            vf = v_s[pl.ds(r0, chunk), :].astype(F32)

            def jbody(jj, acc):
                pick = rowid == jj
                kj = jnp.sum(jnp.where(pick, k, 0.0), axis=0, keepdims=True)
                bj = jnp.sum(jnp.where(pick, b, 0.0), axis=0, keepdims=True)
                vj = jnp.sum(jnp.where(pick, vf, 0.0), axis=0, keepdims=True)
                dm = q * kj * jnp.exp(jnp.minimum(b - bj, 0.0))
                dm = jnp.where(rowid >= jj, dm, 0.0)
                return acc + _dot(dm.astype(BF16), headsel) * vj

            o_s[pl.ds(r0, chunk), :] = lax.fori_loop(
                0, chunk, jbody, jnp.zeros((chunk, GLA_VAL), F32))
            return carry

        lax.fori_loop(0, n_idx, cbody, 0)

    def state_step(r0, s):
        q = q_s[pl.ds(r0, chunk), :]
        k = k_s[pl.ds(r0, chunk), :]
        b = b_s[pl.ds(r0, chunk), :]
        v = v_s[pl.ds(r0, chunk), :]
        blast = b[chunk - 1:chunk, :]
        qb = (q * jnp.exp(b)).astype(BF16)
        kb = (k * jnp.exp(blast - b)).astype(BF16)
        dec = jnp.exp(blast)
        for h in range(GLA_HEADS):
            ksl = slice(h * GLA_DK, (h + 1) * GLA_DK)
            vsl = slice(h * GLA_DV, (h + 1) * GLA_DV)
            st = st_s[s, h]
            o_h = o_s[pl.ds(r0, chunk), vsl] + _dot_nt(qb[:, ksl], st.astype(BF16))
            st_s[s, h] = dec[:, ksl] * st + _dot_tn(v[:, vsl], kb[:, ksl])
            o_h = o_h * lax.rsqrt(jnp.mean(o_h * o_h, axis=-1, keepdims=True) + EPS) * gng
            og_s[pl.ds(r0, chunk), vsl] = (
                o_h * sg_s[pl.ds(r0, chunk), vsl].astype(F32)).astype(BF16)

    loop(state_step)

    for c in range(CONV_DIM // COL_GROUP):
        conv_group(c)
    tail = ubuf_s[:, tbl + SUBLANES - 2:tbl + SUBLANES, :]
    ubuf_s[:, SUBLANES - 2:SUBLANES, :] = tail
    cout_ref[...] = tail
    for c in range(D_MODEL // COL_GROUP):
        yb = _dot(ub_s[...], wconv_ref[:, cols(c)])
        yb_s[:, cols(c)] = (_sigmoid(proj(OFF_GB, c)) * yb).astype(BF16)

    for src, dst in zip(cast_in, cast_out):
        dst[...] = src[...].astype(BF16)

    if prefetch:
        hb_s[1 - slot] = prenorm(xn_ref)

    for c in range(D_MODEL // COL_GROUP):
        ya = _dot(og_s[...], wgla_ref[:, cols(c)])
        merged = _sigmoid(proj(OFF_GA, c)) * ya + yb_s[:, cols(c)].astype(F32)
        ub_s[:, cols(c)] = merged.astype(BF16)
    for c in range(D_MODEL // COL_GROUP):
        m = _dot(ub_s[...], wo_ref[:, cols(c)])
        x1_ref[:, :, cols(c)] = (x_ref[:, :, cols(c)]
                                 + g1[:, :, cols(c)] * m.reshape(n_seq, tbl, COL_GROUP))

    @pl.when(j == nj - 1)
    def _final_state():
        for s in range(n_seq):
            for h in range(GLA_HEADS):
                sout_ref[s, h] = st_s[s, h].T


def _const_spec(shape):
    nd = len(shape)
    return pl.BlockSpec(shape, lambda i, j: (0,) * nd, pipeline_mode=pl.Buffered(1))


BF16_SUBLANES = 16


def _cast_blocking(n_rows, n_steps):
    for n_blocks in range(n_steps, 0, -1):
        if (n_steps % n_blocks == 0 and n_rows % n_blocks == 0
                and (n_rows // n_blocks) % BF16_SUBLANES == 0):
            return n_rows // n_blocks, n_steps // n_blocks
    raise ValueError((n_rows, n_steps))


def _mixer(x, mod, s0, cprev, weights, *, n_seq, tbl, chunk, name, cast=()):
    n_streams, length, _ = x.shape
    assert n_streams % n_seq == 0 and length % tbl == 0
    grid = (n_streams // n_seq, length // tbl)
    rows = n_seq * tbl
    (n1g, win, walpha, balpha, gng, wgla, convw, convb, wconv, wo) = weights
    prefetch = grid[1] > 1
    kern = functools.partial(_mixer_kernel, n_seq=n_seq, tbl=tbl, chunk=chunk,
                             unroll=True, n_cast=len(cast), prefetch=prefetch)
    cast_specs = []
    for w in cast:
        rb, steps = _cast_blocking(w.shape[0], grid[0] * grid[1])
        cast_specs.append(pl.BlockSpec(
            (rb, w.shape[1]), lambda i, j, steps=steps: ((i * grid[1] + j) // steps, 0)))
    x_spec = pl.BlockSpec((n_seq, tbl, D_MODEL), lambda i, j: (i, j, 0))
    x_next_spec = pl.BlockSpec(
        (n_seq, tbl, D_MODEL), lambda i, j: (i, jnp.minimum(j + 1, grid[1] - 1), 0))
    in_specs = ([x_spec, x_next_spec] if prefetch else [x_spec]) + [
        pl.BlockSpec((n_seq, N_MOD, D_MODEL), lambda i, j: (i, 0, 0)),
        pl.BlockSpec((n_seq, GLA_HEADS, GLA_DK, GLA_DV), lambda i, j: (i, 0, 0, 0)),
        pl.BlockSpec((n_seq, CONV_W - 1, CONV_DIM), lambda i, j: (i, 0, 0)),
        _const_spec(n1g.shape), _const_spec(win.shape), _const_spec(walpha.shape),
        _const_spec(balpha.shape), _const_spec(gng.shape), _const_spec(wgla.shape),
        _const_spec(convw.shape), _const_spec(convb.shape), _const_spec(wconv.shape),
        _const_spec(wo.shape),
    ] + cast_specs
    out_specs = [
        pl.BlockSpec((n_seq, tbl, D_MODEL), lambda i, j: (i, j, 0)),
        pl.BlockSpec((n_seq, GLA_HEADS, GLA_DK, GLA_DV), lambda i, j: (i, 0, 0, 0)),
        pl.BlockSpec((n_seq, CONV_W - 1, CONV_DIM), lambda i, j: (i, 0, 0)),
    ] + cast_specs
    out_shape = [
        jax.ShapeDtypeStruct(x.shape, F32),
        jax.ShapeDtypeStruct((n_streams, GLA_HEADS, GLA_DK, GLA_DV), F32),
        jax.ShapeDtypeStruct((n_streams, CONV_W - 1, CONV_DIM), F32),
    ] + [jax.ShapeDtypeStruct(w.shape, BF16) for w in cast]
    scratch = [
        pltpu.VMEM((2 if prefetch else 1, rows, D_MODEL), BF16),
        pltpu.VMEM((rows, GLA_KEY), F32),
        pltpu.VMEM((rows, GLA_KEY), F32),
        pltpu.VMEM((rows, GLA_VAL), BF16),
        pltpu.VMEM((rows, GLA_VAL), BF16),
        pltpu.VMEM((rows, GLA_KEY), F32),
        pltpu.VMEM((n_seq, tbl + SUBLANES, CONV_DIM), F32),
        pltpu.VMEM((rows, D_MODEL), BF16),
        pltpu.VMEM((rows, D_MODEL), BF16),
        pltpu.VMEM((rows, GLA_VAL), F32),
        pltpu.VMEM((rows, GLA_VAL), BF16),
        pltpu.VMEM((n_seq, GLA_HEADS, GLA_DV, GLA_DK), F32),
    ]
    return pl.pallas_call(
        kern, grid=grid, in_specs=in_specs, out_specs=out_specs, out_shape=out_shape,
        scratch_shapes=scratch,
        compiler_params=pltpu.CompilerParams(
            dimension_semantics=("arbitrary", "arbitrary"), vmem_limit_bytes=VMEM_LIMIT),
        name=name,
    )(*((x, x) if prefetch else (x,)), mod, s0, cprev, n1g, win, walpha, balpha, gng, wgla,
      convw, convb, wconv, wo, *cast)


FF_GROUP = 256


def _ffn_kernel(x_ref, mod_ref, n2g_ref, win_ref, wout_ref, nfg_ref, y_ref, hb_s, act_s,
                *, n_seq, tbl, sub):
    sh2 = mod_ref[:, 3:4, :]
    sc2 = mod_ref[:, 4:5, :]
    g2 = mod_ref[:, 5:6, :]
    rows = n_seq * sub

    def sub_block(t, carry):
        t0 = pl.multiple_of(t * sub, sub)
        r0 = pl.multiple_of(t * rows, rows)
        x3 = x_ref[:, pl.ds(t0, sub), :]
        ms = jnp.mean(x3 * x3, axis=-1, keepdims=True)
        h3 = x3 * lax.rsqrt(ms + EPS) * n2g_ref[...] * (1.0 + sc2) + sh2
        hb_s[pl.ds(r0, rows), :] = h3.reshape(rows, D_MODEL).astype(BF16)
        hb = hb_s[pl.ds(r0, rows), :]
        for c in range(D_FF // FF_GROUP):
            gt = _dot(hb, win_ref[:, c * FF_GROUP:(c + 1) * FF_GROUP])
            up = _dot(hb, win_ref[:, D_FF + c * FF_GROUP:D_FF + (c + 1) * FF_GROUP])
            act_s[pl.ds(r0, rows), c * FF_GROUP:(c + 1) * FF_GROUP] = (
                gt * _sigmoid(gt) * up).astype(BF16)
        f = _dot(act_s[pl.ds(r0, rows), :], wout_ref[...])
        x2 = x3 + g2 * f.reshape(n_seq, sub, D_MODEL)
        ms2 = jnp.mean(x2 * x2, axis=-1, keepdims=True)
        y_ref[:, pl.ds(t0, sub), :] = x2 * lax.rsqrt(ms2 + EPS) * nfg_ref[...]
        return carry

    lax.fori_loop(0, tbl // sub, sub_block, 0)


def _ffn(x, mod, n2g, win, wout, nfg, *, n_seq, tbl, sub, name):
    n_streams, length, _ = x.shape
    assert n_streams % n_seq == 0 and length % tbl == 0
    grid = (n_streams // n_seq, length // tbl)
    rows = n_seq * tbl
    kern = functools.partial(_ffn_kernel, n_seq=n_seq, tbl=tbl, sub=sub)
    return pl.pallas_call(
        kern, grid=grid,
        in_specs=[
            pl.BlockSpec((n_seq, tbl, D_MODEL), lambda i, j: (i, j, 0)),
            pl.BlockSpec((n_seq, N_MOD, D_MODEL), lambda i, j: (i, 0, 0)),
            _const_spec(n2g.shape), _const_spec(win.shape), _const_spec(wout.shape),
            _const_spec(nfg.shape),
        ],
        out_specs=pl.BlockSpec((n_seq, tbl, D_MODEL), lambda i, j: (i, j, 0)),
        out_shape=jax.ShapeDtypeStruct(x.shape, F32),
        scratch_shapes=[pltpu.VMEM((rows, D_MODEL), BF16), pltpu.VMEM((rows, D_FF), BF16)],
        compiler_params=pltpu.CompilerParams(
            dimension_semantics=("arbitrary", "arbitrary"), vmem_limit_bytes=VMEM_LIMIT),
        name=name,
    )(x, mod, n2g, win, wout, nfg)


def kernel(x_prompt, x_sample, c_prompt, c_sample, state_gla, cache_conv, w_mod, b_mod, norm1_g,
           w_in, w_alpha, b_alpha, gla_norm_g, w_gla_out, conv_w, conv_b, w_conv_out, w_o,
           norm2_g, w_ffn_in, w_ffn_out, norm_f_g):
    bp = x_prompt.shape[0]
    bs = x_sample.shape[0]
    depth = w_mod.shape[0]
    assert depth == 1
    l = 0
    c_all = jnp.concatenate([c_prompt, c_sample], axis=0)
    n_c = c_all.shape[0]
    n_pad = -n_c % SUBLANES
    c_all = jnp.concatenate([c_all, jnp.zeros((n_pad, D_MODEL), F32)], axis=0)
    mod = _modulation(c_all, w_mod[l], b_mod[l]).reshape(n_c + n_pad, N_MOD, D_MODEL)
    mod_p, mod_s = mod[:bp], mod[bp:bp + bs]

    assert w_in.shape[1:] == (D_MODEL, IN_DIM)
    win_t = jnp.swapaxes(w_in[l], 0, 1).astype(BF16)
    walpha = w_alpha[l].astype(BF16)
    mix_w = (norm1_g[l].reshape(1, -1), win_t, walpha, b_alpha[l].reshape(1, -1),
             gla_norm_g[l].reshape(1, -1), w_gla_out[l].astype(BF16), conv_w[l],
             conv_b[l].reshape(1, -1), w_conv_out[l].astype(BF16), w_o[l].astype(BF16))

    s0_p = jnp.zeros((bp, GLA_HEADS, GLA_DK, GLA_DV), F32)
    c0_p = jnp.zeros((bp, CONV_W - 1, CONV_DIM), F32)
    x1_p, st_p, cv_p, wffn_in, wffn_out = _mixer(
        x_prompt, mod_p, s0_p, c0_p, mix_w, n_seq=1, tbl=512, chunk=256, name="mixer_prompt",
        cast=(w_ffn_in[l], w_ffn_out[l]))
    ffn_w = (norm2_g[l].reshape(1, -1), wffn_in, wffn_out, norm_f_g.reshape(1, -1))
    x1_s, st_s, cv_s = _mixer(x_sample, mod_s, state_gla[l], cache_conv[l], mix_w,
                              n_seq=8, tbl=x_sample.shape[1], chunk=x_sample.shape[1],
                              name="mixer_sample")
    y_p = _ffn(x1_p, mod_p, *ffn_w, n_seq=1, tbl=1024, sub=256, name="ffn_prompt")
    y_s = _ffn(x1_s, mod_s, *ffn_w, n_seq=bs, tbl=x_sample.shape[1], sub=x_sample.shape[1],
               name="ffn_sample")
    return (y_p, y_s, st_p[None], cv_p[None], st_s[None], cv_s[None])
```

```python
import functools

import jax
import jax.numpy as jnp
from jax import lax
from jax.experimental import pallas as pl
from jax.experimental.pallas import tpu as pltpu

F32 = jnp.float32
BF16 = jnp.bfloat16

D_MODEL = 1024
GLA_HEADS = 4
GLA_DK = 128
GLA_DV = 256
GLA_KEY = GLA_HEADS * GLA_DK
GLA_VAL = GLA_HEADS * GLA_DV
GLA_RANK = 16
GLA_TAU = 16.0
CONV_DIM = D_MODEL
CONV_W = 3
D_FF = 2816
N_MOD = 6
EPS = 1e-6

COL_GROUP = 256
LANES = 128
SUBLANES = 8

OFF_Q = 0
OFF_K = OFF_Q + GLA_KEY
OFF_V = OFF_K + GLA_KEY
OFF_G = OFF_V + GLA_VAL
OFF_A = OFF_G + GLA_VAL
OFF_CB = OFF_A + GLA_RANK
OFF_CC = OFF_CB + CONV_DIM
OFF_CH = OFF_CC + CONV_DIM
OFF_GA = OFF_CH + CONV_DIM
OFF_GB = OFF_GA + D_MODEL
IN_DIM = OFF_GB + D_MODEL

SAFE_LOG_DECAY = 60.0

VMEM_LIMIT = 60 * 1024 * 1024


def _dot(a, b):
    return jnp.dot(a, b, preferred_element_type=F32)


def _dot_nt(a, b):
    return lax.dot_general(a, b, (((1,), (1,)), ((), ())), preferred_element_type=F32)


def _dot_tn(a, b):
    return lax.dot_general(a, b, (((0,), (0,)), ((), ())), preferred_element_type=F32)


def _sigmoid(x):
    return 1.0 / (1.0 + jnp.exp(-x))


def _mod_kernel(c_ref, w_ref, b_ref, o_ref):
    o_ref[...] = _dot(c_ref[...].astype(BF16), w_ref[...].astype(BF16)) + b_ref[...]


def _modulation(c_all, w_mod, b_mod):
    n = c_all.shape[0]
    nblk = N_MOD
    return pl.pallas_call(
        _mod_kernel,
        grid=(nblk,),
        in_specs=[
            pl.BlockSpec((n, D_MODEL), lambda j: (0, 0)),
            pl.BlockSpec((D_MODEL, D_MODEL), lambda j: (0, j)),
            pl.BlockSpec((1, D_MODEL), lambda j: (0, j)),
        ],
        out_specs=pl.BlockSpec((n, D_MODEL), lambda j: (0, j)),
        out_shape=jax.ShapeDtypeStruct((n, N_MOD * D_MODEL), F32),
        name="adaln_mod",
    )(c_all, w_mod, b_mod.reshape(1, -1))


def _mixer_kernel(*refs, n_seq, tbl, chunk, unroll, n_cast):
    (x_ref, mod_ref, s0_ref, cprev_ref, n1g_ref, win_ref, walpha_ref, balpha_ref,
     gng_ref, wgla_ref, convw_ref, convb_ref, wconv_ref, wo_ref) = refs[:14]
    cast_in = refs[14:14 + n_cast]
    x1_ref, sout_ref, cout_ref = refs[14 + n_cast:17 + n_cast]
    cast_out = refs[17 + n_cast:17 + 2 * n_cast]
    (hb_s, q_s, k_s, v_s, sg_s, b_s, ubuf_s, ub_s, yb_s, o_s, og_s,
     st_s) = refs[17 + 2 * n_cast:]
    rows = n_seq * tbl
    nchunk = tbl // chunk
    j = pl.program_id(1)
    nj = pl.num_programs(1)

    @pl.when(j == 0)
    def _init():
        for s in range(n_seq):
            for h in range(GLA_HEADS):
                st_s[s, h] = s0_ref[s, h].T
        ubuf_s[:, SUBLANES - 2:SUBLANES, :] = cprev_ref[...]

    sh1 = mod_ref[:, 0:1, :]
    sc1 = mod_ref[:, 1:2, :]
    g1 = mod_ref[:, 2:3, :]
    x3 = x_ref[...]
    ms = jnp.mean(x3 * x3, axis=-1, keepdims=True)
    h3 = x3 * lax.rsqrt(ms + EPS) * n1g_ref[...] * (1.0 + sc1) + sh1
    hb_s[...] = h3.reshape(rows, D_MODEL).astype(BF16)

    def proj(off, c, n=COL_GROUP):
        return _dot_nt(hb_s[...], win_ref[off + c * n:off + (c + 1) * n, :])

    def cols(c):
        return slice(c * COL_GROUP, (c + 1) * COL_GROUP)

    za = proj(OFF_A, 0, GLA_RANK)
    for c in range(GLA_KEY // COL_GROUP):
        q_s[:, cols(c)] = proj(OFF_Q, c) * (GLA_DK ** -0.5)
    xa = _dot(za.astype(BF16), walpha_ref[...]) + balpha_ref[...]
    la = (jnp.minimum(xa, 0.0) - jnp.log(1.0 + jnp.exp(-jnp.abs(xa)))) * (1.0 / GLA_TAU)
    for c in range(GLA_KEY // COL_GROUP):
        k_s[:, cols(c)] = proj(OFF_K, c)

    ri = lax.broadcasted_iota(jnp.int32, (chunk, chunk), 0)
    ci = lax.broadcasted_iota(jnp.int32, (chunk, chunk), 1)
    tril = ri >= ci
    tmat = tril.astype(BF16)
    bmin = jnp.zeros((1, GLA_KEY), F32)
    for idx in range(n_seq * nchunk):
        la_c = la[idx * chunk:(idx + 1) * chunk, :]
        hi = la_c.astype(BF16)
        lo = (la_c - hi.astype(F32)).astype(BF16)
        b_c = _dot(tmat, hi) + _dot(tmat, lo)
        b_s[idx * chunk:(idx + 1) * chunk, :] = b_c
        bmin = jnp.minimum(bmin, b_c[chunk - 1:chunk, :])
    safe = jnp.min(bmin) > -SAFE_LOG_DECAY

    for c in range(GLA_VAL // COL_GROUP):
        v_s[:, cols(c)] = proj(OFF_V, c).astype(BF16)
        g = proj(OFF_G, c)
        sg_s[:, cols(c)] = (g * _sigmoid(g)).astype(BF16)

    gng = gng_ref[...]
    n_idx = n_seq * nchunk

    def loop(body):
        if unroll:
            for idx in range(n_idx):
                body(idx * chunk, idx // nchunk)
        else:
            def fbody(idx, carry):
                body(pl.multiple_of(idx * chunk, chunk), idx // nchunk)
                return carry
            lax.fori_loop(0, n_idx, fbody, 0)

    def intra_fast(r0, s):
        q = q_s[pl.ds(r0, chunk), :]
        k = k_s[pl.ds(r0, chunk), :]
        b = b_s[pl.ds(r0, chunk), :]
        v = v_s[pl.ds(r0, chunk), :]
        bmid = b[chunk // 2 - 1:chunk // 2, :]
        qs = (q * jnp.exp(b - bmid)).astype(BF16)
        ks = (k * jnp.exp(bmid - b)).astype(BF16)
        for h in range(GLA_HEADS):
            ksl = slice(h * GLA_DK, (h + 1) * GLA_DK)
            vsl = slice(h * GLA_DV, (h + 1) * GLA_DV)
            a = jnp.where(tril, _dot_nt(qs[:, ksl], ks[:, ksl]), 0.0)
            o_s[pl.ds(r0, chunk), vsl] = _dot(a.astype(BF16), v[:, vsl])

    def conv_group(c):
        u = proj(OFF_CC, c) * proj(OFF_CH, c)
        ubuf_s[:, SUBLANES:, cols(c)] = u.reshape(n_seq, tbl, COL_GROUP)
        conv = (ubuf_s[:, SUBLANES - 2:SUBLANES - 2 + tbl, cols(c)] * convw_ref[0:1, cols(c)]
                + ubuf_s[:, SUBLANES - 1:SUBLANES - 1 + tbl, cols(c)] * convw_ref[1:2, cols(c)]
                + u.reshape(n_seq, tbl, COL_GROUP) * convw_ref[2:3, cols(c)]
                + convb_ref[:, cols(c)])
        cb = proj(OFF_CB, c)
        ub_s[:, cols(c)] = (cb.reshape(n_seq, tbl, COL_GROUP) * conv).reshape(
            rows, COL_GROUP).astype(BF16)

    loop(intra_fast)

    @pl.when(jnp.logical_not(safe))
    def _pairwise():
        hr = lax.broadcasted_iota(jnp.int32, (GLA_KEY, GLA_VAL), 0) // GLA_DK
        hc = lax.broadcasted_iota(jnp.int32, (GLA_KEY, GLA_VAL), 1) // GLA_DV
        headsel = (hr == hc).astype(BF16)
        rowid = lax.broadcasted_iota(jnp.int32, (chunk, 1), 0)

        def cbody(idx, carry):
            r0 = pl.multiple_of(idx * chunk, chunk)
            q = q_s[pl.ds(r0, chunk), :]
            k = k_s[pl.ds(r0, chunk), :]
            b = b_s[pl.ds(r0, chunk), :]
            vf = v_s[pl.ds(r0, chunk), :].astype(F32)

            def jbody(jj, acc):
                pick = rowid == jj
                kj = jnp.sum(jnp.where(pick, k, 0.0), axis=0, keepdims=True)
                bj = jnp.sum(jnp.where(pick, b, 0.0), axis=0, keepdims=True)
                vj = jnp.sum(jnp.where(pick, vf, 0.0), axis=0, keepdims=True)
                dm = q * kj * jnp.exp(jnp.minimum(b - bj, 0.0))
                dm = jnp.where(rowid >= jj, dm, 0.0)
                return acc + _dot(dm.astype(BF16), headsel) * vj

            o_s[pl.ds(r0, chunk), :] = lax.fori_loop(
                0, chunk, jbody, jnp.zeros((chunk, GLA_VAL), F32))
            return carry

        lax.fori_loop(0, n_idx, cbody, 0)

    def state_step(r0, s):
        q = q_s[pl.ds(r0, chunk), :]
        k = k_s[pl.ds(r0, chunk), :]
        b = b_s[pl.ds(r0, chunk), :]
        v = v_s[pl.ds(r0, chunk), :]
        blast = b[chunk - 1:chunk, :]
        qb = (q * jnp.exp(b)).astype(BF16)
        kb = (k * jnp.exp(blast - b)).astype(BF16)
        dec = jnp.exp(blast)
        for h in range(GLA_HEADS):
            ksl = slice(h * GLA_DK, (h + 1) * GLA_DK)
            vsl = slice(h * GLA_DV, (h + 1) * GLA_DV)
            st = st_s[s, h]
            o_h = o_s[pl.ds(r0, chunk), vsl] + _dot_nt(qb[:, ksl], st.astype(BF16))
            st_s[s, h] = dec[:, ksl] * st + _dot_tn(v[:, vsl], kb[:, ksl])
            o_h = o_h * lax.rsqrt(jnp.mean(o_h * o_h, axis=-1, keepdims=True) + EPS) * gng
            og_s[pl.ds(r0, chunk), vsl] = (
                o_h * sg_s[pl.ds(r0, chunk), vsl].astype(F32)).astype(BF16)

    loop(state_step)

    for c in range(CONV_DIM // COL_GROUP):
        conv_group(c)
    tail = ubuf_s[:, tbl + SUBLANES - 2:tbl + SUBLANES, :]
    ubuf_s[:, SUBLANES - 2:SUBLANES, :] = tail
    cout_ref[...] = tail
    for c in range(D_MODEL // COL_GROUP):
        yb = _dot(ub_s[...], wconv_ref[:, cols(c)])
        yb_s[:, cols(c)] = (_sigmoid(proj(OFF_GB, c)) * yb).astype(BF16)

    for src, dst in zip(cast_in, cast_out):
        dst[...] = src[...].astype(BF16)

    for c in range(D_MODEL // COL_GROUP):
        ya = _dot(og_s[...], wgla_ref[:, cols(c)])
        merged = _sigmoid(proj(OFF_GA, c)) * ya + yb_s[:, cols(c)].astype(F32)
        ub_s[:, cols(c)] = merged.astype(BF16)
    for c in range(D_MODEL // COL_GROUP):
        m = _dot(ub_s[...], wo_ref[:, cols(c)])
        x1_ref[:, :, cols(c)] = (x_ref[:, :, cols(c)]
                                 + g1[:, :, cols(c)] * m.reshape(n_seq, tbl, COL_GROUP))

    @pl.when(j == nj - 1)
    def _final_state():
        for s in range(n_seq):
            for h in range(GLA_HEADS):
                sout_ref[s, h] = st_s[s, h].T


def _const_spec(shape):
    nd = len(shape)
    return pl.BlockSpec(shape, lambda i, j: (0,) * nd, pipeline_mode=pl.Buffered(1))


BF16_SUBLANES = 16


def _cast_blocking(n_rows, n_steps):
    for n_blocks in range(n_steps, 0, -1):
        if (n_steps % n_blocks == 0 and n_rows % n_blocks == 0
                and (n_rows // n_blocks) % BF16_SUBLANES == 0):
            return n_rows // n_blocks, n_steps // n_blocks
    raise ValueError((n_rows, n_steps))


def _mixer(x, mod, s0, cprev, weights, *, n_seq, tbl, chunk, name, cast=()):
    n_streams, length, _ = x.shape
    assert n_streams % n_seq == 0 and length % tbl == 0
    grid = (n_streams // n_seq, length // tbl)
    rows = n_seq * tbl
    (n1g, win, walpha, balpha, gng, wgla, convw, convb, wconv, wo) = weights
    kern = functools.partial(_mixer_kernel, n_seq=n_seq, tbl=tbl, chunk=chunk,
                             unroll=True, n_cast=len(cast))
    cast_specs = []
    for w in cast:
        rb, steps = _cast_blocking(w.shape[0], grid[0] * grid[1])
        cast_specs.append(pl.BlockSpec(
            (rb, w.shape[1]), lambda i, j, steps=steps: ((i * grid[1] + j) // steps, 0)))
    in_specs = [
        pl.BlockSpec((n_seq, tbl, D_MODEL), lambda i, j: (i, j, 0)),
        pl.BlockSpec((n_seq, N_MOD, D_MODEL), lambda i, j: (i, 0, 0)),
        pl.BlockSpec((n_seq, GLA_HEADS, GLA_DK, GLA_DV), lambda i, j: (i, 0, 0, 0)),
        pl.BlockSpec((n_seq, CONV_W - 1, CONV_DIM), lambda i, j: (i, 0, 0)),
        _const_spec(n1g.shape), _const_spec(win.shape), _const_spec(walpha.shape),
        _const_spec(balpha.shape), _const_spec(gng.shape), _const_spec(wgla.shape),
        _const_spec(convw.shape), _const_spec(convb.shape), _const_spec(wconv.shape),
        _const_spec(wo.shape),
    ] + cast_specs
    out_specs = [
        pl.BlockSpec((n_seq, tbl, D_MODEL), lambda i, j: (i, j, 0)),
        pl.BlockSpec((n_seq, GLA_HEADS, GLA_DK, GLA_DV), lambda i, j: (i, 0, 0, 0)),
        pl.BlockSpec((n_seq, CONV_W - 1, CONV_DIM), lambda i, j: (i, 0, 0)),
    ] + cast_specs
    out_shape = [
        jax.ShapeDtypeStruct(x.shape, F32),
        jax.ShapeDtypeStruct((n_streams, GLA_HEADS, GLA_DK, GLA_DV), F32),
        jax.ShapeDtypeStruct((n_streams, CONV_W - 1, CONV_DIM), F32),
    ] + [jax.ShapeDtypeStruct(w.shape, BF16) for w in cast]
    scratch = [
        pltpu.VMEM((rows, D_MODEL), BF16),
        pltpu.VMEM((rows, GLA_KEY), F32),
        pltpu.VMEM((rows, GLA_KEY), F32),
        pltpu.VMEM((rows, GLA_VAL), BF16),
        pltpu.VMEM((rows, GLA_VAL), BF16),
        pltpu.VMEM((rows, GLA_KEY), F32),
        pltpu.VMEM((n_seq, tbl + SUBLANES, CONV_DIM), F32),
        pltpu.VMEM((rows, D_MODEL), BF16),
        pltpu.VMEM((rows, D_MODEL), BF16),
        pltpu.VMEM((rows, GLA_VAL), F32),
        pltpu.VMEM((rows, GLA_VAL), BF16),
        pltpu.VMEM((n_seq, GLA_HEADS, GLA_DV, GLA_DK), F32),
    ]
    return pl.pallas_call(
        kern, grid=grid, in_specs=in_specs, out_specs=out_specs, out_shape=out_shape,
        scratch_shapes=scratch,
        compiler_params=pltpu.CompilerParams(
            dimension_semantics=("arbitrary", "arbitrary"), vmem_limit_bytes=VMEM_LIMIT),
        name=name,
    )(x, mod, s0, cprev, n1g, win, walpha, balpha, gng, wgla, convw, convb, wconv, wo, *cast)


FF_GROUP = 256


def _ffn_kernel(x_ref, mod_ref, n2g_ref, win_ref, wout_ref, nfg_ref, y_ref, hb_s, act_s,
                *, n_seq, tbl, sub):
    sh2 = mod_ref[:, 3:4, :]
    sc2 = mod_ref[:, 4:5, :]
    g2 = mod_ref[:, 5:6, :]
    for t0 in range(0, tbl, sub):
        rows = n_seq * sub
        r0 = (t0 // sub) * rows
        x3 = x_ref[:, t0:t0 + sub, :]
        ms = jnp.mean(x3 * x3, axis=-1, keepdims=True)
        h3 = x3 * lax.rsqrt(ms + EPS) * n2g_ref[...] * (1.0 + sc2) + sh2
        hb_s[r0:r0 + rows, :] = h3.reshape(rows, D_MODEL).astype(BF16)
        hb = hb_s[r0:r0 + rows, :]
        for c in range(D_FF // FF_GROUP):
            gt = _dot(hb, win_ref[:, c * FF_GROUP:(c + 1) * FF_GROUP])
            up = _dot(hb, win_ref[:, D_FF + c * FF_GROUP:D_FF + (c + 1) * FF_GROUP])
            act_s[r0:r0 + rows, c * FF_GROUP:(c + 1) * FF_GROUP] = (
                gt * _sigmoid(gt) * up).astype(BF16)
        f = _dot(act_s[r0:r0 + rows, :], wout_ref[...])
        x2 = x3 + g2 * f.reshape(n_seq, sub, D_MODEL)
        ms2 = jnp.mean(x2 * x2, axis=-1, keepdims=True)
        y_ref[:, t0:t0 + sub, :] = x2 * lax.rsqrt(ms2 + EPS) * nfg_ref[...]


def _ffn(x, mod, n2g, win, wout, nfg, *, n_seq, tbl, sub, name):
    n_streams, length, _ = x.shape
    assert n_streams % n_seq == 0 and length % tbl == 0
    grid = (n_streams // n_seq, length // tbl)
    rows = n_seq * tbl
    kern = functools.partial(_ffn_kernel, n_seq=n_seq, tbl=tbl, sub=sub)
    return pl.pallas_call(
        kern, grid=grid,
        in_specs=[
            pl.BlockSpec((n_seq, tbl, D_MODEL), lambda i, j: (i, j, 0)),
            pl.BlockSpec((n_seq, N_MOD, D_MODEL), lambda i, j: (i, 0, 0)),
            _const_spec(n2g.shape), _const_spec(win.shape), _const_spec(wout.shape),
            _const_spec(nfg.shape),
        ],
        out_specs=pl.BlockSpec((n_seq, tbl, D_MODEL), lambda i, j: (i, j, 0)),
        out_shape=jax.ShapeDtypeStruct(x.shape, F32),
        scratch_shapes=[pltpu.VMEM((rows, D_MODEL), BF16), pltpu.VMEM((rows, D_FF), BF16)],
        compiler_params=pltpu.CompilerParams(
            dimension_semantics=("arbitrary", "arbitrary"), vmem_limit_bytes=VMEM_LIMIT),
        name=name,
    )(x, mod, n2g, win, wout, nfg)


def kernel(x_prompt, x_sample, c_prompt, c_sample, state_gla, cache_conv, w_mod, b_mod, norm1_g,
           w_in, w_alpha, b_alpha, gla_norm_g, w_gla_out, conv_w, conv_b, w_conv_out, w_o,
           norm2_g, w_ffn_in, w_ffn_out, norm_f_g):
    bp = x_prompt.shape[0]
    bs = x_sample.shape[0]
    depth = w_mod.shape[0]
    assert depth == 1
    l = 0
    c_all = jnp.concatenate([c_prompt, c_sample], axis=0)
    n_c = c_all.shape[0]
    n_pad = -n_c % SUBLANES
    c_all = jnp.concatenate([c_all, jnp.zeros((n_pad, D_MODEL), F32)], axis=0)
    mod = _modulation(c_all, w_mod[l], b_mod[l]).reshape(n_c + n_pad, N_MOD, D_MODEL)
    mod_p, mod_s = mod[:bp], mod[bp:bp + bs]

    assert w_in.shape[1:] == (D_MODEL, IN_DIM)
    win_t = jnp.swapaxes(w_in[l], 0, 1).astype(BF16)
    walpha = w_alpha[l].astype(BF16)
    mix_w = (norm1_g[l].reshape(1, -1), win_t, walpha, b_alpha[l].reshape(1, -1),
             gla_norm_g[l].reshape(1, -1), w_gla_out[l].astype(BF16), conv_w[l],
             conv_b[l].reshape(1, -1), w_conv_out[l].astype(BF16), w_o[l].astype(BF16))

    s0_p = jnp.zeros((bp, GLA_HEADS, GLA_DK, GLA_DV), F32)
    c0_p = jnp.zeros((bp, CONV_W - 1, CONV_DIM), F32)
    x1_p, st_p, cv_p, wffn_in, wffn_out = _mixer(
        x_prompt, mod_p, s0_p, c0_p, mix_w, n_seq=1, tbl=512, chunk=256, name="mixer_prompt",
        cast=(w_ffn_in[l], w_ffn_out[l]))
    ffn_w = (norm2_g[l].reshape(1, -1), wffn_in, wffn_out, norm_f_g.reshape(1, -1))
    x1_s, st_s, cv_s = _mixer(x_sample, mod_s, state_gla[l], cache_conv[l], mix_w,
                              n_seq=8, tbl=x_sample.shape[1], chunk=x_sample.shape[1],
                              name="mixer_sample")
    y_p = _ffn(x1_p, mod_p, *ffn_w, n_seq=1, tbl=1024, sub=256, name="ffn_prompt")
    y_s = _ffn(x1_s, mod_s, *ffn_w, n_seq=bs, tbl=x_sample.shape[1], sub=x_sample.shape[1],
               name="ffn_sample")
    return (y_p, y_s, st_p[None], cv_p[None], st_s[None], cv_s[None])
```

```python
import functools

import jax
import jax.numpy as jnp
from jax import lax
from jax.experimental import pallas as pl
from jax.experimental.pallas import tpu as pltpu

F32 = jnp.float32
BF16 = jnp.bfloat16

D_MODEL = 1024
GLA_HEADS = 4
GLA_DK = 128
GLA_DV = 256
GLA_KEY = GLA_HEADS * GLA_DK
GLA_VAL = GLA_HEADS * GLA_DV
GLA_RANK = 16
GLA_TAU = 16.0
CONV_DIM = D_MODEL
CONV_W = 3
D_FF = 2816
N_MOD = 6
EPS = 1e-6

COL_GROUP = 256
LANES = 128
SUBLANES = 8

OFF_Q = 0
OFF_K = OFF_Q + GLA_KEY
OFF_V = OFF_K + GLA_KEY
OFF_G = OFF_V + GLA_VAL
OFF_A = OFF_G + GLA_VAL
OFF_CB = OFF_A + GLA_RANK
OFF_CC = OFF_CB + CONV_DIM
OFF_CH = OFF_CC + CONV_DIM
OFF_GA = OFF_CH + CONV_DIM
OFF_GB = OFF_GA + D_MODEL
IN_DIM = OFF_GB + D_MODEL

SAFE_LOG_DECAY = 60.0

VMEM_LIMIT = 60 * 1024 * 1024


def _dot(a, b):
    return jnp.dot(a, b, preferred_element_type=F32)


def _dot_nt(a, b):
    return lax.dot_general(a, b, (((1,), (1,)), ((), ())), preferred_element_type=F32)


def _dot_tn(a, b):
    return lax.dot_general(a, b, (((0,), (0,)), ((), ())), preferred_element_type=F32)


def _sigmoid(x):
    return 1.0 / (1.0 + jnp.exp(-x))


def _mod_kernel(cp_ref, cs_ref, w_ref, b_ref, op_ref, os_ref):
    j = pl.program_id(0)
    w = w_ref[...].astype(BF16)
    vp = _dot(cp_ref[...].astype(BF16), w) + b_ref[...]
    vs = _dot(cs_ref[...].astype(BF16), w) + b_ref[...]
    for c in range(N_MOD):
        @pl.when(j == c)
        def _store():
            op_ref[:, c, :] = vp
            os_ref[:, c, :] = vs


def _modulation(c_prompt, c_sample, w_mod, b_mod):
    bp, bs = c_prompt.shape[0], c_sample.shape[0]
    return pl.pallas_call(
        _mod_kernel,
        grid=(N_MOD,),
        in_specs=[
            pl.BlockSpec((bp, D_MODEL), lambda j: (0, 0)),
            pl.BlockSpec((bs, D_MODEL), lambda j: (0, 0)),
            pl.BlockSpec((D_MODEL, D_MODEL), lambda j: (0, j)),
            pl.BlockSpec((1, D_MODEL), lambda j: (0, j)),
        ],
        out_specs=[
            pl.BlockSpec((bp, N_MOD, D_MODEL), lambda j: (0, 0, 0)),
            pl.BlockSpec((bs, N_MOD, D_MODEL), lambda j: (0, 0, 0)),
        ],
        out_shape=[jax.ShapeDtypeStruct((bp, N_MOD, D_MODEL), F32),
                   jax.ShapeDtypeStruct((bs, N_MOD, D_MODEL), F32)],
        compiler_params=pltpu.CompilerParams(dimension_semantics=("arbitrary",)),
        name="adaln_mod",
    )(c_prompt, c_sample, w_mod, b_mod.reshape(1, -1))


N_MIXER_IN = 12
N_MIXER_OUT = 3


def _mixer_kernel(*refs, n_seq, tbl, chunk, n_cast, carried):
    (x_ref, mod_ref, n1g_ref, win_ref, walpha_ref, balpha_ref,
     gng_ref, wgla_ref, convw_ref, convb_ref, wconv_ref, wo_ref) = refs[:N_MIXER_IN]
    refs = refs[N_MIXER_IN:]
    if carried:
        (s0_ref, cprev_ref), refs = refs[:2], refs[2:]
    cast_in, refs = refs[:n_cast], refs[n_cast:]
    (x1_ref, sout_ref, cout_ref), refs = refs[:N_MIXER_OUT], refs[N_MIXER_OUT:]
    cast_out, refs = refs[:n_cast], refs[n_cast:]
    (hb_s, q_s, k_s, v_s, sg_s, b_s, ubuf_s, ub_s, yb_s, o_s, og_s, st_s) = refs
    rows = n_seq * tbl
    nchunk = tbl // chunk
    j = pl.program_id(1)
    nj = pl.num_programs(1)

    @pl.when(j == 0)
    def _init():
        for s in range(n_seq):
            for h in range(GLA_HEADS):
                st_s[s, h] = s0_ref[s, h].T if carried else jnp.zeros((GLA_DV, GLA_DK), F32)
        ubuf_s[:, SUBLANES - 2:SUBLANES, :] = (
            cprev_ref[...] if carried else jnp.zeros((n_seq, CONV_W - 1, CONV_DIM), F32))

    sh1 = mod_ref[:, 0:1, :]
    sc1 = mod_ref[:, 1:2, :]
    g1 = mod_ref[:, 2:3, :]
    x3 = x_ref[...]
    ms = jnp.mean(x3 * x3, axis=-1, keepdims=True)
    h3 = x3 * lax.rsqrt(ms + EPS) * n1g_ref[...] * (1.0 + sc1) + sh1
    hb_s[...] = h3.reshape(rows, D_MODEL).astype(BF16)

    def proj(off, c, n=COL_GROUP):
        return _dot_nt(hb_s[...], win_ref[off + c * n:off + (c + 1) * n, :])

    def cols(c):
        return slice(c * COL_GROUP, (c + 1) * COL_GROUP)

    za = proj(OFF_A, 0, GLA_RANK)
    for c in range(GLA_KEY // COL_GROUP):
        q_s[:, cols(c)] = proj(OFF_Q, c) * (GLA_DK ** -0.5)
    xa = _dot(za.astype(BF16), walpha_ref[...]) + balpha_ref[...]
    la = (jnp.minimum(xa, 0.0) - jnp.log(1.0 + jnp.exp(-jnp.abs(xa)))) * (1.0 / GLA_TAU)
    for c in range(GLA_KEY // COL_GROUP):
        k_s[:, cols(c)] = proj(OFF_K, c)

    ri = lax.broadcasted_iota(jnp.int32, (chunk, chunk), 0)
    ci = lax.broadcasted_iota(jnp.int32, (chunk, chunk), 1)
    tril = ri >= ci
    tmat = tril.astype(BF16)
    bmin = jnp.zeros((1, GLA_KEY), F32)
    for idx in range(n_seq * nchunk):
        la_c = la[idx * chunk:(idx + 1) * chunk, :]
        hi = la_c.astype(BF16)
        lo = (la_c - hi.astype(F32)).astype(BF16)
        b_c = _dot(tmat, hi) + _dot(tmat, lo)
        b_s[idx * chunk:(idx + 1) * chunk, :] = b_c
        bmin = jnp.minimum(bmin, b_c[chunk - 1:chunk, :])
    safe = jnp.min(bmin) > -SAFE_LOG_DECAY

    for c in range(GLA_VAL // COL_GROUP):
        v_s[:, cols(c)] = proj(OFF_V, c).astype(BF16)
        g = proj(OFF_G, c)
        sg_s[:, cols(c)] = (g * _sigmoid(g)).astype(BF16)

    gng = gng_ref[...]
    n_idx = n_seq * nchunk

    def loop(body):
        for idx in range(n_idx):
            body(idx * chunk, idx // nchunk)

    def intra_fast(r0, s):
        q = q_s[pl.ds(r0, chunk), :]
        k = k_s[pl.ds(r0, chunk), :]
        b = b_s[pl.ds(r0, chunk), :]
        v = v_s[pl.ds(r0, chunk), :]
        bmid = b[chunk // 2 - 1:chunk // 2, :]
        qs = (q * jnp.exp(b - bmid)).astype(BF16)
        ks = (k * jnp.exp(bmid - b)).astype(BF16)
        for h in range(GLA_HEADS):
            ksl = slice(h * GLA_DK, (h + 1) * GLA_DK)
            vsl = slice(h * GLA_DV, (h + 1) * GLA_DV)
            a = jnp.where(tril, _dot_nt(qs[:, ksl], ks[:, ksl]), 0.0)
            o_s[pl.ds(r0, chunk), vsl] = _dot(a.astype(BF16), v[:, vsl])

    def conv_group(c):
        u = proj(OFF_CC, c) * proj(OFF_CH, c)
        ubuf_s[:, SUBLANES:, cols(c)] = u.reshape(n_seq, tbl, COL_GROUP)
        conv = (ubuf_s[:, SUBLANES - 2:SUBLANES - 2 + tbl, cols(c)] * convw_ref[0, 0:1, cols(c)]
                + ubuf_s[:, SUBLANES - 1:SUBLANES - 1 + tbl, cols(c)] * convw_ref[0, 1:2, cols(c)]
                + u.reshape(n_seq, tbl, COL_GROUP) * convw_ref[0, 2:3, cols(c)]
                + convb_ref[:, cols(c)])
        cb = proj(OFF_CB, c)
        ub_s[:, cols(c)] = (cb.reshape(n_seq, tbl, COL_GROUP) * conv).reshape(
            rows, COL_GROUP).astype(BF16)

    loop(intra_fast)

    @pl.when(jnp.logical_not(safe))
    def _pairwise():
        hr = lax.broadcasted_iota(jnp.int32, (GLA_KEY, GLA_VAL), 0) // GLA_DK
        hc = lax.broadcasted_iota(jnp.int32, (GLA_KEY, GLA_VAL), 1) // GLA_DV
        headsel = (hr == hc).astype(BF16)
        rowid = lax.broadcasted_iota(jnp.int32, (chunk, 1), 0)

        def cbody(idx, carry):
            r0 = pl.multiple_of(idx * chunk, chunk)
            q = q_s[pl.ds(r0, chunk), :]
            k = k_s[pl.ds(r0, chunk), :]
            b = b_s[pl.ds(r0, chunk), :]
            vf = v_s[pl.ds(r0, chunk), :].astype(F32)

            def jbody(jj, acc):
                pick = rowid == jj
                kj = jnp.sum(jnp.where(pick, k, 0.0), axis=0, keepdims=True)
                bj = jnp.sum(jnp.where(pick, b, 0.0), axis=0, keepdims=True)
                vj = jnp.sum(jnp.where(pick, vf, 0.0), axis=0, keepdims=True)
                dm = q * kj * jnp.exp(jnp.minimum(b - bj, 0.0))
                dm = jnp.where(rowid >= jj, dm, 0.0)
                return acc + _dot(dm.astype(BF16), headsel) * vj

            o_s[pl.ds(r0, chunk), :] = lax.fori_loop(
                0, chunk, jbody, jnp.zeros((chunk, GLA_VAL), F32))
            return carry

        lax.fori_loop(0, n_idx, cbody, 0)

    def state_step(r0, s):
        q = q_s[pl.ds(r0, chunk), :]
        k = k_s[pl.ds(r0, chunk), :]
        b = b_s[pl.ds(r0, chunk), :]
        v = v_s[pl.ds(r0, chunk), :]
        blast = b[chunk - 1:chunk, :]
        qb = (q * jnp.exp(b)).astype(BF16)
        kb = (k * jnp.exp(blast - b)).astype(BF16)
        dec = jnp.exp(blast)
        for h in range(GLA_HEADS):
            ksl = slice(h * GLA_DK, (h + 1) * GLA_DK)
            vsl = slice(h * GLA_DV, (h + 1) * GLA_DV)
            st = st_s[s, h]
            o_h = o_s[pl.ds(r0, chunk), vsl] + _dot_nt(qb[:, ksl], st.astype(BF16))
            st_s[s, h] = dec[:, ksl] * st + _dot_tn(v[:, vsl], kb[:, ksl])
            o_h = o_h * lax.rsqrt(jnp.mean(o_h * o_h, axis=-1, keepdims=True) + EPS) * gng
            og_s[pl.ds(r0, chunk), vsl] = (
                o_h * sg_s[pl.ds(r0, chunk), vsl].astype(F32)).astype(BF16)

    loop(state_step)

    for c in range(CONV_DIM // COL_GROUP):
        conv_group(c)
    tail = ubuf_s[:, tbl + SUBLANES - 2:tbl + SUBLANES, :]
    ubuf_s[:, SUBLANES - 2:SUBLANES, :] = tail
    cout_ref[...] = tail
    for c in range(D_MODEL // COL_GROUP):
        yb = _dot(ub_s[...], wconv_ref[:, cols(c)])
        yb_s[:, cols(c)] = (_sigmoid(proj(OFF_GB, c)) * yb).astype(BF16)

    for src, dst in zip(cast_in, cast_out):
        dst[...] = src[...].astype(BF16)

    for c in range(D_MODEL // COL_GROUP):
        ya = _dot(og_s[...], wgla_ref[:, cols(c)])
        merged = _sigmoid(proj(OFF_GA, c)) * ya + yb_s[:, cols(c)].astype(F32)
        ub_s[:, cols(c)] = merged.astype(BF16)
    for c in range(D_MODEL // COL_GROUP):
        m = _dot(ub_s[...], wo_ref[:, cols(c)])
        x1_ref[:, :, cols(c)] = (x_ref[:, :, cols(c)]
                                 + g1[:, :, cols(c)] * m.reshape(n_seq, tbl, COL_GROUP))

    @pl.when(j == nj - 1)
    def _final_state():
        for s in range(n_seq):
            for h in range(GLA_HEADS):
                sout_ref[s, h] = st_s[s, h].T


def _const_spec(shape):
    nd = len(shape)
    return pl.BlockSpec(shape, lambda i, j: (0,) * nd, pipeline_mode=pl.Buffered(1))


BF16_SUBLANES = 16


def _cast_blocking(n_rows, n_steps):
    for n_blocks in range(n_steps, 0, -1):
        if (n_steps % n_blocks == 0 and n_rows % n_blocks == 0
                and (n_rows // n_blocks) % BF16_SUBLANES == 0):
            return n_rows // n_blocks, n_steps // n_blocks
    raise ValueError((n_rows, n_steps))


def _mixer(x, mod, weights, *, n_seq, tbl, chunk, name, carry=None, cast=()):
    n_streams, length, _ = x.shape
    assert n_streams % n_seq == 0 and length % tbl == 0
    grid = (n_streams // n_seq, length // tbl)
    rows = n_seq * tbl
    (n1g, win, walpha, balpha, gng, wgla, convw, convb, wconv, wo) = weights
    carried = carry is not None
    kern = functools.partial(_mixer_kernel, n_seq=n_seq, tbl=tbl, chunk=chunk,
                             n_cast=len(cast), carried=carried)
    carry_specs = [
        pl.BlockSpec((n_seq, GLA_HEADS, GLA_DK, GLA_DV), lambda i, j: (i, 0, 0, 0)),
        pl.BlockSpec((n_seq, CONV_W - 1, CONV_DIM), lambda i, j: (i, 0, 0)),
    ] if carried else []
    cast_specs = []
    for w in cast:
        rb, steps = _cast_blocking(w.shape[0], grid[0] * grid[1])
        cast_specs.append(pl.BlockSpec(
            (rb, w.shape[1]), lambda i, j, steps=steps: ((i * grid[1] + j) // steps, 0)))
    in_specs = [
        pl.BlockSpec((n_seq, tbl, D_MODEL), lambda i, j: (i, j, 0)),
        pl.BlockSpec((n_seq, N_MOD, D_MODEL), lambda i, j: (i, 0, 0)),
        _const_spec(n1g.shape), _const_spec(win.shape), _const_spec(walpha.shape),
        _const_spec(balpha.shape), _const_spec(gng.shape), _const_spec(wgla.shape),
        _const_spec(convw.shape), _const_spec(convb.shape), _const_spec(wconv.shape),
        _const_spec(wo.shape),
    ] + carry_specs + cast_specs
    out_specs = [
        pl.BlockSpec((n_seq, tbl, D_MODEL), lambda i, j: (i, j, 0)),
        pl.BlockSpec((n_seq, GLA_HEADS, GLA_DK, GLA_DV), lambda i, j: (i, 0, 0, 0)),
        pl.BlockSpec((n_seq, CONV_W - 1, CONV_DIM), lambda i, j: (i, 0, 0)),
    ] + cast_specs
    out_shape = [
        jax.ShapeDtypeStruct(x.shape, F32),
        jax.ShapeDtypeStruct((n_streams, GLA_HEADS, GLA_DK, GLA_DV), F32),
        jax.ShapeDtypeStruct((n_streams, CONV_W - 1, CONV_DIM), F32),
    ] + [jax.ShapeDtypeStruct(w.shape, BF16) for w in cast]
    scratch = [
        pltpu.VMEM((rows, D_MODEL), BF16),
        pltpu.VMEM((rows, GLA_KEY), F32),
        pltpu.VMEM((rows, GLA_KEY), F32),
        pltpu.VMEM((rows, GLA_VAL), BF16),
        pltpu.VMEM((rows, GLA_VAL), BF16),
        pltpu.VMEM((rows, GLA_KEY), F32),
        pltpu.VMEM((n_seq, tbl + SUBLANES, CONV_DIM), F32),
        pltpu.VMEM((rows, D_MODEL), BF16),
        pltpu.VMEM((rows, D_MODEL), BF16),
        pltpu.VMEM((rows, GLA_VAL), F32),
        pltpu.VMEM((rows, GLA_VAL), BF16),
        pltpu.VMEM((n_seq, GLA_HEADS, GLA_DV, GLA_DK), F32),
    ]
    return pl.pallas_call(
        kern, grid=grid, in_specs=in_specs, out_specs=out_specs, out_shape=out_shape,
        scratch_shapes=scratch,
        compiler_params=pltpu.CompilerParams(
            dimension_semantics=("arbitrary", "arbitrary"), vmem_limit_bytes=VMEM_LIMIT),
        name=name,
    )(x, mod, n1g, win, walpha, balpha, gng, wgla, convw, convb, wconv, wo,
      *(carry or ()), *cast)


FF_GROUP = 256


def _ffn_kernel(x_ref, mod_ref, n2g_ref, win_ref, wout_ref, nfg_ref, y_ref, hb_s, act_s,
                *, n_seq, tbl, sub):
    sh2 = mod_ref[:, 3:4, :]
    sc2 = mod_ref[:, 4:5, :]
    g2 = mod_ref[:, 5:6, :]
    for t0 in range(0, tbl, sub):
        rows = n_seq * sub
        r0 = (t0 // sub) * rows
        x3 = x_ref[:, t0:t0 + sub, :]
        ms = jnp.mean(x3 * x3, axis=-1, keepdims=True)
        h3 = x3 * lax.rsqrt(ms + EPS) * n2g_ref[...] * (1.0 + sc2) + sh2
        hb_s[r0:r0 + rows, :] = h3.reshape(rows, D_MODEL).astype(BF16)
        hb = hb_s[r0:r0 + rows, :]
        for c in range(D_FF // FF_GROUP):
            gt = _dot(hb, win_ref[:, c * FF_GROUP:(c + 1) * FF_GROUP])
            up = _dot(hb, win_ref[:, D_FF + c * FF_GROUP:D_FF + (c + 1) * FF_GROUP])
            act_s[r0:r0 + rows, c * FF_GROUP:(c + 1) * FF_GROUP] = (
                gt * _sigmoid(gt) * up).astype(BF16)
        f = _dot(act_s[r0:r0 + rows, :], wout_ref[...])
        x2 = x3 + g2 * f.reshape(n_seq, sub, D_MODEL)
        ms2 = jnp.mean(x2 * x2, axis=-1, keepdims=True)
        y_ref[:, t0:t0 + sub, :] = x2 * lax.rsqrt(ms2 + EPS) * nfg_ref[...]


def _ffn(x, mod, n2g, win, wout, nfg, *, n_seq, tbl, sub, name):
    n_streams, length, _ = x.shape
    assert n_streams % n_seq == 0 and length % tbl == 0
    grid = (n_streams // n_seq, length // tbl)
    rows = n_seq * tbl
    kern = functools.partial(_ffn_kernel, n_seq=n_seq, tbl=tbl, sub=sub)
    return pl.pallas_call(
        kern, grid=grid,
        in_specs=[
            pl.BlockSpec((n_seq, tbl, D_MODEL), lambda i, j: (i, j, 0)),
            pl.BlockSpec((n_seq, N_MOD, D_MODEL), lambda i, j: (i, 0, 0)),
            _const_spec(n2g.shape), _const_spec(win.shape), _const_spec(wout.shape),
            _const_spec(nfg.shape),
        ],
        out_specs=pl.BlockSpec((n_seq, tbl, D_MODEL), lambda i, j: (i, j, 0)),
        out_shape=jax.ShapeDtypeStruct(x.shape, F32),
        scratch_shapes=[pltpu.VMEM((rows, D_MODEL), BF16), pltpu.VMEM((rows, D_FF), BF16)],
        compiler_params=pltpu.CompilerParams(
            dimension_semantics=("arbitrary", "arbitrary"), vmem_limit_bytes=VMEM_LIMIT),
        name=name,
    )(x, mod, n2g, win, wout, nfg)


def kernel(x_prompt, x_sample, c_prompt, c_sample, state_gla, cache_conv, w_mod, b_mod, norm1_g,
           w_in, w_alpha, b_alpha, gla_norm_g, w_gla_out, conv_w, conv_b, w_conv_out, w_o,
           norm2_g, w_ffn_in, w_ffn_out, norm_f_g):
    bs = x_sample.shape[0]
    depth = w_mod.shape[0]
    assert depth == 1
    l = 0
    mod_p, mod_s = _modulation(c_prompt, c_sample, w_mod[l], b_mod[l])

    assert w_in.shape[1:] == (D_MODEL, IN_DIM)
    win_t = jnp.swapaxes(w_in[l], 0, 1).astype(BF16)
    walpha = w_alpha[l].astype(BF16)
    mix_w = (norm1_g[l].reshape(1, -1), win_t, walpha, b_alpha[l].reshape(1, -1),
             gla_norm_g[l].reshape(1, -1), w_gla_out[l].astype(BF16), conv_w,
             conv_b[l].reshape(1, -1), w_conv_out[l].astype(BF16), w_o[l].astype(BF16))

    x1_p, st_p, cv_p, wffn_in, wffn_out = _mixer(
        x_prompt, mod_p, mix_w, n_seq=1, tbl=512, chunk=256, name="mixer_prompt",
        cast=(w_ffn_in[l], w_ffn_out[l]))
    ffn_w = (norm2_g[l].reshape(1, -1), wffn_in, wffn_out, norm_f_g.reshape(1, -1))
    x1_s, st_s, cv_s = _mixer(x_sample, mod_s, mix_w, carry=(state_gla[l], cache_conv[l]),
                              n_seq=8, tbl=x_sample.shape[1], chunk=x_sample.shape[1],
                              name="mixer_sample")
    y_p = _ffn(x1_p, mod_p, *ffn_w, n_seq=1, tbl=1024, sub=256, name="ffn_prompt")
    y_s = _ffn(x1_s, mod_s, *ffn_w, n_seq=bs, tbl=x_sample.shape[1], sub=x_sample.shape[1],
               name="ffn_sample")
    return (y_p, y_s, st_p[None], cv_p[None], st_s[None], cv_s[None])
```

```python
import functools

import jax
import jax.numpy as jnp
from jax import lax
from jax.experimental import pallas as pl
from jax.experimental.pallas import tpu as pltpu

F32 = jnp.float32
BF16 = jnp.bfloat16

D_MODEL = 1024
GLA_HEADS = 4
GLA_DK = 128
GLA_DV = 256
GLA_KEY = GLA_HEADS * GLA_DK
GLA_VAL = GLA_HEADS * GLA_DV
GLA_RANK = 16
GLA_TAU = 16.0
CONV_DIM = D_MODEL
CONV_W = 3
D_FF = 2816
N_MOD = 6
EPS = 1e-6

COL_GROUP = 256
LANES = 128
SUBLANES = 8

OFF_Q = 0
OFF_K = OFF_Q + GLA_KEY
OFF_V = OFF_K + GLA_KEY
OFF_G = OFF_V + GLA_VAL
OFF_A = OFF_G + GLA_VAL
OFF_CB = OFF_A + GLA_RANK
OFF_CC = OFF_CB + CONV_DIM
OFF_CH = OFF_CC + CONV_DIM
OFF_GA = OFF_CH + CONV_DIM
OFF_GB = OFF_GA + D_MODEL
IN_DIM = OFF_GB + D_MODEL

SAFE_LOG_DECAY = 60.0

VMEM_LIMIT = 60 * 1024 * 1024


def _dot(a, b):
    return jnp.dot(a, b, preferred_element_type=F32)


def _dot_nt(a, b):
    return lax.dot_general(a, b, (((1,), (1,)), ((), ())), preferred_element_type=F32)


def _dot_tn(a, b):
    return lax.dot_general(a, b, (((0,), (0,)), ((), ())), preferred_element_type=F32)


def _sigmoid(x):
    return 1.0 / (1.0 + jnp.exp(-x))


def _mod_kernel(cp_ref, cs_ref, w_ref, b_ref, op_ref, os_ref):
    j = pl.program_id(0)
    w = w_ref[...].astype(BF16)
    vp = _dot(cp_ref[...].astype(BF16), w) + b_ref[...]
    vs = _dot(cs_ref[...].astype(BF16), w) + b_ref[...]
    for c in range(N_MOD):
        @pl.when(j == c)
        def _store():
            op_ref[:, c, :] = vp
            os_ref[:, c, :] = vs


def _modulation(c_prompt, c_sample, w_mod, b_mod):
    bp, bs = c_prompt.shape[0], c_sample.shape[0]
    return pl.pallas_call(
        _mod_kernel,
        grid=(N_MOD,),
        in_specs=[
            pl.BlockSpec((bp, D_MODEL), lambda j: (0, 0)),
            pl.BlockSpec((bs, D_MODEL), lambda j: (0, 0)),
            pl.BlockSpec((D_MODEL, D_MODEL), lambda j: (0, j)),
            pl.BlockSpec((1, D_MODEL), lambda j: (0, j)),
        ],
        out_specs=[
            pl.BlockSpec((bp, N_MOD, D_MODEL), lambda j: (0, 0, 0)),
            pl.BlockSpec((bs, N_MOD, D_MODEL), lambda j: (0, 0, 0)),
        ],
        out_shape=[jax.ShapeDtypeStruct((bp, N_MOD, D_MODEL), F32),
                   jax.ShapeDtypeStruct((bs, N_MOD, D_MODEL), F32)],
        compiler_params=pltpu.CompilerParams(dimension_semantics=("arbitrary",)),
        name="adaln_mod",
    )(c_prompt, c_sample, w_mod, b_mod.reshape(1, -1))


N_MIXER_IN = 12
N_MIXER_OUT = 3


def _mixer_kernel(*refs, n_seq, tbl, chunk, n_cast, carried):
    (x_ref, mod_ref, n1g_ref, win_ref, walpha_ref, balpha_ref,
     gng_ref, wgla_ref, convw_ref, convb_ref, wconv_ref, wo_ref) = refs[:N_MIXER_IN]
    refs = refs[N_MIXER_IN:]
    if carried:
        (s0_ref, cprev_ref), refs = refs[:2], refs[2:]
    cast_in, refs = refs[:n_cast], refs[n_cast:]
    (x1_ref, sout_ref, cout_ref), refs = refs[:N_MIXER_OUT], refs[N_MIXER_OUT:]
    cast_out, refs = refs[:n_cast], refs[n_cast:]
    (hb_s, q_s, k_s, v_s, sg_s, b_s, ubuf_s, ub_s, yb_s, o_s, og_s, st_s) = refs
    rows = n_seq * tbl
    nchunk = tbl // chunk
    j = pl.program_id(1)
    nj = pl.num_programs(1)

    @pl.when(j == 0)
    def _init():
        for s in range(n_seq):
            for h in range(GLA_HEADS):
                st_s[s, h] = s0_ref[s, h].T if carried else jnp.zeros((GLA_DV, GLA_DK), F32)
        ubuf_s[:, SUBLANES - 2:SUBLANES, :] = (
            cprev_ref[...] if carried else jnp.zeros((n_seq, CONV_W - 1, CONV_DIM), F32))

    sh1 = mod_ref[:, 0:1, :]
    sc1 = mod_ref[:, 1:2, :]
    g1 = mod_ref[:, 2:3, :]
    x3 = x_ref[...]
    ms = jnp.mean(x3 * x3, axis=-1, keepdims=True)
    h3 = x3 * lax.rsqrt(ms + EPS) * n1g_ref[...] * (1.0 + sc1) + sh1
    hb_s[...] = h3.reshape(rows, D_MODEL).astype(BF16)

    def proj(off, c, n=COL_GROUP):
        return _dot_nt(hb_s[...], win_ref[off + c * n:off + (c + 1) * n, :])

    def cols(c):
        return slice(c * COL_GROUP, (c + 1) * COL_GROUP)

    za = proj(OFF_A, 0, GLA_RANK)
    for c in range(GLA_KEY // COL_GROUP):
        q_s[:, cols(c)] = proj(OFF_Q, c) * (GLA_DK ** -0.5)
    xa = _dot(za.astype(BF16), walpha_ref[...]) + balpha_ref[...]
    la = (jnp.minimum(xa, 0.0) - jnp.log(1.0 + jnp.exp(-jnp.abs(xa)))) * (1.0 / GLA_TAU)
    for c in range(GLA_KEY // COL_GROUP):
        k_s[:, cols(c)] = proj(OFF_K, c)

    ri = lax.broadcasted_iota(jnp.int32, (chunk, chunk), 0)
    ci = lax.broadcasted_iota(jnp.int32, (chunk, chunk), 1)
    tril = ri >= ci
    tmat = tril.astype(BF16)
    bmin = jnp.zeros((1, GLA_KEY), F32)
    for idx in range(n_seq * nchunk):
        la_c = la[idx * chunk:(idx + 1) * chunk, :]
        hi = la_c.astype(BF16)
        lo = (la_c - hi.astype(F32)).astype(BF16)
        b_c = _dot(tmat, hi) + _dot(tmat, lo)
        b_s[idx * chunk:(idx + 1) * chunk, :] = b_c
        bmin = jnp.minimum(bmin, b_c[chunk - 1:chunk, :])
    safe = jnp.min(bmin) > -SAFE_LOG_DECAY

    for c in range(GLA_VAL // COL_GROUP):
        v_s[:, cols(c)] = proj(OFF_V, c).astype(BF16)
        g = proj(OFF_G, c)
        sg_s[:, cols(c)] = (g * _sigmoid(g)).astype(BF16)

    gng = gng_ref[...]
    n_idx = n_seq * nchunk

    def loop(body):
        for idx in range(n_idx):
            body(idx * chunk, idx // nchunk)

    def intra_fast(r0, s):
        q = q_s[pl.ds(r0, chunk), :]
        k = k_s[pl.ds(r0, chunk), :]
        b = b_s[pl.ds(r0, chunk), :]
        v = v_s[pl.ds(r0, chunk), :]
        bmid = b[chunk // 2 - 1:chunk // 2, :]
        qs = (q * jnp.exp(b - bmid)).astype(BF16)
        ks = (k * jnp.exp(bmid - b)).astype(BF16)
        for h in range(GLA_HEADS):
            ksl = slice(h * GLA_DK, (h + 1) * GLA_DK)
            vsl = slice(h * GLA_DV, (h + 1) * GLA_DV)
            a = jnp.where(tril, _dot_nt(qs[:, ksl], ks[:, ksl]), 0.0)
            o_s[pl.ds(r0, chunk), vsl] = _dot(a.astype(BF16), v[:, vsl])

    def conv_group(c):
        u = proj(OFF_CC, c) * proj(OFF_CH, c)
        ubuf_s[:, SUBLANES:, cols(c)] = u.reshape(n_seq, tbl, COL_GROUP)
        conv = (ubuf_s[:, SUBLANES - 2:SUBLANES - 2 + tbl, cols(c)] * convw_ref[0, 0:1, cols(c)]
                + ubuf_s[:, SUBLANES - 1:SUBLANES - 1 + tbl, cols(c)] * convw_ref[0, 1:2, cols(c)]
                + u.reshape(n_seq, tbl, COL_GROUP) * convw_ref[0, 2:3, cols(c)]
                + convb_ref[:, cols(c)])
        cb = proj(OFF_CB, c)
        ub_s[:, cols(c)] = (cb.reshape(n_seq, tbl, COL_GROUP) * conv).reshape(
            rows, COL_GROUP).astype(BF16)

    loop(intra_fast)

    @pl.when(jnp.logical_not(safe))
    def _pairwise():
        hr = lax.broadcasted_iota(jnp.int32, (GLA_KEY, GLA_VAL), 0) // GLA_DK
        hc = lax.broadcasted_iota(jnp.int32, (GLA_KEY, GLA_VAL), 1) // GLA_DV
        headsel = (hr == hc).astype(BF16)
        rowid = lax.broadcasted_iota(jnp.int32, (chunk, 1), 0)

        def cbody(idx, carry):
            r0 = pl.multiple_of(idx * chunk, chunk)
            q = q_s[pl.ds(r0, chunk), :]
            k = k_s[pl.ds(r0, chunk), :]
            b = b_s[pl.ds(r0, chunk), :]
            vf = v_s[pl.ds(r0, chunk), :].astype(F32)

            def jbody(jj, acc):
                pick = rowid == jj
                kj = jnp.sum(jnp.where(pick, k, 0.0), axis=0, keepdims=True)
                bj = jnp.sum(jnp.where(pick, b, 0.0), axis=0, keepdims=True)
                vj = jnp.sum(jnp.where(pick, vf, 0.0), axis=0, keepdims=True)
                dm = q * kj * jnp.exp(jnp.minimum(b - bj, 0.0))
                dm = jnp.where(rowid >= jj, dm, 0.0)
                return acc + _dot(dm.astype(BF16), headsel) * vj

            o_s[pl.ds(r0, chunk), :] = lax.fori_loop(
                0, chunk, jbody, jnp.zeros((chunk, GLA_VAL), F32))
            return carry

        lax.fori_loop(0, n_idx, cbody, 0)

    def state_step(r0, s):
        q = q_s[pl.ds(r0, chunk), :]
        k = k_s[pl.ds(r0, chunk), :]
        b = b_s[pl.ds(r0, chunk), :]
        v = v_s[pl.ds(r0, chunk), :]
        blast = b[chunk - 1:chunk, :]
        qb = (q * jnp.exp(b)).astype(BF16)
        kb = (k * jnp.exp(blast - b)).astype(BF16)
        dec = jnp.exp(blast)
        for h in range(GLA_HEADS):
            ksl = slice(h * GLA_DK, (h + 1) * GLA_DK)
            vsl = slice(h * GLA_DV, (h + 1) * GLA_DV)
            st = st_s[s, h]
            o_h = o_s[pl.ds(r0, chunk), vsl] + _dot_nt(qb[:, ksl], st.astype(BF16))
            st_s[s, h] = dec[:, ksl] * st + _dot_tn(v[:, vsl], kb[:, ksl])
            o_h = o_h * lax.rsqrt(jnp.mean(o_h * o_h, axis=-1, keepdims=True) + EPS) * gng
            og_s[pl.ds(r0, chunk), vsl] = (
                o_h * sg_s[pl.ds(r0, chunk), vsl].astype(F32)).astype(BF16)

    loop(state_step)

    for c in range(CONV_DIM // COL_GROUP):
        conv_group(c)
    tail = ubuf_s[:, tbl + SUBLANES - 2:tbl + SUBLANES, :]
    ubuf_s[:, SUBLANES - 2:SUBLANES, :] = tail
    cout_ref[...] = tail
    for c in range(D_MODEL // COL_GROUP):
        yb = _dot(ub_s[...], wconv_ref[:, cols(c)])
        yb_s[:, cols(c)] = (_sigmoid(proj(OFF_GB, c)) * yb).astype(BF16)

    for src, dst in zip(cast_in, cast_out):
        dst[...] = src[...].astype(BF16)

    for c in range(D_MODEL // COL_GROUP):
        ya = _dot(og_s[...], wgla_ref[:, cols(c)])
        merged = _sigmoid(proj(OFF_GA, c)) * ya + yb_s[:, cols(c)].astype(F32)
        ub_s[:, cols(c)] = merged.astype(BF16)
    for c in range(D_MODEL // COL_GROUP):
        m = _dot(ub_s[...], wo_ref[:, cols(c)])
        x1_ref[:, :, cols(c)] = (x_ref[:, :, cols(c)]
                                 + g1[:, :, cols(c)] * m.reshape(n_seq, tbl, COL_GROUP))

    @pl.when(j == nj - 1)
    def _final_state():
        for s in range(n_seq):
            for h in range(GLA_HEADS):
                sout_ref[s, h] = st_s[s, h].T


def _const_spec(shape):
    nd = len(shape)
    return pl.BlockSpec(shape, lambda i, j: (0,) * nd, pipeline_mode=pl.Buffered(1))


BF16_SUBLANES = 16


def _cast_blocking(n_rows, n_steps):
    for n_blocks in range(n_steps, 0, -1):
        if (n_steps % n_blocks == 0 and n_rows % n_blocks == 0
                and (n_rows // n_blocks) % BF16_SUBLANES == 0):
            return n_rows // n_blocks, n_steps // n_blocks
    raise ValueError((n_rows, n_steps))


CAST_STEPS = 9


def _cast_kernel(*refs):
    n = len(refs) // 2
    for src, dst in zip(refs[:n], refs[n:]):
        dst[...] = src[...].astype(BF16)


def _cast_weights(*ws):
    specs = []
    for w in ws:
        n_blocks = max(n for n in range(1, CAST_STEPS + 1)
                       if w.shape[0] % n == 0 and (w.shape[0] // n) % BF16_SUBLANES == 0)
        specs.append(pl.BlockSpec(
            (w.shape[0] // n_blocks, w.shape[1]),
            lambda i, n_blocks=n_blocks: (jnp.minimum(i, n_blocks - 1), 0)))
    return pl.pallas_call(
        _cast_kernel, grid=(CAST_STEPS,), in_specs=specs, out_specs=specs,
        out_shape=[jax.ShapeDtypeStruct(w.shape, BF16) for w in ws],
        compiler_params=pltpu.CompilerParams(
            dimension_semantics=("arbitrary",), vmem_limit_bytes=VMEM_LIMIT),
        name="cast_weights",
    )(*ws)


def _mixer(x, mod, weights, *, n_seq, tbl, chunk, name, carry=None, cast=()):
    n_streams, length, _ = x.shape
    assert n_streams % n_seq == 0 and length % tbl == 0
    grid = (n_streams // n_seq, length // tbl)
    rows = n_seq * tbl
    (n1g, win, walpha, balpha, gng, wgla, convw, convb, wconv, wo) = weights
    carried = carry is not None
    kern = functools.partial(_mixer_kernel, n_seq=n_seq, tbl=tbl, chunk=chunk,
                             n_cast=len(cast), carried=carried)
    carry_specs = [
        pl.BlockSpec((n_seq, GLA_HEADS, GLA_DK, GLA_DV), lambda i, j: (i, 0, 0, 0)),
        pl.BlockSpec((n_seq, CONV_W - 1, CONV_DIM), lambda i, j: (i, 0, 0)),
    ] if carried else []
    cast_specs = []
    for w in cast:
        rb, steps = _cast_blocking(w.shape[0], grid[0] * grid[1])
        cast_specs.append(pl.BlockSpec(
            (rb, w.shape[1]), lambda i, j, steps=steps: ((i * grid[1] + j) // steps, 0)))
    in_specs = [
        pl.BlockSpec((n_seq, tbl, D_MODEL), lambda i, j: (i, j, 0)),
        pl.BlockSpec((n_seq, N_MOD, D_MODEL), lambda i, j: (i, 0, 0)),
        _const_spec(n1g.shape), _const_spec(win.shape), _const_spec(walpha.shape),
        _const_spec(balpha.shape), _const_spec(gng.shape), _const_spec(wgla.shape),
        _const_spec(convw.shape), _const_spec(convb.shape), _const_spec(wconv.shape),
        _const_spec(wo.shape),
    ] + carry_specs + cast_specs
    out_specs = [
        pl.BlockSpec((n_seq, tbl, D_MODEL), lambda i, j: (i, j, 0)),
        pl.BlockSpec((n_seq, GLA_HEADS, GLA_DK, GLA_DV), lambda i, j: (i, 0, 0, 0)),
        pl.BlockSpec((n_seq, CONV_W - 1, CONV_DIM), lambda i, j: (i, 0, 0)),
    ] + cast_specs
    out_shape = [
        jax.ShapeDtypeStruct(x.shape, F32),
        jax.ShapeDtypeStruct((n_streams, GLA_HEADS, GLA_DK, GLA_DV), F32),
        jax.ShapeDtypeStruct((n_streams, CONV_W - 1, CONV_DIM), F32),
    ] + [jax.ShapeDtypeStruct(w.shape, BF16) for w in cast]
    scratch = [
        pltpu.VMEM((rows, D_MODEL), BF16),
        pltpu.VMEM((rows, GLA_KEY), F32),
        pltpu.VMEM((rows, GLA_KEY), F32),
        pltpu.VMEM((rows, GLA_VAL), BF16),
        pltpu.VMEM((rows, GLA_VAL), BF16),
        pltpu.VMEM((rows, GLA_KEY), F32),
        pltpu.VMEM((n_seq, tbl + SUBLANES, CONV_DIM), F32),
        pltpu.VMEM((rows, D_MODEL), BF16),
        pltpu.VMEM((rows, D_MODEL), BF16),
        pltpu.VMEM((rows, GLA_VAL), F32),
        pltpu.VMEM((rows, GLA_VAL), BF16),
        pltpu.VMEM((n_seq, GLA_HEADS, GLA_DV, GLA_DK), F32),
    ]
    return pl.pallas_call(
        kern, grid=grid, in_specs=in_specs, out_specs=out_specs, out_shape=out_shape,
        scratch_shapes=scratch,
        compiler_params=pltpu.CompilerParams(
            dimension_semantics=("arbitrary", "arbitrary"), vmem_limit_bytes=VMEM_LIMIT),
        name=name,
    )(x, mod, n1g, win, walpha, balpha, gng, wgla, convw, convb, wconv, wo,
      *(carry or ()), *cast)


FF_GROUP = 256


def _ffn_kernel(x_ref, mod_ref, n2g_ref, win_ref, wout_ref, nfg_ref, y_ref, hb_s, act_s,
                *, n_seq, tbl, sub):
    sh2 = mod_ref[:, 3:4, :]
    sc2 = mod_ref[:, 4:5, :]
    g2 = mod_ref[:, 5:6, :]
    for t0 in range(0, tbl, sub):
        rows = n_seq * sub
        r0 = (t0 // sub) * rows
        x3 = x_ref[:, t0:t0 + sub, :]
        ms = jnp.mean(x3 * x3, axis=-1, keepdims=True)
        h3 = x3 * lax.rsqrt(ms + EPS) * n2g_ref[...] * (1.0 + sc2) + sh2
        hb_s[r0:r0 + rows, :] = h3.reshape(rows, D_MODEL).astype(BF16)
        hb = hb_s[r0:r0 + rows, :]
        for c in range(D_FF // FF_GROUP):
            gt = _dot(hb, win_ref[:, c * FF_GROUP:(c + 1) * FF_GROUP])
            up = _dot(hb, win_ref[:, D_FF + c * FF_GROUP:D_FF + (c + 1) * FF_GROUP])
            act_s[r0:r0 + rows, c * FF_GROUP:(c + 1) * FF_GROUP] = (
                gt * _sigmoid(gt) * up).astype(BF16)
        f = _dot(act_s[r0:r0 + rows, :], wout_ref[...])
        x2 = x3 + g2 * f.reshape(n_seq, sub, D_MODEL)
        ms2 = jnp.mean(x2 * x2, axis=-1, keepdims=True)
        y_ref[:, t0:t0 + sub, :] = x2 * lax.rsqrt(ms2 + EPS) * nfg_ref[...]


def _ffn(x, mod, n2g, win, wout, nfg, *, n_seq, tbl, sub, name):
    n_streams, length, _ = x.shape
    assert n_streams % n_seq == 0 and length % tbl == 0
    grid = (n_streams // n_seq, length // tbl)
    rows = n_seq * tbl
    kern = functools.partial(_ffn_kernel, n_seq=n_seq, tbl=tbl, sub=sub)
    return pl.pallas_call(
        kern, grid=grid,
        in_specs=[
            pl.BlockSpec((n_seq, tbl, D_MODEL), lambda i, j: (i, j, 0)),
            pl.BlockSpec((n_seq, N_MOD, D_MODEL), lambda i, j: (i, 0, 0)),
            _const_spec(n2g.shape), _const_spec(win.shape), _const_spec(wout.shape),
            _const_spec(nfg.shape),
        ],
        out_specs=pl.BlockSpec((n_seq, tbl, D_MODEL), lambda i, j: (i, j, 0)),
        out_shape=jax.ShapeDtypeStruct(x.shape, F32),
        scratch_shapes=[pltpu.VMEM((rows, D_MODEL), BF16), pltpu.VMEM((rows, D_FF), BF16)],
        compiler_params=pltpu.CompilerParams(
            dimension_semantics=("arbitrary", "arbitrary"), vmem_limit_bytes=VMEM_LIMIT),
        name=name,
    )(x, mod, n2g, win, wout, nfg)


def kernel(x_prompt, x_sample, c_prompt, c_sample, state_gla, cache_conv, w_mod, b_mod, norm1_g,
           w_in, w_alpha, b_alpha, gla_norm_g, w_gla_out, conv_w, conv_b, w_conv_out, w_o,
           norm2_g, w_ffn_in, w_ffn_out, norm_f_g):
    bs = x_sample.shape[0]
    depth = w_mod.shape[0]
    assert depth == 1
    l = 0
    mod_p, mod_s = _modulation(c_prompt, c_sample, w_mod[l], b_mod[l])

    assert w_in.shape[1:] == (D_MODEL, IN_DIM)
    win_t, wgla, wconv, wo = _cast_weights(
        jnp.swapaxes(w_in[l], 0, 1), w_gla_out[l], w_conv_out[l], w_o[l])
    walpha = w_alpha[l].astype(BF16)
    mix_w = (norm1_g[l].reshape(1, -1), win_t, walpha, b_alpha[l].reshape(1, -1),
             gla_norm_g[l].reshape(1, -1), wgla, conv_w, conv_b[l].reshape(1, -1), wconv, wo)

    x1_p, st_p, cv_p, wffn_in, wffn_out = _mixer(
        x_prompt, mod_p, mix_w, n_seq=1, tbl=512, chunk=256, name="mixer_prompt",
        cast=(w_ffn_in[l], w_ffn_out[l]))
    ffn_w = (norm2_g[l].reshape(1, -1), wffn_in, wffn_out, norm_f_g.reshape(1, -1))
    x1_s, st_s, cv_s = _mixer(x_sample, mod_s, mix_w, carry=(state_gla[l], cache_conv[l]),
                              n_seq=8, tbl=x_sample.shape[1], chunk=x_sample.shape[1],
                              name="mixer_sample")
    y_p = _ffn(x1_p, mod_p, *ffn_w, n_seq=1, tbl=1024, sub=256, name="ffn_prompt")
    y_s = _ffn(x1_s, mod_s, *ffn_w, n_seq=bs, tbl=x_sample.shape[1], sub=x_sample.shape[1],
               name="ffn_sample")
    return (y_p, y_s, st_p[None], cv_p[None], st_s[None], cv_s[None])
```

```python
import functools

import jax
import jax.numpy as jnp
from jax import lax
from jax.experimental import pallas as pl
from jax.experimental.pallas import tpu as pltpu

F32 = jnp.float32
BF16 = jnp.bfloat16

D_MODEL = 1024
GLA_HEADS = 4
GLA_DK = 128
GLA_DV = 256
GLA_KEY = GLA_HEADS * GLA_DK
GLA_VAL = GLA_HEADS * GLA_DV
GLA_RANK = 16
GLA_TAU = 16.0
CONV_DIM = D_MODEL
CONV_W = 3
D_FF = 2816
N_MOD = 6
EPS = 1e-6

COL_GROUP = 256
SUBLANES = 8

OFF_Q = 0
OFF_K = OFF_Q + GLA_KEY
OFF_V = OFF_K + GLA_KEY
OFF_G = OFF_V + GLA_VAL
OFF_A = OFF_G + GLA_VAL
OFF_CB = OFF_A + GLA_RANK
OFF_CC = OFF_CB + CONV_DIM
OFF_CH = OFF_CC + CONV_DIM
OFF_GA = OFF_CH + CONV_DIM
OFF_GB = OFF_GA + D_MODEL
IN_DIM = OFF_GB + D_MODEL

SAFE_LOG_DECAY = 60.0

VMEM_LIMIT = 60 * 1024 * 1024


def _dot(a, b):
    return jnp.dot(a, b, preferred_element_type=F32)


def _dot_nt(a, b):
    return lax.dot_general(a, b, (((1,), (1,)), ((), ())), preferred_element_type=F32)


def _dot_tn(a, b):
    return lax.dot_general(a, b, (((0,), (0,)), ((), ())), preferred_element_type=F32)


def _sigmoid(x):
    return 1.0 / (1.0 + jnp.exp(-x))


def _mod_kernel(*refs):
    cp_ref, cs_ref, w_ref, b_ref = refs[:4]
    n_cast = (len(refs) - 6) // 2
    cast_in = refs[4:4 + n_cast]
    op_ref, os_ref = refs[4 + n_cast:6 + n_cast]
    cast_out = refs[6 + n_cast:]
    j = pl.program_id(0)
    w = w_ref[...].astype(BF16)
    vp = _dot(cp_ref[...].astype(BF16), w) + b_ref[...]
    vs = _dot(cs_ref[...].astype(BF16), w) + b_ref[...]
    for c in range(N_MOD):
        @pl.when(j == c)
        def _store():
            op_ref[:, c, :] = vp
            os_ref[:, c, :] = vs
    for src, dst in zip(cast_in, cast_out):
        dst[...] = src[...].astype(BF16)


def _modulation(c_prompt, c_sample, w_mod, b_mod, cast=()):
    bp, bs = c_prompt.shape[0], c_sample.shape[0]
    cast_specs = []
    for w in cast:
        n_blocks = max(n for n in range(1, N_MOD + 1)
                       if w.shape[0] % n == 0 and (w.shape[0] // n) % BF16_SUBLANES == 0)
        cast_specs.append(pl.BlockSpec(
            (w.shape[0] // n_blocks, w.shape[1]),
            lambda j, n_blocks=n_blocks: (jnp.minimum(j, n_blocks - 1), 0)))
    return pl.pallas_call(
        _mod_kernel,
        grid=(N_MOD,),
        in_specs=[
            pl.BlockSpec((bp, D_MODEL), lambda j: (0, 0)),
            pl.BlockSpec((bs, D_MODEL), lambda j: (0, 0)),
            pl.BlockSpec((D_MODEL, D_MODEL), lambda j: (0, j)),
            pl.BlockSpec((1, D_MODEL), lambda j: (0, j)),
        ] + cast_specs,
        out_specs=[
            pl.BlockSpec((bp, N_MOD, D_MODEL), lambda j: (0, 0, 0)),
            pl.BlockSpec((bs, N_MOD, D_MODEL), lambda j: (0, 0, 0)),
        ] + cast_specs,
        out_shape=[jax.ShapeDtypeStruct((bp, N_MOD, D_MODEL), F32),
                   jax.ShapeDtypeStruct((bs, N_MOD, D_MODEL), F32)]
        + [jax.ShapeDtypeStruct(w.shape, BF16) for w in cast],
        compiler_params=pltpu.CompilerParams(
            dimension_semantics=("arbitrary",), vmem_limit_bytes=VMEM_LIMIT),
        name="adaln_mod",
    )(c_prompt, c_sample, w_mod, b_mod.reshape(1, -1), *cast)


N_MIXER_IN = 12
N_MIXER_OUT = 3


def _mixer_kernel(*refs, n_seq, tbl, chunk, n_cast, carried):
    (x_ref, mod_ref, n1g_ref, win_ref, walpha_ref, balpha_ref,
     gng_ref, wgla_ref, convw_ref, convb_ref, wconv_ref, wo_ref) = refs[:N_MIXER_IN]
    refs = refs[N_MIXER_IN:]
    if carried:
        (s0_ref, cprev_ref), refs = refs[:2], refs[2:]
    cast_in, refs = refs[:n_cast], refs[n_cast:]
    (x1_ref, sout_ref, cout_ref), refs = refs[:N_MIXER_OUT], refs[N_MIXER_OUT:]
    cast_out, refs = refs[:n_cast], refs[n_cast:]
    (hb_s, q_s, k_s, v_s, sg_s, b_s, ubuf_s, ub_s, yb_s, o_s, og_s, st_s) = refs
    rows = n_seq * tbl
    nchunk = tbl // chunk
    j = pl.program_id(1)
    nj = pl.num_programs(1)

    @pl.when(j == 0)
    def _init():
        for s in range(n_seq):
            for h in range(GLA_HEADS):
                st_s[s, h] = s0_ref[s, h].T if carried else jnp.zeros((GLA_DV, GLA_DK), F32)
        ubuf_s[:, SUBLANES - 2:SUBLANES, :] = (
            cprev_ref[...] if carried else jnp.zeros((n_seq, CONV_W - 1, CONV_DIM), F32))

    sh1 = mod_ref[:, 0:1, :]
    sc1 = mod_ref[:, 1:2, :]
    g1 = mod_ref[:, 2:3, :]
    x3 = x_ref[...]
    ms = jnp.mean(x3 * x3, axis=-1, keepdims=True)
    h3 = x3 * lax.rsqrt(ms + EPS) * n1g_ref[...] * (1.0 + sc1) + sh1
    hb_s[...] = h3.reshape(rows, D_MODEL).astype(BF16)

    def proj(off, c, n=COL_GROUP):
        return _dot_nt(hb_s[...], win_ref[off + c * n:off + (c + 1) * n, :])

    def cols(c):
        return slice(c * COL_GROUP, (c + 1) * COL_GROUP)

    za = proj(OFF_A, 0, GLA_RANK)
    for c in range(GLA_KEY // COL_GROUP):
        q_s[:, cols(c)] = proj(OFF_Q, c) * (GLA_DK ** -0.5)
    xa = _dot(za.astype(BF16), walpha_ref[...]) + balpha_ref[...]
    la = (jnp.minimum(xa, 0.0) - jnp.log(1.0 + jnp.exp(-jnp.abs(xa)))) * (1.0 / GLA_TAU)
    for c in range(GLA_KEY // COL_GROUP):
        k_s[:, cols(c)] = proj(OFF_K, c)

    ri = lax.broadcasted_iota(jnp.int32, (chunk, chunk), 0)
    ci = lax.broadcasted_iota(jnp.int32, (chunk, chunk), 1)
    tril = ri >= ci
    tmat = tril.astype(BF16)
    bmin = jnp.zeros((1, GLA_KEY), F32)
    for idx in range(n_seq * nchunk):
        la_c = la[idx * chunk:(idx + 1) * chunk, :]
        hi = la_c.astype(BF16)
        lo = (la_c - hi.astype(F32)).astype(BF16)
        b_c = _dot(tmat, hi) + _dot(tmat, lo)
        b_s[idx * chunk:(idx + 1) * chunk, :] = b_c
        bmin = jnp.minimum(bmin, b_c[chunk - 1:chunk, :])
    safe = jnp.min(bmin) > -SAFE_LOG_DECAY

    for c in range(GLA_VAL // COL_GROUP):
        v_s[:, cols(c)] = proj(OFF_V, c).astype(BF16)
        g = proj(OFF_G, c)
        sg_s[:, cols(c)] = (g * _sigmoid(g)).astype(BF16)

    gng = gng_ref[...]
    n_idx = n_seq * nchunk

    def loop(body):
        for idx in range(n_idx):
            body(idx * chunk, idx // nchunk)

    def intra_fast(r0, s):
        q = q_s[pl.ds(r0, chunk), :]
        k = k_s[pl.ds(r0, chunk), :]
        b = b_s[pl.ds(r0, chunk), :]
        v = v_s[pl.ds(r0, chunk), :]
        bmid = b[chunk // 2 - 1:chunk // 2, :]
        qs = (q * jnp.exp(b - bmid)).astype(BF16)
        ks = (k * jnp.exp(bmid - b)).astype(BF16)
        for h in range(GLA_HEADS):
            ksl = slice(h * GLA_DK, (h + 1) * GLA_DK)
            vsl = slice(h * GLA_DV, (h + 1) * GLA_DV)
            a = jnp.where(tril, _dot_nt(qs[:, ksl], ks[:, ksl]), 0.0)
            o_s[pl.ds(r0, chunk), vsl] = _dot(a.astype(BF16), v[:, vsl])

    def conv_group(c):
        u = proj(OFF_CC, c) * proj(OFF_CH, c)
        ubuf_s[:, SUBLANES:, cols(c)] = u.reshape(n_seq, tbl, COL_GROUP)
        conv = (ubuf_s[:, SUBLANES - 2:SUBLANES - 2 + tbl, cols(c)] * convw_ref[0, 0:1, cols(c)]
                + ubuf_s[:, SUBLANES - 1:SUBLANES - 1 + tbl, cols(c)] * convw_ref[0, 1:2, cols(c)]
                + u.reshape(n_seq, tbl, COL_GROUP) * convw_ref[0, 2:3, cols(c)]
                + convb_ref[:, cols(c)])
        cb = proj(OFF_CB, c)
        ub_s[:, cols(c)] = (cb.reshape(n_seq, tbl, COL_GROUP) * conv).reshape(
            rows, COL_GROUP).astype(BF16)

    loop(intra_fast)

    @pl.when(jnp.logical_not(safe))
    def _pairwise():
        hr = lax.broadcasted_iota(jnp.int32, (GLA_KEY, GLA_VAL), 0) // GLA_DK
        hc = lax.broadcasted_iota(jnp.int32, (GLA_KEY, GLA_VAL), 1) // GLA_DV
        headsel = (hr == hc).astype(BF16)
        rowid = lax.broadcasted_iota(jnp.int32, (chunk, 1), 0)

        def cbody(idx, carry):
            r0 = pl.multiple_of(idx * chunk, chunk)
            q = q_s[pl.ds(r0, chunk), :]
            k = k_s[pl.ds(r0, chunk), :]
            b = b_s[pl.ds(r0, chunk), :]
            vf = v_s[pl.ds(r0, chunk), :].astype(F32)

            def jbody(jj, acc):
                pick = rowid == jj
                kj = jnp.sum(jnp.where(pick, k, 0.0), axis=0, keepdims=True)
                bj = jnp.sum(jnp.where(pick, b, 0.0), axis=0, keepdims=True)
                vj = jnp.sum(jnp.where(pick, vf, 0.0), axis=0, keepdims=True)
                dm = q * kj * jnp.exp(jnp.minimum(b - bj, 0.0))
                dm = jnp.where(rowid >= jj, dm, 0.0)
                return acc + _dot(dm.astype(BF16), headsel) * vj

            o_s[pl.ds(r0, chunk), :] = lax.fori_loop(
                0, chunk, jbody, jnp.zeros((chunk, GLA_VAL), F32))
            return carry

        lax.fori_loop(0, n_idx, cbody, 0)

    def state_step(r0, s):
        q = q_s[pl.ds(r0, chunk), :]
        k = k_s[pl.ds(r0, chunk), :]
        b = b_s[pl.ds(r0, chunk), :]
        v = v_s[pl.ds(r0, chunk), :]
        blast = b[chunk - 1:chunk, :]
        qb = (q * jnp.exp(b)).astype(BF16)
        kb = (k * jnp.exp(blast - b)).astype(BF16)
        dec = jnp.exp(blast)
        for h in range(GLA_HEADS):
            ksl = slice(h * GLA_DK, (h + 1) * GLA_DK)
            vsl = slice(h * GLA_DV, (h + 1) * GLA_DV)
            st = st_s[s, h]
            o_h = o_s[pl.ds(r0, chunk), vsl] + _dot_nt(qb[:, ksl], st.astype(BF16))
            st_s[s, h] = dec[:, ksl] * st + _dot_tn(v[:, vsl], kb[:, ksl])
            o_h = o_h * lax.rsqrt(jnp.mean(o_h * o_h, axis=-1, keepdims=True) + EPS) * gng
            og_s[pl.ds(r0, chunk), vsl] = (
                o_h * sg_s[pl.ds(r0, chunk), vsl].astype(F32)).astype(BF16)

    loop(state_step)

    for c in range(CONV_DIM // COL_GROUP):
        conv_group(c)
    tail = ubuf_s[:, tbl + SUBLANES - 2:tbl + SUBLANES, :]
    ubuf_s[:, SUBLANES - 2:SUBLANES, :] = tail
    cout_ref[...] = tail
    for c in range(D_MODEL // COL_GROUP):
        yb = _dot(ub_s[...], wconv_ref[:, cols(c)])
        yb_s[:, cols(c)] = (_sigmoid(proj(OFF_GB, c)) * yb).astype(BF16)

    for src, dst in zip(cast_in, cast_out):
        dst[...] = src[...].astype(BF16)

    for c in range(D_MODEL // COL_GROUP):
        ya = _dot(og_s[...], wgla_ref[:, cols(c)])
        merged = _sigmoid(proj(OFF_GA, c)) * ya + yb_s[:, cols(c)].astype(F32)
        ub_s[:, cols(c)] = merged.astype(BF16)
    for c in range(D_MODEL // COL_GROUP):
        m = _dot(ub_s[...], wo_ref[:, cols(c)])
        x1_ref[:, :, cols(c)] = (x_ref[:, :, cols(c)]
                                 + g1[:, :, cols(c)] * m.reshape(n_seq, tbl, COL_GROUP))

    @pl.when(j == nj - 1)
    def _final_state():
        for s in range(n_seq):
            for h in range(GLA_HEADS):
                sout_ref[s, h] = st_s[s, h].T


def _const_spec(shape):
    nd = len(shape)
    return pl.BlockSpec(shape, lambda i, j: (0,) * nd, pipeline_mode=pl.Buffered(1))


BF16_SUBLANES = 16


def _cast_blocking(n_rows, n_steps):
    for n_blocks in range(n_steps, 0, -1):
        if (n_steps % n_blocks == 0 and n_rows % n_blocks == 0
                and (n_rows // n_blocks) % BF16_SUBLANES == 0):
            return n_rows // n_blocks, n_steps // n_blocks
    raise ValueError((n_rows, n_steps))


def _mixer(x, mod, weights, *, n_seq, tbl, chunk, name, carry=None, cast=()):
    n_streams, length, _ = x.shape
    assert n_streams % n_seq == 0 and length % tbl == 0
    grid = (n_streams // n_seq, length // tbl)
    rows = n_seq * tbl
    (n1g, win, walpha, balpha, gng, wgla, convw, convb, wconv, wo) = weights
    carried = carry is not None
    kern = functools.partial(_mixer_kernel, n_seq=n_seq, tbl=tbl, chunk=chunk,
                             n_cast=len(cast), carried=carried)
    carry_specs = [
        pl.BlockSpec((n_seq, GLA_HEADS, GLA_DK, GLA_DV), lambda i, j: (i, 0, 0, 0)),
        pl.BlockSpec((n_seq, CONV_W - 1, CONV_DIM), lambda i, j: (i, 0, 0)),
    ] if carried else []
    cast_specs = []
    for w in cast:
        rb, steps = _cast_blocking(w.shape[0], grid[0] * grid[1])
        cast_specs.append(pl.BlockSpec(
            (rb, w.shape[1]), lambda i, j, steps=steps: ((i * grid[1] + j) // steps, 0)))
    in_specs = [
        pl.BlockSpec((n_seq, tbl, D_MODEL), lambda i, j: (i, j, 0)),
        pl.BlockSpec((n_seq, N_MOD, D_MODEL), lambda i, j: (i, 0, 0)),
        _const_spec(n1g.shape), _const_spec(win.shape), _const_spec(walpha.shape),
        _const_spec(balpha.shape), _const_spec(gng.shape), _const_spec(wgla.shape),
        _const_spec(convw.shape), _const_spec(convb.shape), _const_spec(wconv.shape),
        _const_spec(wo.shape),
    ] + carry_specs + cast_specs
    out_specs = [
        pl.BlockSpec((n_seq, tbl, D_MODEL), lambda i, j: (i, j, 0)),
        pl.BlockSpec((n_seq, GLA_HEADS, GLA_DK, GLA_DV), lambda i, j: (i, 0, 0, 0)),
        pl.BlockSpec((n_seq, CONV_W - 1, CONV_DIM), lambda i, j: (i, 0, 0)),
    ] + cast_specs
    out_shape = [
        jax.ShapeDtypeStruct(x.shape, F32),
        jax.ShapeDtypeStruct((n_streams, GLA_HEADS, GLA_DK, GLA_DV), F32),
        jax.ShapeDtypeStruct((n_streams, CONV_W - 1, CONV_DIM), F32),
    ] + [jax.ShapeDtypeStruct(w.shape, BF16) for w in cast]
    scratch = [
        pltpu.VMEM((rows, D_MODEL), BF16),
        pltpu.VMEM((rows, GLA_KEY), F32),
        pltpu.VMEM((rows, GLA_KEY), F32),
        pltpu.VMEM((rows, GLA_VAL), BF16),
        pltpu.VMEM((rows, GLA_VAL), BF16),
        pltpu.VMEM((rows, GLA_KEY), F32),
        pltpu.VMEM((n_seq, tbl + SUBLANES, CONV_DIM), F32),
        pltpu.VMEM((rows, D_MODEL), BF16),
        pltpu.VMEM((rows, D_MODEL), BF16),
        pltpu.VMEM((rows, GLA_VAL), F32),
        pltpu.VMEM((rows, GLA_VAL), BF16),
        pltpu.VMEM((n_seq, GLA_HEADS, GLA_DV, GLA_DK), F32),
    ]
    return pl.pallas_call(
        kern, grid=grid, in_specs=in_specs, out_specs=out_specs, out_shape=out_shape,
        scratch_shapes=scratch,
        compiler_params=pltpu.CompilerParams(
            dimension_semantics=("arbitrary", "arbitrary"), vmem_limit_bytes=VMEM_LIMIT),
        name=name,
    )(x, mod, n1g, win, walpha, balpha, gng, wgla, convw, convb, wconv, wo,
      *(carry or ()), *cast)


FF_GROUP = 256


def _ffn_kernel(x_ref, mod_ref, n2g_ref, win_ref, wout_ref, nfg_ref, y_ref, hb_s, act_s,
                *, n_seq, tbl, sub):
    sh2 = mod_ref[:, 3:4, :]
    sc2 = mod_ref[:, 4:5, :]
    g2 = mod_ref[:, 5:6, :]
    for t0 in range(0, tbl, sub):
        rows = n_seq * sub
        r0 = (t0 // sub) * rows
        x3 = x_ref[:, t0:t0 + sub, :]
        ms = jnp.mean(x3 * x3, axis=-1, keepdims=True)
        h3 = x3 * lax.rsqrt(ms + EPS) * n2g_ref[...] * (1.0 + sc2) + sh2
        hb_s[r0:r0 + rows, :] = h3.reshape(rows, D_MODEL).astype(BF16)
        hb = hb_s[r0:r0 + rows, :]
        for c in range(D_FF // FF_GROUP):
            gt = _dot(hb, win_ref[:, c * FF_GROUP:(c + 1) * FF_GROUP])
            up = _dot(hb, win_ref[:, D_FF + c * FF_GROUP:D_FF + (c + 1) * FF_GROUP])
            act_s[r0:r0 + rows, c * FF_GROUP:(c + 1) * FF_GROUP] = (
                gt * _sigmoid(gt) * up).astype(BF16)
        f = _dot(act_s[r0:r0 + rows, :], wout_ref[...])
        x2 = x3 + g2 * f.reshape(n_seq, sub, D_MODEL)
        ms2 = jnp.mean(x2 * x2, axis=-1, keepdims=True)
        y_ref[:, t0:t0 + sub, :] = x2 * lax.rsqrt(ms2 + EPS) * nfg_ref[...]


def _ffn(x, mod, n2g, win, wout, nfg, *, n_seq, tbl, sub, name):
    n_streams, length, _ = x.shape
    assert n_streams % n_seq == 0 and length % tbl == 0
    grid = (n_streams // n_seq, length // tbl)
    rows = n_seq * tbl
    kern = functools.partial(_ffn_kernel, n_seq=n_seq, tbl=tbl, sub=sub)
    return pl.pallas_call(
        kern, grid=grid,
        in_specs=[
            pl.BlockSpec((n_seq, tbl, D_MODEL), lambda i, j: (i, j, 0)),
            pl.BlockSpec((n_seq, N_MOD, D_MODEL), lambda i, j: (i, 0, 0)),
            _const_spec(n2g.shape), _const_spec(win.shape), _const_spec(wout.shape),
            _const_spec(nfg.shape),
        ],
        out_specs=pl.BlockSpec((n_seq, tbl, D_MODEL), lambda i, j: (i, j, 0)),
        out_shape=jax.ShapeDtypeStruct(x.shape, F32),
        scratch_shapes=[pltpu.VMEM((rows, D_MODEL), BF16), pltpu.VMEM((rows, D_FF), BF16)],
        compiler_params=pltpu.CompilerParams(
            dimension_semantics=("arbitrary", "arbitrary"), vmem_limit_bytes=VMEM_LIMIT),
        name=name,
    )(x, mod, n2g, win, wout, nfg)


def kernel(x_prompt, x_sample, c_prompt, c_sample, state_gla, cache_conv, w_mod, b_mod, norm1_g,
           w_in, w_alpha, b_alpha, gla_norm_g, w_gla_out, conv_w, conv_b, w_conv_out, w_o,
           norm2_g, w_ffn_in, w_ffn_out, norm_f_g):
    bs = x_sample.shape[0]
    depth = w_mod.shape[0]
    assert depth == 1
    l = 0
    assert w_in.shape[1:] == (D_MODEL, IN_DIM)
    mod_p, mod_s, win_t, wgla, wconv, wo = _modulation(
        c_prompt, c_sample, w_mod[l], b_mod[l],
        cast=(jnp.swapaxes(w_in[l], 0, 1), w_gla_out[l], w_conv_out[l], w_o[l]))
    walpha = w_alpha[l].astype(BF16)
    mix_w = (norm1_g[l].reshape(1, -1), win_t, walpha, b_alpha[l].reshape(1, -1),
             gla_norm_g[l].reshape(1, -1), wgla, conv_w, conv_b[l].reshape(1, -1), wconv, wo)

    x1_p, st_p, cv_p, wffn_in, wffn_out = _mixer(
        x_prompt, mod_p, mix_w, n_seq=1, tbl=512, chunk=256, name="mixer_prompt",
        cast=(w_ffn_in[l], w_ffn_out[l]))
    ffn_w = (norm2_g[l].reshape(1, -1), wffn_in, wffn_out, norm_f_g.reshape(1, -1))
    x1_s, st_s, cv_s = _mixer(x_sample, mod_s, mix_w, carry=(state_gla[l], cache_conv[l]),
                              n_seq=8, tbl=x_sample.shape[1], chunk=x_sample.shape[1],
                              name="mixer_sample")
    y_p = _ffn(x1_p, mod_p, *ffn_w, n_seq=1, tbl=1024, sub=256, name="ffn_prompt")
    y_s = _ffn(x1_s, mod_s, *ffn_w, n_seq=bs, tbl=x_sample.shape[1], sub=x_sample.shape[1],
               name="ffn_sample")
    return (y_p, y_s, st_p[None], cv_p[None], st_s[None], cv_s[None])
```

```python
import functools

import jax
import jax.numpy as jnp
from jax import lax
from jax.experimental import pallas as pl
from jax.experimental.pallas import tpu as pltpu

F32 = jnp.float32
BF16 = jnp.bfloat16

D_MODEL = 1024
GLA_HEADS = 4
GLA_DK = 128
GLA_DV = 256
GLA_KEY = GLA_HEADS * GLA_DK
GLA_VAL = GLA_HEADS * GLA_DV
GLA_RANK = 16
GLA_TAU = 16.0
CONV_DIM = D_MODEL
CONV_W = 3
D_FF = 2816
N_MOD = 6
EPS = 1e-6

COL_GROUP = 256
SUBLANES = 8

OFF_Q = 0
OFF_K = OFF_Q + GLA_KEY
OFF_V = OFF_K + GLA_KEY
OFF_G = OFF_V + GLA_VAL
OFF_A = OFF_G + GLA_VAL
OFF_CB = OFF_A + GLA_RANK
OFF_CC = OFF_CB + CONV_DIM
OFF_CH = OFF_CC + CONV_DIM
OFF_GA = OFF_CH + CONV_DIM
OFF_GB = OFF_GA + D_MODEL
IN_DIM = OFF_GB + D_MODEL

SAFE_LOG_DECAY = 60.0

INTRA_SKEW = 4

VMEM_LIMIT = 60 * 1024 * 1024


def _dot(a, b):
    return jnp.dot(a, b, preferred_element_type=F32)


def _dot_nt(a, b):
    return lax.dot_general(a, b, (((1,), (1,)), ((), ())), preferred_element_type=F32)


def _dot_tn(a, b):
    return lax.dot_general(a, b, (((0,), (0,)), ((), ())), preferred_element_type=F32)


def _sigmoid(x):
    return 1.0 / (1.0 + jnp.exp(-x))


def _mod_kernel(*refs):
    cp_ref, cs_ref, w_ref, b_ref = refs[:4]
    n_cast = (len(refs) - 6) // 2
    cast_in = refs[4:4 + n_cast]
    op_ref, os_ref = refs[4 + n_cast:6 + n_cast]
    cast_out = refs[6 + n_cast:]
    j = pl.program_id(0)
    w = w_ref[...].astype(BF16)
    vp = _dot(cp_ref[...].astype(BF16), w) + b_ref[...]
    vs = _dot(cs_ref[...].astype(BF16), w) + b_ref[...]
    for c in range(N_MOD):
        @pl.when(j == c)
        def _store():
            op_ref[:, c, :] = vp
            os_ref[:, c, :] = vs
    for src, dst in zip(cast_in, cast_out):
        dst[...] = src[...].astype(BF16)


def _modulation(c_prompt, c_sample, w_mod, b_mod, cast=()):
    bp, bs = c_prompt.shape[0], c_sample.shape[0]
    cast_specs = []
    for w in cast:
        n_blocks = max(n for n in range(1, N_MOD + 1)
                       if w.shape[0] % n == 0 and (w.shape[0] // n) % BF16_SUBLANES == 0)
        cast_specs.append(pl.BlockSpec(
            (w.shape[0] // n_blocks, w.shape[1]),
            lambda j, n_blocks=n_blocks: (jnp.minimum(j, n_blocks - 1), 0)))
    return pl.pallas_call(
        _mod_kernel,
        grid=(N_MOD,),
        in_specs=[
            pl.BlockSpec((bp, D_MODEL), lambda j: (0, 0)),
            pl.BlockSpec((bs, D_MODEL), lambda j: (0, 0)),
            pl.BlockSpec((D_MODEL, D_MODEL), lambda j: (0, j)),
            pl.BlockSpec((1, D_MODEL), lambda j: (0, j)),
        ] + cast_specs,
        out_specs=[
            pl.BlockSpec((bp, N_MOD, D_MODEL), lambda j: (0, 0, 0)),
            pl.BlockSpec((bs, N_MOD, D_MODEL), lambda j: (0, 0, 0)),
        ] + cast_specs,
        out_shape=[jax.ShapeDtypeStruct((bp, N_MOD, D_MODEL), F32),
                   jax.ShapeDtypeStruct((bs, N_MOD, D_MODEL), F32)]
        + [jax.ShapeDtypeStruct(w.shape, BF16) for w in cast],
        compiler_params=pltpu.CompilerParams(
            dimension_semantics=("arbitrary",), vmem_limit_bytes=VMEM_LIMIT),
        name="adaln_mod",
    )(c_prompt, c_sample, w_mod, b_mod.reshape(1, -1), *cast)


N_MIXER_IN = 12
N_MIXER_OUT = 3


def _mixer_kernel(*refs, n_seq, tbl, chunk, n_cast, carried):
    (x_ref, mod_ref, n1g_ref, win_ref, walpha_ref, balpha_ref,
     gng_ref, wgla_ref, convw_ref, convb_ref, wconv_ref, wo_ref) = refs[:N_MIXER_IN]
    refs = refs[N_MIXER_IN:]
    if carried:
        (s0_ref, cprev_ref), refs = refs[:2], refs[2:]
    cast_in, refs = refs[:n_cast], refs[n_cast:]
    (x1_ref, sout_ref, cout_ref), refs = refs[:N_MIXER_OUT], refs[N_MIXER_OUT:]
    cast_out, refs = refs[:n_cast], refs[n_cast:]
    (hb_s, q_s, k_s, v_s, sg_s, b_s, ubuf_s, ub_s, yb_s, o_s, og_s, st_s) = refs
    rows = n_seq * tbl
    nchunk = tbl // chunk
    j = pl.program_id(1)
    nj = pl.num_programs(1)

    @pl.when(j == 0)
    def _init():
        for s in range(n_seq):
            for h in range(GLA_HEADS):
                st_s[s, h] = s0_ref[s, h].T if carried else jnp.zeros((GLA_DV, GLA_DK), F32)
        ubuf_s[:, SUBLANES - 2:SUBLANES, :] = (
            cprev_ref[...] if carried else jnp.zeros((n_seq, CONV_W - 1, CONV_DIM), F32))

    sh1 = mod_ref[:, 0:1, :]
    sc1 = mod_ref[:, 1:2, :]
    g1 = mod_ref[:, 2:3, :]
    x3 = x_ref[...]
    ms = jnp.mean(x3 * x3, axis=-1, keepdims=True)
    h3 = x3 * lax.rsqrt(ms + EPS) * n1g_ref[...] * (1.0 + sc1) + sh1
    hb_s[...] = h3.reshape(rows, D_MODEL).astype(BF16)

    def proj(off, c, n=COL_GROUP):
        return _dot_nt(hb_s[...], win_ref[off + c * n:off + (c + 1) * n, :])

    def cols(c):
        return slice(c * COL_GROUP, (c + 1) * COL_GROUP)

    za = proj(OFF_A, 0, GLA_RANK)
    for c in range(GLA_KEY // COL_GROUP):
        q_s[:, cols(c)] = proj(OFF_Q, c) * (GLA_DK ** -0.5)
    xa = _dot(za.astype(BF16), walpha_ref[...]) + balpha_ref[...]
    la = (jnp.minimum(xa, 0.0) - jnp.log(1.0 + jnp.exp(-jnp.abs(xa)))) * (1.0 / GLA_TAU)
    for c in range(GLA_KEY // COL_GROUP):
        k_s[:, cols(c)] = proj(OFF_K, c)

    ri = lax.broadcasted_iota(jnp.int32, (chunk, chunk), 0)
    ci = lax.broadcasted_iota(jnp.int32, (chunk, chunk), 1)
    tril = ri >= ci
    tmat = tril.astype(BF16)
    bmin = jnp.zeros((1, GLA_KEY), F32)
    for idx in range(n_seq * nchunk):
        la_c = la[idx * chunk:(idx + 1) * chunk, :]
        hi = la_c.astype(BF16)
        lo = (la_c - hi.astype(F32)).astype(BF16)
        b_c = _dot(tmat, hi) + _dot(tmat, lo)
        b_s[idx * chunk:(idx + 1) * chunk, :] = b_c
        bmin = jnp.minimum(bmin, b_c[chunk - 1:chunk, :])
    safe = jnp.min(bmin) > -SAFE_LOG_DECAY

    for c in range(GLA_VAL // COL_GROUP):
        v_s[:, cols(c)] = proj(OFF_V, c).astype(BF16)
        g = proj(OFF_G, c)
        sg_s[:, cols(c)] = (g * _sigmoid(g)).astype(BF16)

    gng = gng_ref[...]
    n_idx = n_seq * nchunk

    def loop(body):
        for idx in range(n_idx):
            body(idx * chunk, idx // nchunk)

    def intra_all():
        pending = []

        def apply(r0, h, a):
            vsl = slice(h * GLA_DV, (h + 1) * GLA_DV)
            o_s[pl.ds(r0, chunk), vsl] = _dot(a, v_s[pl.ds(r0, chunk), vsl])

        for idx in range(n_idx):
            r0 = idx * chunk
            q = q_s[pl.ds(r0, chunk), :]
            k = k_s[pl.ds(r0, chunk), :]
            b = b_s[pl.ds(r0, chunk), :]
            bmid = b[chunk // 2 - 1:chunk // 2, :]
            qs = (q * jnp.exp(b - bmid)).astype(BF16)
            ks = (k * jnp.exp(bmid - b)).astype(BF16)
            for h in range(GLA_HEADS):
                ksl = slice(h * GLA_DK, (h + 1) * GLA_DK)
                a = jnp.where(tril, _dot_nt(qs[:, ksl], ks[:, ksl]), 0.0).astype(BF16)
                pending.append((r0, h, a))
                if len(pending) > INTRA_SKEW:
                    apply(*pending.pop(0))
        for item in pending:
            apply(*item)

    def conv_group(c):
        u = proj(OFF_CC, c) * proj(OFF_CH, c)
        ubuf_s[:, SUBLANES:, cols(c)] = u.reshape(n_seq, tbl, COL_GROUP)
        conv = (ubuf_s[:, SUBLANES - 2:SUBLANES - 2 + tbl, cols(c)] * convw_ref[0, 0:1, cols(c)]
                + ubuf_s[:, SUBLANES - 1:SUBLANES - 1 + tbl, cols(c)] * convw_ref[0, 1:2, cols(c)]
                + u.reshape(n_seq, tbl, COL_GROUP) * convw_ref[0, 2:3, cols(c)]
                + convb_ref[:, cols(c)])
        cb = proj(OFF_CB, c)
        ub_s[:, cols(c)] = (cb.reshape(n_seq, tbl, COL_GROUP) * conv).reshape(
            rows, COL_GROUP).astype(BF16)

    intra_all()

    @pl.when(jnp.logical_not(safe))
    def _pairwise():
        hr = lax.broadcasted_iota(jnp.int32, (GLA_KEY, GLA_VAL), 0) // GLA_DK
        hc = lax.broadcasted_iota(jnp.int32, (GLA_KEY, GLA_VAL), 1) // GLA_DV
        headsel = (hr == hc).astype(BF16)
        rowid = lax.broadcasted_iota(jnp.int32, (chunk, 1), 0)

        def cbody(idx, carry):
            r0 = pl.multiple_of(idx * chunk, chunk)
            q = q_s[pl.ds(r0, chunk), :]
            k = k_s[pl.ds(r0, chunk), :]
            b = b_s[pl.ds(r0, chunk), :]
            vf = v_s[pl.ds(r0, chunk), :].astype(F32)

            def jbody(jj, acc):
                pick = rowid == jj
                kj = jnp.sum(jnp.where(pick, k, 0.0), axis=0, keepdims=True)
                bj = jnp.sum(jnp.where(pick, b, 0.0), axis=0, keepdims=True)
                vj = jnp.sum(jnp.where(pick, vf, 0.0), axis=0, keepdims=True)
                dm = q * kj * jnp.exp(jnp.minimum(b - bj, 0.0))
                dm = jnp.where(rowid >= jj, dm, 0.0)
                return acc + _dot(dm.astype(BF16), headsel) * vj

            o_s[pl.ds(r0, chunk), :] = lax.fori_loop(
                0, chunk, jbody, jnp.zeros((chunk, GLA_VAL), F32))
            return carry

        lax.fori_loop(0, n_idx, cbody, 0)

    def state_step(r0, s):
        q = q_s[pl.ds(r0, chunk), :]
        k = k_s[pl.ds(r0, chunk), :]
        b = b_s[pl.ds(r0, chunk), :]
        v = v_s[pl.ds(r0, chunk), :]
        blast = b[chunk - 1:chunk, :]
        qb = (q * jnp.exp(b)).astype(BF16)
        kb = (k * jnp.exp(blast - b)).astype(BF16)
        dec = jnp.exp(blast)
        for h in range(GLA_HEADS):
            ksl = slice(h * GLA_DK, (h + 1) * GLA_DK)
            vsl = slice(h * GLA_DV, (h + 1) * GLA_DV)
            st = st_s[s, h]
            o_h = o_s[pl.ds(r0, chunk), vsl] + _dot_nt(qb[:, ksl], st.astype(BF16))
            st_s[s, h] = dec[:, ksl] * st + _dot_tn(v[:, vsl], kb[:, ksl])
            o_h = o_h * lax.rsqrt(jnp.mean(o_h * o_h, axis=-1, keepdims=True) + EPS) * gng
            og_s[pl.ds(r0, chunk), vsl] = (
                o_h * sg_s[pl.ds(r0, chunk), vsl].astype(F32)).astype(BF16)

    loop(state_step)

    for c in range(CONV_DIM // COL_GROUP):
        conv_group(c)
    tail = ubuf_s[:, tbl + SUBLANES - 2:tbl + SUBLANES, :]
    ubuf_s[:, SUBLANES - 2:SUBLANES, :] = tail
    cout_ref[...] = tail
    for c in range(D_MODEL // COL_GROUP):
        yb = _dot(ub_s[...], wconv_ref[:, cols(c)])
        yb_s[:, cols(c)] = (_sigmoid(proj(OFF_GB, c)) * yb).astype(BF16)

    for src, dst in zip(cast_in, cast_out):
        dst[...] = src[...].astype(BF16)

    for c in range(D_MODEL // COL_GROUP):
        ya = _dot(og_s[...], wgla_ref[:, cols(c)])
        merged = _sigmoid(proj(OFF_GA, c)) * ya + yb_s[:, cols(c)].astype(F32)
        ub_s[:, cols(c)] = merged.astype(BF16)
    for c in range(D_MODEL // COL_GROUP):
        m = _dot(ub_s[...], wo_ref[:, cols(c)])
        x1_ref[:, :, cols(c)] = (x_ref[:, :, cols(c)]
                                 + g1[:, :, cols(c)] * m.reshape(n_seq, tbl, COL_GROUP))

    @pl.when(j == nj - 1)
    def _final_state():
        for s in range(n_seq):
            for h in range(GLA_HEADS):
                sout_ref[s, h] = st_s[s, h].T


def _const_spec(shape):
    nd = len(shape)
    return pl.BlockSpec(shape, lambda i, j: (0,) * nd, pipeline_mode=pl.Buffered(1))


BF16_SUBLANES = 16


def _cast_blocking(n_rows, n_steps):
    for n_blocks in range(n_steps, 0, -1):
        if (n_steps % n_blocks == 0 and n_rows % n_blocks == 0
                and (n_rows // n_blocks) % BF16_SUBLANES == 0):
            return n_rows // n_blocks, n_steps // n_blocks
    raise ValueError((n_rows, n_steps))


def _mixer(x, mod, weights, *, n_seq, tbl, chunk, name, carry=None, cast=()):
    n_streams, length, _ = x.shape
    assert n_streams % n_seq == 0 and length % tbl == 0
    grid = (n_streams // n_seq, length // tbl)
    rows = n_seq * tbl
    (n1g, win, walpha, balpha, gng, wgla, convw, convb, wconv, wo) = weights
    carried = carry is not None
    kern = functools.partial(_mixer_kernel, n_seq=n_seq, tbl=tbl, chunk=chunk,
                             n_cast=len(cast), carried=carried)
    carry_specs = [
        pl.BlockSpec((n_seq, GLA_HEADS, GLA_DK, GLA_DV), lambda i, j: (i, 0, 0, 0)),
        pl.BlockSpec((n_seq, CONV_W - 1, CONV_DIM), lambda i, j: (i, 0, 0)),
    ] if carried else []
    cast_specs = []
    for w in cast:
        rb, steps = _cast_blocking(w.shape[0], grid[0] * grid[1])
        cast_specs.append(pl.BlockSpec(
            (rb, w.shape[1]), lambda i, j, steps=steps: ((i * grid[1] + j) // steps, 0)))
    in_specs = [
        pl.BlockSpec((n_seq, tbl, D_MODEL), lambda i, j: (i, j, 0)),
        pl.BlockSpec((n_seq, N_MOD, D_MODEL), lambda i, j: (i, 0, 0)),
        _const_spec(n1g.shape), _const_spec(win.shape), _const_spec(walpha.shape),
        _const_spec(balpha.shape), _const_spec(gng.shape), _const_spec(wgla.shape),
        _const_spec(convw.shape), _const_spec(convb.shape), _const_spec(wconv.shape),
        _const_spec(wo.shape),
    ] + carry_specs + cast_specs
    out_specs = [
        pl.BlockSpec((n_seq, tbl, D_MODEL), lambda i, j: (i, j, 0)),
        pl.BlockSpec((n_seq, GLA_HEADS, GLA_DK, GLA_DV), lambda i, j: (i, 0, 0, 0)),
        pl.BlockSpec((n_seq, CONV_W - 1, CONV_DIM), lambda i, j: (i, 0, 0)),
    ] + cast_specs
    out_shape = [
        jax.ShapeDtypeStruct(x.shape, F32),
        jax.ShapeDtypeStruct((n_streams, GLA_HEADS, GLA_DK, GLA_DV), F32),
        jax.ShapeDtypeStruct((n_streams, CONV_W - 1, CONV_DIM), F32),
    ] + [jax.ShapeDtypeStruct(w.shape, BF16) for w in cast]
    scratch = [
        pltpu.VMEM((rows, D_MODEL), BF16),
        pltpu.VMEM((rows, GLA_KEY), F32),
        pltpu.VMEM((rows, GLA_KEY), F32),
        pltpu.VMEM((rows, GLA_VAL), BF16),
        pltpu.VMEM((rows, GLA_VAL), BF16),
        pltpu.VMEM((rows, GLA_KEY), F32),
        pltpu.VMEM((n_seq, tbl + SUBLANES, CONV_DIM), F32),
        pltpu.VMEM((rows, D_MODEL), BF16),
        pltpu.VMEM((rows, D_MODEL), BF16),
        pltpu.VMEM((rows, GLA_VAL), F32),
        pltpu.VMEM((rows, GLA_VAL), BF16),
        pltpu.VMEM((n_seq, GLA_HEADS, GLA_DV, GLA_DK), F32),
    ]
    return pl.pallas_call(
        kern, grid=grid, in_specs=in_specs, out_specs=out_specs, out_shape=out_shape,
        scratch_shapes=scratch,
        compiler_params=pltpu.CompilerParams(
            dimension_semantics=("arbitrary", "arbitrary"), vmem_limit_bytes=VMEM_LIMIT),
        name=name,
    )(x, mod, n1g, win, walpha, balpha, gng, wgla, convw, convb, wconv, wo,
      *(carry or ()), *cast)


FF_GROUP = 256


def _ffn_kernel(x_ref, mod_ref, n2g_ref, win_ref, wout_ref, nfg_ref, y_ref, hb_s, act_s,
                *, n_seq, tbl, sub):
    sh2 = mod_ref[:, 3:4, :]
    sc2 = mod_ref[:, 4:5, :]
    g2 = mod_ref[:, 5:6, :]
    for t0 in range(0, tbl, sub):
        rows = n_seq * sub
        r0 = (t0 // sub) * rows
        x3 = x_ref[:, t0:t0 + sub, :]
        ms = jnp.mean(x3 * x3, axis=-1, keepdims=True)
        h3 = x3 * lax.rsqrt(ms + EPS) * n2g_ref[...] * (1.0 + sc2) + sh2
        hb_s[r0:r0 + rows, :] = h3.reshape(rows, D_MODEL).astype(BF16)
        hb = hb_s[r0:r0 + rows, :]
        for c in range(D_FF // FF_GROUP):
            gt = _dot(hb, win_ref[:, c * FF_GROUP:(c + 1) * FF_GROUP])
            up = _dot(hb, win_ref[:, D_FF + c * FF_GROUP:D_FF + (c + 1) * FF_GROUP])
            act_s[r0:r0 + rows, c * FF_GROUP:(c + 1) * FF_GROUP] = (
                gt * _sigmoid(gt) * up).astype(BF16)
        f = _dot(act_s[r0:r0 + rows, :], wout_ref[...])
        x2 = x3 + g2 * f.reshape(n_seq, sub, D_MODEL)
        ms2 = jnp.mean(x2 * x2, axis=-1, keepdims=True)
        y_ref[:, t0:t0 + sub, :] = x2 * lax.rsqrt(ms2 + EPS) * nfg_ref[...]


def _ffn(x, mod, n2g, win, wout, nfg, *, n_seq, tbl, sub, name):
    n_streams, length, _ = x.shape
    assert n_streams % n_seq == 0 and length % tbl == 0
    grid = (n_streams // n_seq, length // tbl)
    rows = n_seq * tbl
    kern = functools.partial(_ffn_kernel, n_seq=n_seq, tbl=tbl, sub=sub)
    return pl.pallas_call(
        kern, grid=grid,
        in_specs=[
            pl.BlockSpec((n_seq, tbl, D_MODEL), lambda i, j: (i, j, 0)),
            pl.BlockSpec((n_seq, N_MOD, D_MODEL), lambda i, j: (i, 0, 0)),
            _const_spec(n2g.shape), _const_spec(win.shape), _const_spec(wout.shape),
            _const_spec(nfg.shape),
        ],
        out_specs=pl.BlockSpec((n_seq, tbl, D_MODEL), lambda i, j: (i, j, 0)),
        out_shape=jax.ShapeDtypeStruct(x.shape, F32),
        scratch_shapes=[pltpu.VMEM((rows, D_MODEL), BF16), pltpu.VMEM((rows, D_FF), BF16)],
        compiler_params=pltpu.CompilerParams(
            dimension_semantics=("arbitrary", "arbitrary"), vmem_limit_bytes=VMEM_LIMIT),
        name=name,
    )(x, mod, n2g, win, wout, nfg)


def kernel(x_prompt, x_sample, c_prompt, c_sample, state_gla, cache_conv, w_mod, b_mod, norm1_g,
           w_in, w_alpha, b_alpha, gla_norm_g, w_gla_out, conv_w, conv_b, w_conv_out, w_o,
           norm2_g, w_ffn_in, w_ffn_out, norm_f_g):
    bs = x_sample.shape[0]
    depth = w_mod.shape[0]
    assert depth == 1
    l = 0
    assert w_in.shape[1:] == (D_MODEL, IN_DIM)
    mod_p, mod_s, win_t, wgla, wconv, wo = _modulation(
        c_prompt, c_sample, w_mod[l], b_mod[l],
        cast=(jnp.swapaxes(w_in[l], 0, 1), w_gla_out[l], w_conv_out[l], w_o[l]))
    walpha = w_alpha[l].astype(BF16)
    mix_w = (norm1_g[l].reshape(1, -1), win_t, walpha, b_alpha[l].reshape(1, -1),
             gla_norm_g[l].reshape(1, -1), wgla, conv_w, conv_b[l].reshape(1, -1), wconv, wo)

    x1_p, st_p, cv_p, wffn_in, wffn_out = _mixer(
        x_prompt, mod_p, mix_w, n_seq=1, tbl=512, chunk=256, name="mixer_prompt",
        cast=(w_ffn_in[l], w_ffn_out[l]))
    ffn_w = (norm2_g[l].reshape(1, -1), wffn_in, wffn_out, norm_f_g.reshape(1, -1))
    x1_s, st_s, cv_s = _mixer(x_sample, mod_s, mix_w, carry=(state_gla[l], cache_conv[l]),
                              n_seq=8, tbl=x_sample.shape[1], chunk=x_sample.shape[1],
                              name="mixer_sample")
    y_p = _ffn(x1_p, mod_p, *ffn_w, n_seq=1, tbl=1024, sub=256, name="ffn_prompt")
    y_s = _ffn(x1_s, mod_s, *ffn_w, n_seq=bs, tbl=x_sample.shape[1], sub=x_sample.shape[1],
               name="ffn_sample")
    return (y_p, y_s, st_p[None], cv_p[None], st_s[None], cv_s[None])
```

```python
import functools

import jax
import jax.numpy as jnp
from jax import lax
from jax.experimental import pallas as pl
from jax.experimental.pallas import tpu as pltpu

F32 = jnp.float32
BF16 = jnp.bfloat16

D_MODEL = 1024
GLA_HEADS = 4
GLA_DK = 128
GLA_DV = 256
GLA_KEY = GLA_HEADS * GLA_DK
GLA_VAL = GLA_HEADS * GLA_DV
GLA_RANK = 16
GLA_TAU = 16.0
CONV_DIM = D_MODEL
CONV_W = 3
D_FF = 2816
N_MOD = 6
EPS = 1e-6

COL_GROUP = 256
SUBLANES = 8

OFF_Q = 0
OFF_K = OFF_Q + GLA_KEY
OFF_V = OFF_K + GLA_KEY
OFF_G = OFF_V + GLA_VAL
OFF_A = OFF_G + GLA_VAL
OFF_CB = OFF_A + GLA_RANK
OFF_CC = OFF_CB + CONV_DIM
OFF_CH = OFF_CC + CONV_DIM
OFF_GA = OFF_CH + CONV_DIM
OFF_GB = OFF_GA + D_MODEL
IN_DIM = OFF_GB + D_MODEL

SAFE_LOG_DECAY = 60.0

INTRA_SKEW = 4

VMEM_LIMIT = 60 * 1024 * 1024


def _dot(a, b):
    return jnp.dot(a, b, preferred_element_type=F32)


def _dot_nt(a, b):
    return lax.dot_general(a, b, (((1,), (1,)), ((), ())), preferred_element_type=F32)


def _dot_tn(a, b):
    return lax.dot_general(a, b, (((0,), (0,)), ((), ())), preferred_element_type=F32)


def _sigmoid(x):
    return 1.0 / (1.0 + jnp.exp(-x))


def _mod_kernel(*refs):
    cp_ref, cs_ref, w_ref, b_ref = refs[:4]
    n_cast = (len(refs) - 6) // 2
    cast_in = refs[4:4 + n_cast]
    op_ref, os_ref = refs[4 + n_cast:6 + n_cast]
    cast_out = refs[6 + n_cast:]
    j = pl.program_id(0)
    w = w_ref[...].astype(BF16)
    vp = _dot(cp_ref[...].astype(BF16), w) + b_ref[...]
    vs = _dot(cs_ref[...].astype(BF16), w) + b_ref[...]
    for c in range(N_MOD):
        @pl.when(j == c)
        def _store():
            op_ref[:, c, :] = vp
            os_ref[:, c, :] = vs
    for src, dst in zip(cast_in, cast_out):
        dst[...] = src[...].astype(BF16)


def _modulation(c_prompt, c_sample, w_mod, b_mod, cast=()):
    bp, bs = c_prompt.shape[0], c_sample.shape[0]
    cast_specs = []
    for w in cast:
        n_blocks = max(n for n in range(1, N_MOD + 1)
                       if w.shape[0] % n == 0 and (w.shape[0] // n) % BF16_SUBLANES == 0)
        cast_specs.append(pl.BlockSpec(
            (w.shape[0] // n_blocks, w.shape[1]),
            lambda j, n_blocks=n_blocks: (jnp.minimum(j, n_blocks - 1), 0)))
    return pl.pallas_call(
        _mod_kernel,
        grid=(N_MOD,),
        in_specs=[
            pl.BlockSpec((bp, D_MODEL), lambda j: (0, 0)),
            pl.BlockSpec((bs, D_MODEL), lambda j: (0, 0)),
            pl.BlockSpec((D_MODEL, D_MODEL), lambda j: (0, j)),
            pl.BlockSpec((1, D_MODEL), lambda j: (0, j)),
        ] + cast_specs,
        out_specs=[
            pl.BlockSpec((bp, N_MOD, D_MODEL), lambda j: (0, 0, 0)),
            pl.BlockSpec((bs, N_MOD, D_MODEL), lambda j: (0, 0, 0)),
        ] + cast_specs,
        out_shape=[jax.ShapeDtypeStruct((bp, N_MOD, D_MODEL), F32),
                   jax.ShapeDtypeStruct((bs, N_MOD, D_MODEL), F32)]
        + [jax.ShapeDtypeStruct(w.shape, BF16) for w in cast],
        compiler_params=pltpu.CompilerParams(
            dimension_semantics=("arbitrary",), vmem_limit_bytes=VMEM_LIMIT),
        name="adaln_mod",
    )(c_prompt, c_sample, w_mod, b_mod.reshape(1, -1), *cast)


N_MIXER_IN = 12
N_MIXER_OUT = 3


def _mixer_kernel(*refs, n_seq, tbl, chunk, n_cast, carried):
    (x_ref, mod_ref, n1g_ref, win_ref, walpha_ref, balpha_ref,
     gng_ref, wgla_ref, convw_ref, convb_ref, wconv_ref, wo_ref) = refs[:N_MIXER_IN]
    refs = refs[N_MIXER_IN:]
    if carried:
        (s0_ref, cprev_ref), refs = refs[:2], refs[2:]
    cast_in, refs = refs[:n_cast], refs[n_cast:]
    (x1_ref, sout_ref, cout_ref), refs = refs[:N_MIXER_OUT], refs[N_MIXER_OUT:]
    cast_out, refs = refs[:n_cast], refs[n_cast:]
    (hb_s, q_s, k_s, v_s, sg_s, b_s, ubuf_s, ub_s, yb_s, o_s, og_s, st_s) = refs
    rows = n_seq * tbl
    nchunk = tbl // chunk
    j = pl.program_id(1)
    nj = pl.num_programs(1)

    @pl.when(j == 0)
    def _init():
        for s in range(n_seq):
            for h in range(GLA_HEADS):
                st_s[s, h] = s0_ref[s, h].T if carried else jnp.zeros((GLA_DV, GLA_DK), F32)
        ubuf_s[:, SUBLANES - 2:SUBLANES, :] = (
            cprev_ref[...] if carried else jnp.zeros((n_seq, CONV_W - 1, CONV_DIM), F32))

    sh1 = mod_ref[:, 0:1, :]
    sc1 = mod_ref[:, 1:2, :]
    g1 = mod_ref[:, 2:3, :]
    x3 = x_ref[...]
    ms = jnp.mean(x3 * x3, axis=-1, keepdims=True)
    h3 = x3 * lax.rsqrt(ms + EPS) * n1g_ref[...] * (1.0 + sc1) + sh1
    hb_s[...] = h3.reshape(rows, D_MODEL).astype(BF16)

    def proj(off, c, n=COL_GROUP):
        return _dot_nt(hb_s[...], win_ref[off + c * n:off + (c + 1) * n, :])

    def cols(c):
        return slice(c * COL_GROUP, (c + 1) * COL_GROUP)

    za = proj(OFF_A, 0, GLA_RANK)
    for c in range(GLA_KEY // COL_GROUP):
        q_s[:, cols(c)] = proj(OFF_Q, c) * (GLA_DK ** -0.5)
    xa = _dot(za.astype(BF16), walpha_ref[...]) + balpha_ref[...]
    la = (jnp.minimum(xa, 0.0) - jnp.log(1.0 + jnp.exp(-jnp.abs(xa)))) * (1.0 / GLA_TAU)
    for c in range(GLA_KEY // COL_GROUP):
        k_s[:, cols(c)] = proj(OFF_K, c)

    ri = lax.broadcasted_iota(jnp.int32, (chunk, chunk), 0)
    ci = lax.broadcasted_iota(jnp.int32, (chunk, chunk), 1)
    tril = ri >= ci
    tmat = tril.astype(BF16)
    bmin = jnp.zeros((1, GLA_KEY), F32)
    for idx in range(n_seq * nchunk):
        la_c = la[idx * chunk:(idx + 1) * chunk, :]
        hi = la_c.astype(BF16)
        lo = (la_c - hi.astype(F32)).astype(BF16)
        b_c = _dot(tmat, hi) + _dot(tmat, lo)
        b_s[idx * chunk:(idx + 1) * chunk, :] = b_c
        bmin = jnp.minimum(bmin, b_c[chunk - 1:chunk, :])
    safe = jnp.min(bmin) > -SAFE_LOG_DECAY

    for c in range(GLA_VAL // COL_GROUP):
        v_s[:, cols(c)] = proj(OFF_V, c).astype(BF16)
        g = proj(OFF_G, c)
        sg_s[:, cols(c)] = (g * _sigmoid(g)).astype(BF16)

    gng = gng_ref[...]
    n_idx = n_seq * nchunk

    def loop(body):
        for idx in range(n_idx):
            body(idx * chunk, idx // nchunk)

    def intra_all():
        pending = []

        def apply(r0, h, a):
            vsl = slice(h * GLA_DV, (h + 1) * GLA_DV)
            o_s[pl.ds(r0, chunk), vsl] = _dot(a, v_s[pl.ds(r0, chunk), vsl])

        for idx in range(n_idx):
            r0 = idx * chunk
            q = q_s[pl.ds(r0, chunk), :]
            k = k_s[pl.ds(r0, chunk), :]
            b = b_s[pl.ds(r0, chunk), :]
            bmid = b[chunk // 2 - 1:chunk // 2, :]
            qs = (q * jnp.exp(b - bmid)).astype(BF16)
            ks = (k * jnp.exp(bmid - b)).astype(BF16)
            for h in range(GLA_HEADS):
                ksl = slice(h * GLA_DK, (h + 1) * GLA_DK)
                a = jnp.where(tril, _dot_nt(qs[:, ksl], ks[:, ksl]), 0.0).astype(BF16)
                pending.append((r0, h, a))
                if len(pending) > INTRA_SKEW:
                    apply(*pending.pop(0))
        for item in pending:
            apply(*item)

    def conv_group(c):
        u = proj(OFF_CC, c) * proj(OFF_CH, c)
        ubuf_s[:, SUBLANES:, cols(c)] = u.reshape(n_seq, tbl, COL_GROUP)
        conv = (ubuf_s[:, SUBLANES - 2:SUBLANES - 2 + tbl, cols(c)] * convw_ref[0, 0:1, cols(c)]
                + ubuf_s[:, SUBLANES - 1:SUBLANES - 1 + tbl, cols(c)] * convw_ref[0, 1:2, cols(c)]
                + u.reshape(n_seq, tbl, COL_GROUP) * convw_ref[0, 2:3, cols(c)]
                + convb_ref[:, cols(c)])
        cb = proj(OFF_CB, c)
        ub_s[:, cols(c)] = (cb.reshape(n_seq, tbl, COL_GROUP) * conv).reshape(
            rows, COL_GROUP).astype(BF16)

    intra_all()

    @pl.when(jnp.logical_not(safe))
    def _pairwise():
        hr = lax.broadcasted_iota(jnp.int32, (GLA_KEY, GLA_VAL), 0) // GLA_DK
        hc = lax.broadcasted_iota(jnp.int32, (GLA_KEY, GLA_VAL), 1) // GLA_DV
        headsel = (hr == hc).astype(BF16)
        rowid = lax.broadcasted_iota(jnp.int32, (chunk, 1), 0)

        def cbody(idx, carry):
            r0 = pl.multiple_of(idx * chunk, chunk)
            q = q_s[pl.ds(r0, chunk), :]
            k = k_s[pl.ds(r0, chunk), :]
            b = b_s[pl.ds(r0, chunk), :]
            vf = v_s[pl.ds(r0, chunk), :].astype(F32)

            def jbody(jj, acc):
                pick = rowid == jj
                kj = jnp.sum(jnp.where(pick, k, 0.0), axis=0, keepdims=True)
                bj = jnp.sum(jnp.where(pick, b, 0.0), axis=0, keepdims=True)
                vj = jnp.sum(jnp.where(pick, vf, 0.0), axis=0, keepdims=True)
                dm = q * kj * jnp.exp(jnp.minimum(b - bj, 0.0))
                dm = jnp.where(rowid >= jj, dm, 0.0)
                return acc + _dot(dm.astype(BF16), headsel) * vj

            o_s[pl.ds(r0, chunk), :] = lax.fori_loop(
                0, chunk, jbody, jnp.zeros((chunk, GLA_VAL), F32))
            return carry

        lax.fori_loop(0, n_idx, cbody, 0)

    def state_step(r0, s):
        q = q_s[pl.ds(r0, chunk), :]
        k = k_s[pl.ds(r0, chunk), :]
        b = b_s[pl.ds(r0, chunk), :]
        v = v_s[pl.ds(r0, chunk), :]
        blast = b[chunk - 1:chunk, :]
        qb = (q * jnp.exp(b)).astype(BF16)
        kb = (k * jnp.exp(blast - b)).astype(BF16)
        dec = jnp.exp(blast)
        for h in range(GLA_HEADS):
            ksl = slice(h * GLA_DK, (h + 1) * GLA_DK)
            vsl = slice(h * GLA_DV, (h + 1) * GLA_DV)
            st = st_s[s, h]
            o_h = o_s[pl.ds(r0, chunk), vsl] + _dot_nt(qb[:, ksl], st.astype(BF16))
            st_s[s, h] = dec[:, ksl] * st + _dot_tn(v[:, vsl], kb[:, ksl])
            o_h = o_h * lax.rsqrt(jnp.mean(o_h * o_h, axis=-1, keepdims=True) + EPS) * gng
            og_s[pl.ds(r0, chunk), vsl] = (
                o_h * sg_s[pl.ds(r0, chunk), vsl].astype(F32)).astype(BF16)

    loop(state_step)

    for c in range(CONV_DIM // COL_GROUP):
        conv_group(c)
    tail = ubuf_s[:, tbl + SUBLANES - 2:tbl + SUBLANES, :]
    ubuf_s[:, SUBLANES - 2:SUBLANES, :] = tail
    cout_ref[...] = tail
    for c in range(D_MODEL // COL_GROUP):
        yb = _dot(ub_s[...], wconv_ref[:, cols(c)])
        yb_s[:, cols(c)] = (_sigmoid(proj(OFF_GB, c)) * yb).astype(BF16)

    for src, dst in zip(cast_in, cast_out):
        dst[...] = src[...].astype(BF16)

    for c in range(D_MODEL // COL_GROUP):
        ya = _dot(og_s[...], wgla_ref[:, cols(c)])
        merged = _sigmoid(proj(OFF_GA, c)) * ya + yb_s[:, cols(c)].astype(F32)
        ub_s[:, cols(c)] = merged.astype(BF16)
    for c in range(D_MODEL // COL_GROUP):
        m = _dot(ub_s[...], wo_ref[:, cols(c)])
        x1_ref[:, :, cols(c)] = (x_ref[:, :, cols(c)]
                                 + g1[:, :, cols(c)] * m.reshape(n_seq, tbl, COL_GROUP))

    @pl.when(j == nj - 1)
    def _final_state():
        for s in range(n_seq):
            for h in range(GLA_HEADS):
                sout_ref[s, h] = st_s[s, h].T


def _const_spec(shape):
    nd = len(shape)
    return pl.BlockSpec(shape, lambda i, j: (0,) * nd, pipeline_mode=pl.Buffered(1))


BF16_SUBLANES = 16


def _cast_blocking(n_rows, n_steps):
    for n_blocks in range(n_steps, 0, -1):
        if (n_steps % n_blocks == 0 and n_rows % n_blocks == 0
                and (n_rows // n_blocks) % BF16_SUBLANES == 0):
            return n_rows // n_blocks, n_steps // n_blocks
    raise ValueError((n_rows, n_steps))


def _mixer(x, mod, weights, *, n_seq, tbl, chunk, name, carry=None, cast=()):
    n_streams, length, _ = x.shape
    assert n_streams % n_seq == 0 and length % tbl == 0
    grid = (n_streams // n_seq, length // tbl)
    rows = n_seq * tbl
    (n1g, win, walpha, balpha, gng, wgla, convw, convb, wconv, wo) = weights
    carried = carry is not None
    kern = functools.partial(_mixer_kernel, n_seq=n_seq, tbl=tbl, chunk=chunk,
                             n_cast=len(cast), carried=carried)
    carry_specs = [
        pl.BlockSpec((n_seq, GLA_HEADS, GLA_DK, GLA_DV), lambda i, j: (i, 0, 0, 0)),
        pl.BlockSpec((n_seq, CONV_W - 1, CONV_DIM), lambda i, j: (i, 0, 0)),
    ] if carried else []
    cast_specs = []
    for w in cast:
        rb, steps = _cast_blocking(w.shape[0], grid[0] * grid[1])
        cast_specs.append(pl.BlockSpec(
            (rb, w.shape[1]), lambda i, j, steps=steps: ((i * grid[1] + j) // steps, 0)))
    in_specs = [
        pl.BlockSpec((n_seq, tbl, D_MODEL), lambda i, j: (i, j, 0)),
        pl.BlockSpec((n_seq, N_MOD, D_MODEL), lambda i, j: (i, 0, 0)),
        _const_spec(n1g.shape), _const_spec(win.shape), _const_spec(walpha.shape),
        _const_spec(balpha.shape), _const_spec(gng.shape), _const_spec(wgla.shape),
        _const_spec(convw.shape), _const_spec(convb.shape), _const_spec(wconv.shape),
        _const_spec(wo.shape),
    ] + carry_specs + cast_specs
    out_specs = [
        pl.BlockSpec((n_seq, tbl, D_MODEL), lambda i, j: (i, j, 0)),
        pl.BlockSpec((n_seq, GLA_HEADS, GLA_DK, GLA_DV), lambda i, j: (i, 0, 0, 0)),
        pl.BlockSpec((n_seq, CONV_W - 1, CONV_DIM), lambda i, j: (i, 0, 0)),
    ] + cast_specs
    out_shape = [
        jax.ShapeDtypeStruct(x.shape, F32),
        jax.ShapeDtypeStruct((n_streams, GLA_HEADS, GLA_DK, GLA_DV), F32),
        jax.ShapeDtypeStruct((n_streams, CONV_W - 1, CONV_DIM), F32),
    ] + [jax.ShapeDtypeStruct(w.shape, BF16) for w in cast]
    scratch = [
        pltpu.VMEM((rows, D_MODEL), BF16),
        pltpu.VMEM((rows, GLA_KEY), F32),
        pltpu.VMEM((rows, GLA_KEY), F32),
        pltpu.VMEM((rows, GLA_VAL), BF16),
        pltpu.VMEM((rows, GLA_VAL), BF16),
        pltpu.VMEM((rows, GLA_KEY), F32),
        pltpu.VMEM((n_seq, tbl + SUBLANES, CONV_DIM), F32),
        pltpu.VMEM((rows, D_MODEL), BF16),
        pltpu.VMEM((rows, D_MODEL), BF16),
        pltpu.VMEM((rows, GLA_VAL), F32),
        pltpu.VMEM((rows, GLA_VAL), BF16),
        pltpu.VMEM((n_seq, GLA_HEADS, GLA_DV, GLA_DK), F32),
    ]
    return pl.pallas_call(
        kern, grid=grid, in_specs=in_specs, out_specs=out_specs, out_shape=out_shape,
        scratch_shapes=scratch,
        compiler_params=pltpu.CompilerParams(
            dimension_semantics=("arbitrary", "arbitrary"), vmem_limit_bytes=VMEM_LIMIT),
        name=name,
    )(x, mod, n1g, win, walpha, balpha, gng, wgla, convw, convb, wconv, wo,
      *(carry or ()), *cast)


FF_GROUP = 256


def _ffn_kernel(x_ref, mod_ref, n2g_ref, win_ref, wout_ref, nfg_ref, y_ref, hb_s, act_s,
                *, n_seq, tbl, sub):
    sh2 = mod_ref[:, 3:4, :]
    sc2 = mod_ref[:, 4:5, :]
    g2 = mod_ref[:, 5:6, :]
    for t0 in range(0, tbl, sub):
        rows = n_seq * sub
        r0 = (t0 // sub) * rows
        x3 = x_ref[:, t0:t0 + sub, :]
        ms = jnp.mean(x3 * x3, axis=-1, keepdims=True)
        h3 = x3 * lax.rsqrt(ms + EPS) * n2g_ref[...] * (1.0 + sc2) + sh2
        hb_s[r0:r0 + rows, :] = h3.reshape(rows, D_MODEL).astype(BF16)
        hb = hb_s[r0:r0 + rows, :]
        for c in range(D_FF // FF_GROUP):
            gt = _dot(hb, win_ref[:, c * FF_GROUP:(c + 1) * FF_GROUP])
            up = _dot(hb, win_ref[:, D_FF + c * FF_GROUP:D_FF + (c + 1) * FF_GROUP])
            act_s[r0:r0 + rows, c * FF_GROUP:(c + 1) * FF_GROUP] = (
                gt * _sigmoid(gt) * up).astype(BF16)
        f = _dot(act_s[r0:r0 + rows, :], wout_ref[...])
        x2 = x3 + g2 * f.reshape(n_seq, sub, D_MODEL)
        ms2 = jnp.mean(x2 * x2, axis=-1, keepdims=True)
        y_ref[:, t0:t0 + sub, :] = x2 * lax.rsqrt(ms2 + EPS) * nfg_ref[...]


def _ffn(x, mod, n2g, win, wout, nfg, *, n_seq, tbl, sub, name):
    n_streams, length, _ = x.shape
    assert n_streams % n_seq == 0 and length % tbl == 0
    grid = (n_streams // n_seq, length // tbl)
    rows = n_seq * tbl
    kern = functools.partial(_ffn_kernel, n_seq=n_seq, tbl=tbl, sub=sub)
    return pl.pallas_call(
        kern, grid=grid,
        in_specs=[
            pl.BlockSpec((n_seq, tbl, D_MODEL), lambda i, j: (i, j, 0)),
            pl.BlockSpec((n_seq, N_MOD, D_MODEL), lambda i, j: (i, 0, 0)),
            _const_spec(n2g.shape), _const_spec(win.shape), _const_spec(wout.shape),
            _const_spec(nfg.shape),
        ],
        out_specs=pl.BlockSpec((n_seq, tbl, D_MODEL), lambda i, j: (i, j, 0)),
        out_shape=jax.ShapeDtypeStruct(x.shape, F32),
        scratch_shapes=[pltpu.VMEM((rows, D_MODEL), BF16), pltpu.VMEM((rows, D_FF), BF16)],
        compiler_params=pltpu.CompilerParams(
            dimension_semantics=("arbitrary", "arbitrary"), vmem_limit_bytes=VMEM_LIMIT),
        name=name,
    )(x, mod, n2g, win, wout, nfg)


def kernel(x_prompt, x_sample, c_prompt, c_sample, state_gla, cache_conv, w_mod, b_mod, norm1_g,
           w_in, w_alpha, b_alpha, gla_norm_g, w_gla_out, conv_w, conv_b, w_conv_out, w_o,
           norm2_g, w_ffn_in, w_ffn_out, norm_f_g):
    bs = x_sample.shape[0]
    depth = w_mod.shape[0]
    assert depth == 1
    l = 0
    assert w_in.shape[1:] == (D_MODEL, IN_DIM)
    mod_p, mod_s, win_t, wgla, wconv, wo = _modulation(
        c_prompt, c_sample, w_mod[l], b_mod[l],
        cast=(jnp.swapaxes(w_in[l], 0, 1), w_gla_out[l], w_conv_out[l], w_o[l]))
    walpha = w_alpha[l].astype(BF16)
    mix_w = (norm1_g[l].reshape(1, -1), win_t, walpha, b_alpha[l].reshape(1, -1),
             gla_norm_g[l].reshape(1, -1), wgla, conv_w, conv_b[l].reshape(1, -1), wconv, wo)

    x1_p, st_p, cv_p, wffn_in, wffn_out = _mixer(
        x_prompt, mod_p, mix_w, n_seq=2, tbl=256, chunk=256, name="mixer_prompt",
        cast=(w_ffn_in[l], w_ffn_out[l]))
    ffn_w = (norm2_g[l].reshape(1, -1), wffn_in, wffn_out, norm_f_g.reshape(1, -1))
    x1_s, st_s, cv_s = _mixer(x_sample, mod_s, mix_w, carry=(state_gla[l], cache_conv[l]),
                              n_seq=8, tbl=x_sample.shape[1], chunk=x_sample.shape[1],
                              name="mixer_sample")
    y_p = _ffn(x1_p, mod_p, *ffn_w, n_seq=1, tbl=1024, sub=256, name="ffn_prompt")
    y_s = _ffn(x1_s, mod_s, *ffn_w, n_seq=bs, tbl=x_sample.shape[1], sub=x_sample.shape[1],
               name="ffn_sample")
    return (y_p, y_s, st_p[None], cv_p[None], st_s[None], cv_s[None])
```
